```python
import math
import jax, jax.numpy as jnp
from jax import lax
import numpy as np

D_MODEL = 1024
BATCH = 4
SEQ = 8192
DEPTH = 4

N_A_LAYERS = DEPTH // 2
N_B_LAYERS = DEPTH - N_A_LAYERS
A_HEADS = D_MODEL // 128
A_HEAD_DIM = 64
A_VALUE_DIM = 2 * A_HEAD_DIM
D_ATTN_A = A_HEADS * A_VALUE_DIM
B_HEADS = D_MODEL // 64
B_HEAD_DIM = 64
D_ATTN_B = B_HEADS * B_HEAD_DIM
N_EXPERTS = 32
TOP_K = 4
D_EXPERT = D_MODEL
SWIGLU_ALPHA = 1.702
SWIGLU_LIMIT = 7.0
Q_BLOCK = 128
EXPERT_BLOCK = 128
EPS = 1e-6
NEG_INF = -1e30

kernel_name = 'yoco_diffattn_stickbreaking_moe_adaln'


def rmsnorm(x, gain):
    xf = x.astype(jnp.float32)
    y = xf * lax.rsqrt(jnp.mean(xf * xf, axis=-1, keepdims=True) + EPS)
    return (y * gain.astype(jnp.float32)).astype(x.dtype)


def adaln(x, gain, shift, scale):
    return rmsnorm(x, gain) * (1 + scale) + shift


def alibi_slopes(n_heads):
    return jnp.asarray(np.array([2.0 ** (-8.0 * (i + 1) / n_heads) for i in range(n_heads)], dtype=np.float32))


def to_query_blocks(q):
    b, s = q.shape[0], q.shape[1]
    return jnp.moveaxis(q.reshape((b, s // Q_BLOCK, Q_BLOCK) + q.shape[2:]), 1, 0)


def from_query_blocks(o):
    o = jnp.moveaxis(o, 0, 1)
    return o.reshape((o.shape[0], o.shape[1] * o.shape[2]) + o.shape[3:])


def diff_attention(h, wqkv, wo, lam_vecs, subln, layer_idx):
    b, s, _ = h.shape
    qkv = h @ wqkv
    q, k, v = jnp.split(qkv, 3, axis=-1)
    q = q.reshape(b, s, A_HEADS, 2, A_HEAD_DIM) * (A_HEAD_DIM ** -0.5)
    k = k.reshape(b, s, A_HEADS, 2, A_HEAD_DIM)
    v = v.reshape(b, s, A_HEADS, A_VALUE_DIM)
    lam_init = 0.8 - 0.6 * math.exp(-0.3 * layer_idx)
    lv = lam_vecs.astype(jnp.float32)
    lam = jnp.exp(jnp.sum(lv[0] * lv[1])) - jnp.exp(jnp.sum(lv[2] * lv[3])) + lam_init
    slopes = alibi_slopes(A_HEADS)[:, None, None, None]
    kpos = jnp.arange(s)
    n_blk = s // Q_BLOCK

    def block(args):
        i, qi = args
        qpos = i * Q_BLOCK + jnp.arange(Q_BLOCK)
        dist = (qpos[:, None] - kpos[None, :]).astype(jnp.float32)
        sc = jnp.einsum('bqhrd,bkhrd->bhrqk', qi, k).astype(jnp.float32)
        sc = jnp.where(dist >= 0, sc - slopes * dist, NEG_INF)
        p = jax.nn.softmax(sc, axis=-1)
        attn = p[:, :, 0] - lam * p[:, :, 1]
        return jnp.einsum('bhqk,bkhe->bqhe', attn.astype(v.dtype), v)

    o = from_query_blocks(lax.map(block, (jnp.arange(n_blk), to_query_blocks(q))))
    o = rmsnorm(o, subln) * (1 - lam_init)
    return o.reshape(b, s, D_ATTN_A) @ wo


def stick_breaking_attention(h, wq, wo, k, v):
    b, s, _ = h.shape
    q = (h @ wq).reshape(b, s, B_HEADS, B_HEAD_DIM) * (B_HEAD_DIM ** -0.5)
    kpos = jnp.arange(s)
    n_blk = s // Q_BLOCK

    def block(args):
        i, qi = args
        qpos = i * Q_BLOCK + jnp.arange(Q_BLOCK)
        strict = kpos[None, :] < qpos[:, None]
        z = jnp.einsum('bqhd,bkhd->bhqk', qi, k).astype(jnp.float32)
        log_beta = jax.nn.log_sigmoid(z)
        log_1mb = jnp.where(strict, jax.nn.log_sigmoid(-z), 0.0)
        log_rem = lax.cumsum(log_1mb, axis=3, reverse=True) - log_1mb
        a = jnp.where(strict, jnp.exp(log_beta + log_rem), 0.0)
        return jnp.einsum('bhqk,bkhd->bqhd', a.astype(v.dtype), v)

    o = from_query_blocks(lax.map(block, (jnp.arange(n_blk), to_query_blocks(q))))
    return o.reshape(b, s, D_ATTN_B) @ wo


def clamped_swiglu(g, u):
    g = jnp.minimum(g, SWIGLU_LIMIT)
    u = jnp.clip(u, -SWIGLU_LIMIT, SWIGLU_LIMIT)
    return g * jax.nn.sigmoid(SWIGLU_ALPHA * g) * (u + 1)


def moe(h, rw, rb, wgu, bgu, wd, bd):
    b, s, d = h.shape
    t = b * s
    xt = h.reshape(t, d)
    logits = (xt @ rw + rb).astype(jnp.float32)
    top_vals, top_idx = lax.top_k(logits, TOP_K)
    gates = jax.nn.softmax(top_vals, axis=-1)
    n_assign = t * TOP_K
    e_flat = top_idx.reshape(n_assign)
    order = jnp.argsort(e_flat)
    sorted_e = e_flat[order]
    sorted_tok = (order // TOP_K).astype(jnp.int32)
    sorted_gate = gates.reshape(n_assign)[order]
    counts = jnp.bincount(e_flat, length=N_EXPERTS)
    padded = (counts + EXPERT_BLOCK - 1) // EXPERT_BLOCK * EXPERT_BLOCK
    start = jnp.cumsum(counts) - counts
    pend = jnp.cumsum(padded)
    pstart = pend - padded
    dest = pstart[sorted_e] + (jnp.arange(n_assign) - start[sorted_e])
    n_blocks = (n_assign + N_EXPERTS * (EXPERT_BLOCK - 1) + EXPERT_BLOCK - 1) // EXPERT_BLOCK
    n_rows = n_blocks * EXPERT_BLOCK
    row_tok = jnp.zeros((n_rows,), jnp.int32).at[dest].set(sorted_tok)
    row_gate = jnp.zeros((n_rows,), jnp.float32).at[dest].set(sorted_gate)
    block_e = jnp.minimum(jnp.searchsorted(pend, jnp.arange(n_blocks) * EXPERT_BLOCK, side='right'), N_EXPERTS - 1)

    def expert_block(args):
        tok, e = args
        gu = xt[tok] @ wgu[e] + bgu[e]
        return clamped_swiglu(gu[:, :D_EXPERT], gu[:, D_EXPERT:]) @ wd[e] + bd[e]

    ys = lax.map(expert_block, (row_tok.reshape(n_blocks, EXPERT_BLOCK), block_e)).reshape(n_rows, d)
    ys = ys * row_gate[:, None].astype(ys.dtype)
    return jax.ops.segment_sum(ys, row_tok, num_segments=t).reshape(b, s, d)


def setup_inputs(seed: int = 0) -> dict:
    key = jax.random.key(seed)
    ks = jax.random.split(key, 24)
    f32 = jnp.float32
    D = D_MODEL
    def nrm(k, shape, scale):
        return jax.random.normal(k, shape, f32) * scale
    return {
        'x': nrm(ks[0], (BATCH, SEQ, D), 1.0),
        'c': nrm(ks[1], (BATCH, D), 1.0),
        'ada_w': nrm(ks[2], (DEPTH, D, 6 * D), 0.5 * D ** -0.5),
        'ada_b': nrm(ks[3], (DEPTH, 6 * D), 0.02),
        'norm_mix': 1.0 + nrm(ks[4], (DEPTH, D), 0.02),
        'norm_moe': 1.0 + nrm(ks[5], (DEPTH, D), 0.02),
        'a_wqkv': nrm(ks[6], (N_A_LAYERS, D, 3 * D_ATTN_A), D ** -0.5),
        'a_wo': nrm(ks[7], (N_A_LAYERS, D_ATTN_A, D), D_ATTN_A ** -0.5),
        'a_lambda': nrm(ks[8], (N_A_LAYERS, 4, A_HEAD_DIM), 0.1),
        'a_subln': 1.0 + nrm(ks[9], (N_A_LAYERS, A_VALUE_DIM), 0.02),
        'kv_norm': 1.0 + nrm(ks[10], (D,), 0.02),
        'kv_ada_w': nrm(ks[11], (D, 2 * D), 0.5 * D ** -0.5),
        'kv_ada_b': nrm(ks[12], (2 * D,), 0.02),
        'kv_w': nrm(ks[13], (D, 2 * D_ATTN_B), D ** -0.5),
        'b_wq': nrm(ks[14], (N_B_LAYERS, D, D_ATTN_B), D ** -0.5),
        'b_wo': nrm(ks[15], (N_B_LAYERS, D_ATTN_B, D), D_ATTN_B ** -0.5),
        'router_w': nrm(ks[16], (DEPTH, D, N_EXPERTS), D ** -0.5),
        'router_b': nrm(ks[17], (DEPTH, N_EXPERTS), 0.01),
        'w_gate_up': nrm(ks[18], (DEPTH, N_EXPERTS, D, 2 * D_EXPERT), D ** -0.5),
        'b_gate_up': nrm(ks[19], (DEPTH, N_EXPERTS, 2 * D_EXPERT), 0.02),
        'w_down': nrm(ks[20], (DEPTH, N_EXPERTS, D_EXPERT, D), D_EXPERT ** -0.5),
        'b_down': nrm(ks[21], (DEPTH, N_EXPERTS, D), 0.02),
        'final_norm': 1.0 + nrm(ks[22], (D,), 0.02),
    }


def reference(x, c, ada_w, ada_b, norm_mix, norm_moe, a_wqkv, a_wo, a_lambda, a_subln,
              kv_norm, kv_ada_w, kv_ada_b, kv_w, b_wq, b_wo, router_w, router_b,
              w_gate_up, b_gate_up, w_down, b_down, final_norm):
    b, s, _ = x.shape
    c_act = jax.nn.silu(c)
    k_sh = None
    v_sh = None
    for l in range(DEPTH):
        mod = (c_act @ ada_w[l] + ada_b[l])[:, None, :]
        sh1, sc1, g1, sh2, sc2, g2 = jnp.split(mod, 6, axis=-1)
        h = adaln(x, norm_mix[l], sh1, sc1)
        if l < N_A_LAYERS:
            y = diff_attention(h, a_wqkv[l], a_wo[l], a_lambda[l], a_subln[l], l)
        else:
            if l == N_A_LAYERS:
                kv_mod = (c_act @ kv_ada_w + kv_ada_b)[:, None, :]
                kv_sh, kv_sc = jnp.split(kv_mod, 2, axis=-1)
                kv = adaln(x, kv_norm, kv_sh, kv_sc) @ kv_w
                k_sh, v_sh = jnp.split(kv, 2, axis=-1)
                k_sh = k_sh.reshape(b, s, B_HEADS, B_HEAD_DIM)
                v_sh = v_sh.reshape(b, s, B_HEADS, B_HEAD_DIM)
            j = l - N_A_LAYERS
            y = stick_breaking_attention(h, b_wq[j], b_wo[j], k_sh, v_sh)
        x = x + g1 * y
        h = adaln(x, norm_moe[l], sh2, sc2)
        x = x + g2 * moe(h, router_w[l], router_b[l], w_gate_up[l], b_gate_up[l], w_down[l], b_down[l])
    return rmsnorm(x, final_norm)
```

```python
import functools
import math

import jax
import jax.numpy as jnp
import numpy as np
from jax import lax
from jax.experimental import pallas as pl
from jax.experimental.pallas import tpu as pltpu

F32 = jnp.float32
BF16 = jnp.bfloat16
EPS = 1e-6
MASK_VALUE = -1e30
TOP_K = 4
SWIGLU_ALPHA = 1.702
SWIGLU_LIMIT = 7.0

LANES = 128
HEAD_DIM = 64
V7X_VMEM_BYTES = 64 * 2 ** 20
VMEM_LIMIT = V7X_VMEM_BYTES * 7 // 8

TOKEN_TILE = 512
ATTN_TILE = 512
EXPERT_TILE = 512


def _params(*semantics):
    return pltpu.CompilerParams(dimension_semantics=semantics, vmem_limit_bytes=VMEM_LIMIT)


def _largest_tile(n, cap):
    t = min(n, cap) // LANES * LANES
    while n % t:
        t -= LANES
    return t


def _adaln(x, gain, shift, scale):
    ms = jnp.mean(x * x, axis=-1, keepdims=True)
    return x * lax.rsqrt(ms + EPS) * gain * (1.0 + scale) + shift


def _mod_kernel(c_ref, w_ref, b_ref, o_ref):
    c = c_ref[...]
    a = c / (1.0 + jnp.exp(-c))
    o_ref[...] = jnp.dot(a, w_ref[...], precision=lax.Precision.HIGHEST,
                         preferred_element_type=F32) + b_ref[...]


def _modulation(c_pad, w, b):
    n_l, d, n = w.shape
    tn = _largest_tile(n, 2048)
    return pl.pallas_call(
        _mod_kernel,
        out_shape=jax.ShapeDtypeStruct((n_l, 8, n), F32),
        grid=(n_l, n // tn),
        in_specs=[
            pl.BlockSpec((8, d), lambda l, j: (0, 0)),
            pl.BlockSpec((None, d, tn), lambda l, j: (l, 0, j)),
            pl.BlockSpec((None, 1, tn), lambda l, j: (l, 0, j)),
        ],
        out_specs=pl.BlockSpec((None, 8, tn), lambda l, j: (l, 0, j)),
        compiler_params=_params("parallel", "parallel"),
        name="modulation",
    )(c_pad, w, b.reshape(n_l, 1, n))


def _norm_linear_kernel(x_ref, g_ref, sh_ref, sc_ref, w_ref, o_ref):
    h = _adaln(x_ref[...], g_ref[...], sh_ref[...], sc_ref[...])
    o_ref[...] = jnp.dot(h.astype(BF16), w_ref[...], preferred_element_type=F32).astype(o_ref.dtype)


def _norm_linear(x, gain, shift, scale, w, seq):
    t, d = x.shape
    n = w.shape[1]
    tm = min(TOKEN_TILE, seq)
    per_seq = seq // tm
    b = shift.shape[0]
    return pl.pallas_call(
        _norm_linear_kernel,
        out_shape=jax.ShapeDtypeStruct((t, n), BF16),
        grid=(t // tm,),
        in_specs=[
            pl.BlockSpec((tm, d), lambda i: (i, 0)),
            pl.BlockSpec((1, d), lambda i: (0, 0)),
            pl.BlockSpec((None, 1, d), lambda i: (i // per_seq, 0, 0)),
            pl.BlockSpec((None, 1, d), lambda i: (i // per_seq, 0, 0)),
            pl.BlockSpec((d, n), lambda i: (0, 0)),
        ],
        out_specs=pl.BlockSpec((tm, n), lambda i: (i, 0)),
        compiler_params=_params("parallel"),
        name="norm_linear",
    )(x, gain.reshape(1, d), shift.reshape(b, 1, d), scale.reshape(b, 1, d), w)


def _proj_residual_kernel(a_ref, w_ref, x_ref, g_ref, o_ref):
    y = jnp.dot(a_ref[...], w_ref[...], preferred_element_type=F32)
    o_ref[...] = x_ref[...] + g_ref[...] * y


def _proj_residual(a, w, x, gate, seq):
    t, d = x.shape
    k = a.shape[1]
    tm = min(TOKEN_TILE, seq)
    per_seq = seq // tm
    b = gate.shape[0]
    return pl.pallas_call(
        _proj_residual_kernel,
        out_shape=jax.ShapeDtypeStruct((t, d), F32),
        grid=(t // tm,),
        in_specs=[
            pl.BlockSpec((tm, k), lambda i: (i, 0)),
            pl.BlockSpec((k, d), lambda i: (0, 0)),
            pl.BlockSpec((tm, d), lambda i: (i, 0)),
            pl.BlockSpec((None, 1, d), lambda i: (i // per_seq, 0, 0)),
        ],
        out_specs=pl.BlockSpec((tm, d), lambda i: (i, 0)),
        compiler_params=_params("parallel"),
        name="proj_residual",
    )(a, w, x, gate.reshape(b, 1, d))


def _split_halves(q):
    lane = lax.broadcasted_iota(jnp.int32, q.shape, 1)
    zero = jnp.zeros_like(q)
    return jnp.concatenate([jnp.where(lane < HEAD_DIM, q, zero),
                            jnp.where(lane >= HEAD_DIM, q, zero)], axis=0)


def _dot_nt(a, b):
    return lax.dot_general(a, b, (((1,), (1,)), ((), ())), preferred_element_type=F32)


def _tile_row_col(tq, tk):
    row = lax.broadcasted_iota(jnp.int32, (2 * tq, tk), 0)
    row = jnp.where(row >= tq, row - tq, row)
    col = lax.broadcasted_iota(jnp.int32, (2 * tq, tk), 1)
    return row, col


def _diff_attn_kernel(q_ref, k_ref, v_ref, slope_ref, lam_ref, subln_ref, o_ref,
                      m_ref, l_ref, acc_ref, *, tile, lam_init):
    i = pl.program_id(2)
    q = (q_ref[...].astype(F32) * (HEAD_DIM ** -0.5)).astype(BF16)
    qq = _split_halves(q)
    slope = slope_ref[...]
    col_i = lax.broadcasted_iota(jnp.int32, (1, tile), 1)
    m_ref[...] = jnp.full(m_ref.shape, MASK_VALUE, F32)
    l_ref[...] = jnp.zeros(l_ref.shape, F32)
    acc_ref[...] = jnp.zeros(acc_ref.shape, F32)

    def step(kb, diagonal):
        start = pl.multiple_of(kb * tile, tile)
        k = k_ref[pl.ds(start, tile), :]
        v = v_ref[pl.ds(start, tile), :]
        s = _dot_nt(qq, k)
        rel = (col_i + (kb - i) * tile).astype(F32)
        s = s + slope * rel
        if diagonal:
            row, col = _tile_row_col(tile, tile)
            s = jnp.where(col <= row, s, MASK_VALUE)
        m_prev = m_ref[...]
        m_new = jnp.maximum(m_prev, jnp.max(s, axis=-1, keepdims=True))
        alpha = jnp.exp(m_prev - m_new)
        p = jnp.exp(s - m_new)
        l_ref[...] = alpha * l_ref[...] + jnp.sum(p, axis=-1, keepdims=True)
        acc_ref[...] = alpha * acc_ref[...] + jnp.dot(p.astype(BF16), v, preferred_element_type=F32)
        m_ref[...] = m_new

    def body(kb, carry):
        step(kb, False)
        return carry

    lax.fori_loop(0, i, body, 0)
    step(i, True)

    o_all = acc_ref[...] / l_ref[...]
    lv = lam_ref[...]
    lam = (jnp.exp(jnp.sum(lv[0:1] * lv[1:2], axis=1, keepdims=True))
           - jnp.exp(jnp.sum(lv[2:3] * lv[3:4], axis=1, keepdims=True)) + lam_init)
    o = o_all[:tile] - lam * o_all[tile:]
    ms = jnp.mean(o * o, axis=-1, keepdims=True)
    o = o * lax.rsqrt(ms + EPS) * subln_ref[...] * (1.0 - lam_init)
    o_ref[...] = o.astype(o_ref.dtype)


def _diff_attention(qkv, lam_vecs, subln, batch, seq, layer_idx):
    t = qkv.shape[0]
    d = qkv.shape[1] // 3
    heads = d // LANES
    tile = min(ATTN_TILE, seq)
    nq = seq // tile
    lam_init = 0.8 - 0.6 * math.exp(-0.3 * layer_idx)
    slopes = np.array([2.0 ** (-8.0 * (h + 1) / heads) for h in range(heads)], np.float32)
    slopes = jnp.asarray(np.broadcast_to(slopes[:, None, None], (heads, 1, tile)))
    kernel = functools.partial(_diff_attn_kernel, tile=tile, lam_init=lam_init)
    return pl.pallas_call(
        kernel,
        out_shape=jax.ShapeDtypeStruct((t, d), BF16),
        grid=(batch, heads, nq),
        in_specs=[
            pl.BlockSpec((tile, LANES), lambda b, h, i: (b * nq + i, h)),
            pl.BlockSpec((seq, LANES), lambda b, h, i: (b, heads + h)),
            pl.BlockSpec((seq, LANES), lambda b, h, i: (b, 2 * heads + h)),
            pl.BlockSpec((None, 1, tile), lambda b, h, i: (h, 0, 0)),
            pl.BlockSpec(lam_vecs.shape, lambda b, h, i: (0, 0)),
            pl.BlockSpec((1, LANES), lambda b, h, i: (0, 0)),
        ],
        out_specs=pl.BlockSpec((tile, LANES), lambda b, h, i: (b * nq + i, h)),
        scratch_shapes=[
            pltpu.VMEM((2 * tile, 1), F32),
            pltpu.VMEM((2 * tile, 1), F32),
            pltpu.VMEM((2 * tile, LANES), F32),
        ],
        compiler_params=_params("parallel", "parallel", "arbitrary"),
        name="diff_attention",
    )(qkv, qkv, qkv, slopes, lam_vecs, subln.reshape(1, LANES))


def _suffix_sum_matrix():
    j = np.arange(LANES)
    m = (j[:, None] > j[None, :]).astype(np.float32)
    m = np.concatenate([m, np.ones((LANES, LANES), np.float32)], axis=1)
    return jnp.asarray(np.concatenate([m, m], axis=0), dtype=BF16)


def _stick_attn_kernel(q_ref, k_ref, v_ref, mt_ref, o_ref, rem_ref, acc_ref, *, tile):
    i = pl.program_id(2)
    q = (q_ref[...].astype(F32) * (HEAD_DIM ** -0.5)).astype(BF16)
    qq = _split_halves(q)
    mt = mt_ref[...]
    n_chunks = tile // LANES
    rem_ref[...] = jnp.zeros(rem_ref.shape, F32)
    acc_ref[...] = jnp.zeros(acc_ref.shape, F32)

    def step(kb, diagonal):
        start = pl.multiple_of(kb * tile, tile)
        k = k_ref[pl.ds(start, tile), :]
        v = v_ref[pl.ds(start, tile), :]
        z = _dot_nt(qq, k)
        softplus = jnp.maximum(z, 0.0) + jnp.log(1.0 + jnp.exp(-jnp.abs(z)))
        log_beta = z - softplus
        log_1mb = -softplus
        if diagonal:
            row, col = _tile_row_col(tile, tile)
            strict = col < row
            log_1mb = jnp.where(strict, log_1mb, 0.0)
        rem = rem_ref[...]
        chunks = [None] * n_chunks
        for c in reversed(range(n_chunks)):
            sl = slice(c * LANES, (c + 1) * LANES)
            x = log_1mb[:, sl]
            hi = x.astype(BF16)
            lo = (x - hi.astype(F32)).astype(BF16)
            sums = jnp.dot(jnp.concatenate([hi, lo], axis=1), mt, preferred_element_type=F32)
            a = jnp.exp(log_beta[:, sl] + sums[:, :LANES] + rem)
            rem = rem + sums[:, LANES:]
            if diagonal:
                a = jnp.where(strict[:, sl], a, 0.0)
            chunks[c] = a.astype(BF16)
        rem_ref[...] = rem
        acc_ref[...] += jnp.dot(jnp.concatenate(chunks, axis=1), v, preferred_element_type=F32)

    step(i, True)

    def body(n, carry):
        step(i - 1 - n, False)
        return carry

    lax.fori_loop(0, i, body, 0)

    acc = acc_ref[...]
    lane = lax.broadcasted_iota(jnp.int32, (tile, LANES), 1)
    o_ref[...] = jnp.where(lane < HEAD_DIM, acc[:tile], acc[tile:]).astype(o_ref.dtype)


def _stick_attention(q, kv, batch, seq):
    t, d = q.shape
    pairs = d // LANES
    tile = min(ATTN_TILE, seq)
    nq = seq // tile
    kernel = functools.partial(_stick_attn_kernel, tile=tile)
    return pl.pallas_call(
        kernel,
        out_shape=jax.ShapeDtypeStruct((t, d), BF16),
        grid=(batch, pairs, nq),
        in_specs=[
            pl.BlockSpec((tile, LANES), lambda b, j, i: (b * nq + i, j)),
            pl.BlockSpec((seq, LANES), lambda b, j, i: (b, j)),
            pl.BlockSpec((seq, LANES), lambda b, j, i: (b, pairs + j)),
            pl.BlockSpec((2 * LANES, 2 * LANES), lambda b, j, i: (0, 0)),
        ],
        out_specs=pl.BlockSpec((tile, LANES), lambda b, j, i: (b * nq + i, j)),
        scratch_shapes=[
            pltpu.VMEM((2 * tile, LANES), F32),
            pltpu.VMEM((2 * tile, LANES), F32),
        ],
        compiler_params=_params("parallel", "parallel", "arbitrary"),
        name="stick_attention",
    )(q, kv, kv, _suffix_sum_matrix())


def _moe_route_kernel(x_ref, g_ref, sh_ref, sc_ref, rw_ref, rb_ref, h_ref, idx_ref, gate_ref):
    h = _adaln(x_ref[...], g_ref[...], sh_ref[...], sc_ref[...])
    h_ref[...] = h.astype(h_ref.dtype)
    logits = jnp.dot(h, rw_ref[...], precision=lax.Precision.HIGHEST,
                     preferred_element_type=F32) + rb_ref[...]
    lane = lax.broadcasted_iota(jnp.int32, logits.shape, 1)
    lane_f = lane.astype(F32)
    vals, ids = [], []
    for _ in range(TOP_K):
        m = jnp.max(logits, axis=-1, keepdims=True)
        first = jnp.min(jnp.where(logits == m, lane_f, float(LANES)), axis=-1, keepdims=True)
        vals.append(m)
        ids.append(first)
        logits = jnp.where(lane_f == first, -jnp.inf, logits)
    exps = [jnp.exp(v - vals[0]) for v in vals]
    denom = exps[0]
    for e in exps[1:]:
        denom = denom + e
    gate = jnp.zeros(logits.shape, F32)
    idx = jnp.zeros(logits.shape, F32)
    for k in range(TOP_K):
        gate = jnp.where(lane == k, exps[k] / denom, gate)
        idx = jnp.where(lane == k, ids[k], idx)
    gate_ref[...] = gate
    idx_ref[...] = idx.astype(jnp.int32)


def _moe_route(x, gain, shift, scale, rw_pad, rb_pad, seq):
    t, d = x.shape
    tm = min(TOKEN_TILE, seq)
    per_seq = seq // tm
    b = shift.shape[0]
    return pl.pallas_call(
        _moe_route_kernel,
        out_shape=(jax.ShapeDtypeStruct((t, d), BF16),
                   jax.ShapeDtypeStruct((t, LANES), jnp.int32),
                   jax.ShapeDtypeStruct((t, LANES), F32)),
        grid=(t // tm,),
        in_specs=[
            pl.BlockSpec((tm, d), lambda i: (i, 0)),
            pl.BlockSpec((1, d), lambda i: (0, 0)),
            pl.BlockSpec((None, 1, d), lambda i: (i // per_seq, 0, 0)),
            pl.BlockSpec((None, 1, d), lambda i: (i // per_seq, 0, 0)),
            pl.BlockSpec((d, LANES), lambda i: (0, 0)),
            pl.BlockSpec((1, LANES), lambda i: (0, 0)),
        ],
        out_specs=(pl.BlockSpec((tm, d), lambda i: (i, 0)),
                   pl.BlockSpec((tm, LANES), lambda i: (i, 0)),
                   pl.BlockSpec((tm, LANES), lambda i: (i, 0))),
        compiler_params=_params("parallel"),
        name="moe_route",
    )(x, gain.reshape(1, d), shift.reshape(b, 1, d), scale.reshape(b, 1, d), rw_pad, rb_pad)


def _expert_kernel(block_e_ref, n_used_ref, h_ref, wgu_ref, bgu_ref, wd_ref, bd_ref, gate_ref, o_ref):
    i = pl.program_id(0)

    @pl.when(i < n_used_ref[0])
    def _():
        d_e = wd_ref.shape[0]
        gu = jnp.dot(h_ref[...], wgu_ref[...], preferred_element_type=F32) + bgu_ref[...]
        g = jnp.minimum(gu[:, :d_e], SWIGLU_LIMIT)
        u = jnp.clip(gu[:, d_e:], -SWIGLU_LIMIT, SWIGLU_LIMIT)
        act = g / (1.0 + jnp.exp(-SWIGLU_ALPHA * g)) * (u + 1.0)
        y = jnp.dot(act.astype(BF16), wd_ref[...], preferred_element_type=F32) + bd_ref[...]
        o_ref[...] = (y * gate_ref[...]).astype(o_ref.dtype)

    @pl.when(i >= n_used_ref[0])
    def _():
        o_ref[...] = jnp.zeros(o_ref.shape, o_ref.dtype)


def _expert_mlp(block_e, n_used, rows, wgu, bgu, wd, bd, row_gate, tile):
    n_rows, d = rows.shape
    n_e, _, two_de = wgu.shape
    d_e = two_de // 2
    n_blocks = n_rows // tile
    grid_spec = pltpu.PrefetchScalarGridSpec(
        num_scalar_prefetch=2,
        grid=(n_blocks,),
        in_specs=[
            pl.BlockSpec((tile, d), lambda i, be, nu: (i, 0)),
            pl.BlockSpec((None, d, two_de), lambda i, be, nu: (be[i], 0, 0)),
            pl.BlockSpec((None, 1, two_de), lambda i, be, nu: (be[i], 0, 0)),
            pl.BlockSpec((None, d_e, d), lambda i, be, nu: (be[i], 0, 0)),
            pl.BlockSpec((None, 1, d), lambda i, be, nu: (be[i], 0, 0)),
            pl.BlockSpec((tile, 1), lambda i, be, nu: (i, 0)),
        ],
        out_specs=pl.BlockSpec((tile, d), lambda i, be, nu: (i, 0)),
    )
    return pl.pallas_call(
        _expert_kernel,
        out_shape=jax.ShapeDtypeStruct((n_rows, d), BF16),
        grid_spec=grid_spec,
        compiler_params=_params("arbitrary"),
        name="expert_mlp",
    )(block_e, n_used, rows, wgu, bgu.reshape(n_e, 1, two_de), wd, bd.reshape(n_e, 1, d),
      row_gate.reshape(n_rows, 1))


def _combine_kernel(y_ref, x_ref, g_ref, o_ref):
    y = y_ref[0].astype(F32)
    for k in range(1, y_ref.shape[0]):
        y = y + y_ref[k].astype(F32)
    o_ref[...] = x_ref[...] + g_ref[...] * y


def _combine(y_k, x, gate, seq):
    t, d = x.shape
    tm = min(TOKEN_TILE, seq)
    per_seq = seq // tm
    b = gate.shape[0]
    return pl.pallas_call(
        _combine_kernel,
        out_shape=jax.ShapeDtypeStruct((t, d), F32),
        grid=(t // tm,),
        in_specs=[
            pl.BlockSpec((y_k.shape[0], tm, d), lambda i: (0, i, 0)),
            pl.BlockSpec((tm, d), lambda i: (i, 0)),
            pl.BlockSpec((None, 1, d), lambda i: (i // per_seq, 0, 0)),
        ],
        out_specs=pl.BlockSpec((tm, d), lambda i: (i, 0)),
        compiler_params=_params("parallel"),
        name="moe_combine",
    )(y_k, x, gate.reshape(b, 1, d))


def _routing_tables(idx, gates, n_experts, tile):
    t = idx.shape[0]
    n_assign = t * TOP_K
    e_flat = idx.reshape(n_assign)
    g_flat = gates.reshape(n_assign)
    order = jnp.argsort(e_flat)
    sorted_e = e_flat[order]
    counts = jnp.bincount(e_flat, length=n_experts)
    padded = (counts + tile - 1) // tile * tile
    start = jnp.cumsum(counts) - counts
    pend = jnp.cumsum(padded)
    pstart = pend - padded
    dest = (pstart[sorted_e] + (jnp.arange(n_assign) - start[sorted_e])).astype(jnp.int32)
    n_blocks = (n_assign + n_experts * (tile - 1) + tile - 1) // tile
    n_rows = n_blocks * tile
    row_tok = jnp.zeros((n_rows,), jnp.int32).at[dest].set((order // TOP_K).astype(jnp.int32))
    row_gate = jnp.zeros((n_rows,), F32).at[dest].set(g_flat[order])
    block_e = jnp.minimum(jnp.searchsorted(pend, jnp.arange(n_blocks) * tile, side='right'),
                          n_experts - 1).astype(jnp.int32)
    n_used = (pend[-1] // tile).astype(jnp.int32).reshape(1)
    slot_row = jnp.zeros((n_assign,), jnp.int32).at[order].set(dest)
    return row_tok, row_gate, block_e, n_used, slot_row


def _moe(x, gain, shift, scale, gate, rw, rb, wgu, bgu, wd, bd, seq):
    t, d = x.shape
    n_e = rw.shape[1]
    rw_pad = jnp.pad(rw, ((0, 0), (0, LANES - n_e)))
    rb_pad = jnp.pad(rb, (0, LANES - n_e), constant_values=MASK_VALUE).reshape(1, LANES)
    h, idx, gates = _moe_route(x, gain, shift, scale, rw_pad, rb_pad, seq)
    tile = min(EXPERT_TILE, t * TOP_K // n_e)
    row_tok, row_gate, block_e, n_used, slot_row = _routing_tables(
        idx[:, :TOP_K], gates[:, :TOP_K], n_e, tile)
    rows = jnp.take(h, row_tok, axis=0)
    ys = _expert_mlp(block_e, n_used, rows, wgu, bgu, wd, bd, row_gate, tile)
    slot_major = slot_row.reshape(t, TOP_K).T.reshape(-1)
    y_k = jnp.take(ys, slot_major, axis=0).reshape(TOP_K, t, d)
    return _combine(y_k, x, gate, seq)


def _rmsnorm_kernel(x_ref, g_ref, o_ref):
    x = x_ref[...]
    ms = jnp.mean(x * x, axis=-1, keepdims=True)
    o_ref[...] = x * lax.rsqrt(ms + EPS) * g_ref[...]


def _rmsnorm(x, gain):
    t, d = x.shape
    tm = min(TOKEN_TILE, t)
    return pl.pallas_call(
        _rmsnorm_kernel,
        out_shape=jax.ShapeDtypeStruct((t, d), F32),
        grid=(t // tm,),
        in_specs=[pl.BlockSpec((tm, d), lambda i: (i, 0)),
                  pl.BlockSpec((1, d), lambda i: (0, 0))],
        out_specs=pl.BlockSpec((tm, d), lambda i: (i, 0)),
        compiler_params=_params("parallel"),
        name="final_rmsnorm",
    )(x, gain.reshape(1, d))


def kernel(x, c, ada_w, ada_b, norm_mix, norm_moe, a_wqkv, a_wo, a_lambda, a_subln, kv_norm,
           kv_ada_w, kv_ada_b, kv_w, b_wq, b_wo, router_w, router_b, w_gate_up, b_gate_up,
           w_down, b_down, final_norm):
    b, s, d = x.shape
    depth = ada_w.shape[0]
    n_a = a_wqkv.shape[0]
    xt = x.reshape(b * s, d)

    c_pad = jnp.pad(c, ((0, 8 - b), (0, 0)))
    mod = _modulation(c_pad, ada_w, ada_b)[:, :b]
    kv_mod = _modulation(c_pad, kv_ada_w[None], kv_ada_b[None])[0, :b]

    for l in range(depth):
        sh1, sc1, g1, sh2, sc2, g2 = jnp.split(mod[l], 6, axis=-1)
        if l < n_a:
            qkv = _norm_linear(xt, norm_mix[l], sh1, sc1, a_wqkv[l].astype(BF16), s)
            o = _diff_attention(qkv, a_lambda[l], a_subln[l], b, s, l)
            xt = _proj_residual(o, a_wo[l].astype(BF16), xt, g1, s)
        else:
            if l == n_a:
                kv_sh, kv_sc = jnp.split(kv_mod, 2, axis=-1)
                kv = _norm_linear(xt, kv_norm, kv_sh, kv_sc, kv_w.astype(BF16), s)
            j = l - n_a
            q = _norm_linear(xt, norm_mix[l], sh1, sc1, b_wq[j].astype(BF16), s)
            o = _stick_attention(q, kv, b, s)
            xt = _proj_residual(o, b_wo[j].astype(BF16), xt, g1, s)
        xt = _moe(xt, norm_moe[l], sh2, sc2, g2, router_w[l], router_b[l],
                  w_gate_up[l].astype(BF16), b_gate_up[l], w_down[l].astype(BF16), b_down[l], s)
    return _rmsnorm(xt, final_norm).reshape(b, s, d)
```

```python
import functools
import math

import jax
import jax.numpy as jnp
import numpy as np
from jax import lax
from jax.experimental import pallas as pl
from jax.experimental.pallas import tpu as pltpu

F32 = jnp.float32
BF16 = jnp.bfloat16
EPS = 1e-6
MASK_VALUE = -1e30
TOP_K = 4
SWIGLU_ALPHA = 1.702
SWIGLU_LIMIT = 7.0
LOG2E = math.log2(math.e)

LANES = 128
SUBLANES = 8
HEAD_DIM = 64
V7X_VMEM_BYTES = 64 * 2 ** 20
VMEM_LIMIT = V7X_VMEM_BYTES * 7 // 8

TOKEN_TILE = 512
ATTN_TILE = 512
EXPERT_TILE = 512


def _params(*semantics):
    return pltpu.CompilerParams(dimension_semantics=semantics, vmem_limit_bytes=VMEM_LIMIT)


def _largest_tile(n, cap):
    t = min(n, cap) // LANES * LANES
    while n % t:
        t -= LANES
    return t


def _adaln(x, gain, shift, scale):
    ms = jnp.mean(x * x, axis=-1, keepdims=True)
    return x * lax.rsqrt(ms + EPS) * gain * (1.0 + scale) + shift


def _mod_kernel(c_ref, w_ref, b_ref, o_ref):
    c = c_ref[...]
    a = c / (1.0 + jnp.exp(-c))
    o_ref[...] = jnp.dot(a, w_ref[...], precision=lax.Precision.HIGHEST,
                         preferred_element_type=F32) + b_ref[...]


def _modulation(c_pad, w, b):
    n_l, d, n = w.shape
    tn = _largest_tile(n, 2048)
    return pl.pallas_call(
        _mod_kernel,
        out_shape=jax.ShapeDtypeStruct((n_l, SUBLANES, n), F32),
        grid=(n_l, n // tn),
        in_specs=[
            pl.BlockSpec((SUBLANES, d), lambda l, j: (0, 0)),
            pl.BlockSpec((None, d, tn), lambda l, j: (l, 0, j)),
            pl.BlockSpec((None, 1, tn), lambda l, j: (l, 0, j)),
        ],
        out_specs=pl.BlockSpec((None, SUBLANES, tn), lambda l, j: (l, 0, j)),
        compiler_params=_params("parallel", "parallel"),
        name="modulation",
    )(c_pad, w, b.reshape(n_l, 1, n))


def _norm_linear_kernel(x_ref, g_ref, sh_ref, sc_ref, w_ref, o_ref):
    h = _adaln(x_ref[...], g_ref[...], sh_ref[...], sc_ref[...])
    o_ref[...] = jnp.dot(h.astype(BF16), w_ref[...], preferred_element_type=F32).astype(o_ref.dtype)


def _norm_linear(x, gain, shift, scale, w, seq):
    t, d = x.shape
    n = w.shape[1]
    tm = min(TOKEN_TILE, seq)
    per_seq = seq // tm
    b = shift.shape[0]
    return pl.pallas_call(
        _norm_linear_kernel,
        out_shape=jax.ShapeDtypeStruct((t, n), BF16),
        grid=(t // tm,),
        in_specs=[
            pl.BlockSpec((tm, d), lambda i: (i, 0)),
            pl.BlockSpec((1, d), lambda i: (0, 0)),
            pl.BlockSpec((None, 1, d), lambda i: (i // per_seq, 0, 0)),
            pl.BlockSpec((None, 1, d), lambda i: (i // per_seq, 0, 0)),
            pl.BlockSpec((d, n), lambda i: (0, 0)),
        ],
        out_specs=pl.BlockSpec((tm, n), lambda i: (i, 0)),
        compiler_params=_params("parallel"),
        name="norm_linear",
    )(x, gain.reshape(1, d), shift.reshape(b, 1, d), scale.reshape(b, 1, d), w)


def _proj_residual_kernel(a_ref, w_ref, x_ref, g_ref, o_ref):
    y = jnp.dot(a_ref[...], w_ref[...], preferred_element_type=F32)
    o_ref[...] = x_ref[...] + g_ref[...] * y


def _proj_residual(a, w, x, gate, seq):
    t, d = x.shape
    k = a.shape[1]
    tm = min(TOKEN_TILE, seq)
    per_seq = seq // tm
    b = gate.shape[0]
    return pl.pallas_call(
        _proj_residual_kernel,
        out_shape=jax.ShapeDtypeStruct((t, d), F32),
        grid=(t // tm,),
        in_specs=[
            pl.BlockSpec((tm, k), lambda i: (i, 0)),
            pl.BlockSpec((k, d), lambda i: (0, 0)),
            pl.BlockSpec((tm, d), lambda i: (i, 0)),
            pl.BlockSpec((None, 1, d), lambda i: (i // per_seq, 0, 0)),
        ],
        out_specs=pl.BlockSpec((tm, d), lambda i: (i, 0)),
        compiler_params=_params("parallel"),
        name="proj_residual",
    )(a, w, x, gate.reshape(b, 1, d))


def _scaled_halves(q_ref):
    q = (q_ref[...].astype(F32) * (HEAD_DIM ** -0.5 * LOG2E)).astype(BF16)
    lane = lax.broadcasted_iota(jnp.int32, q.shape, 1)
    zero = jnp.zeros_like(q)
    return jnp.concatenate([jnp.where(lane < HEAD_DIM, q, zero),
                            jnp.where(lane >= HEAD_DIM, q, zero)], axis=0)


def _dot_nt(a, b):
    return lax.dot_general(a, b, (((1,), (1,)), ((), ())), preferred_element_type=F32)


def _tile_row_col(tq, tk):
    row = lax.broadcasted_iota(jnp.int32, (2 * tq, tk), 0)
    row = jnp.where(row >= tq, row - tq, row)
    col = lax.broadcasted_iota(jnp.int32, (2 * tq, tk), 1)
    return row, col


def _lane_chunks(n):
    return [slice(c * LANES, (c + 1) * LANES) for c in range(n // LANES)]


def _diff_attn_kernel(q_ref, k_ref, v_ref, slope_ref, lam_ref, subln_ref, o_ref,
                      m_ref, l_ref, acc_ref, s_ref, *, tile, lam_init):
    i = pl.program_id(2)
    qq = _scaled_halves(q_ref)
    slope = slope_ref[...]
    col_i = lax.broadcasted_iota(jnp.int32, (1, tile), 1)
    ones = jnp.ones((tile, LANES), BF16)
    m_ref[...] = jnp.full(m_ref.shape, MASK_VALUE, F32)
    l_ref[...] = jnp.zeros(l_ref.shape, F32)
    acc_ref[...] = jnp.zeros(acc_ref.shape, F32)

    def scores(kb):
        k = k_ref[pl.ds(pl.multiple_of(kb * tile, tile), tile), :]
        return _dot_nt(qq, k) + slope * (col_i + (kb - i) * tile).astype(F32)

    def accumulate(s, kb, diagonal):
        v = v_ref[pl.ds(pl.multiple_of(kb * tile, tile), tile), :]
        if diagonal:
            row, col = _tile_row_col(tile, tile)
            s = jnp.where(col <= row, s, MASK_VALUE)
        m_prev = m_ref[...]
        m_new = jnp.maximum(m_prev, jnp.max(s, axis=-1, keepdims=True))
        alpha = jnp.exp2(m_prev - m_new)
        p = jnp.concatenate([jnp.exp2(s[:, sl] - m_new).astype(BF16) for sl in _lane_chunks(tile)],
                            axis=1)
        pv = jnp.dot(p, jnp.concatenate([v, ones], axis=1),
                     preferred_element_type=F32)
        acc_ref[...] = alpha * acc_ref[...] + pv[:, :LANES]
        l_ref[...] = alpha * l_ref[...] + pv[:, LANES:]
        m_ref[...] = m_new

    s_ref[...] = scores(0)

    def body(kb, carry):
        s = s_ref[...]
        s_ref[...] = scores(kb + 1)
        accumulate(s, kb, False)
        return carry

    lax.fori_loop(0, i, body, 0)
    accumulate(s_ref[...], i, True)

    o_all = acc_ref[...] / l_ref[...]
    lv = lam_ref[...]
    lam = (jnp.exp(jnp.sum(lv[0:1] * lv[1:2], axis=1, keepdims=True))
           - jnp.exp(jnp.sum(lv[2:3] * lv[3:4], axis=1, keepdims=True)) + lam_init)
    o = o_all[:tile] - lam * o_all[tile:]
    ms = jnp.mean(o * o, axis=-1, keepdims=True)
    o = o * lax.rsqrt(ms + EPS) * subln_ref[...] * (1.0 - lam_init)
    o_ref[...] = o.astype(o_ref.dtype)


def _diff_attention(qkv, lam_vecs, subln, batch, seq, layer_idx):
    t = qkv.shape[0]
    d = qkv.shape[1] // 3
    heads = d // LANES
    tile = min(ATTN_TILE, seq)
    nq = seq // tile
    lam_init = 0.8 - 0.6 * math.exp(-0.3 * layer_idx)
    slopes = np.array([2.0 ** (-8.0 * (h + 1) / heads) * LOG2E for h in range(heads)], np.float32)
    slopes = jnp.asarray(np.broadcast_to(slopes[:, None, None], (heads, 1, tile)))
    kernel = functools.partial(_diff_attn_kernel, tile=tile, lam_init=lam_init)
    return pl.pallas_call(
        kernel,
        out_shape=jax.ShapeDtypeStruct((t, d), BF16),
        grid=(batch, heads, nq),
        in_specs=[
            pl.BlockSpec((tile, LANES), lambda b, h, i: (b * nq + i, h)),
            pl.BlockSpec((seq, LANES), lambda b, h, i: (b, heads + h)),
            pl.BlockSpec((seq, LANES), lambda b, h, i: (b, 2 * heads + h)),
            pl.BlockSpec((None, 1, tile), lambda b, h, i: (h, 0, 0)),
            pl.BlockSpec(lam_vecs.shape, lambda b, h, i: (0, 0)),
            pl.BlockSpec((1, LANES), lambda b, h, i: (0, 0)),
        ],
        out_specs=pl.BlockSpec((tile, LANES), lambda b, h, i: (b * nq + i, h)),
        scratch_shapes=[
            pltpu.VMEM((2 * tile, LANES), F32),
            pltpu.VMEM((2 * tile, LANES), F32),
            pltpu.VMEM((2 * tile, LANES), F32),
            pltpu.VMEM((2 * tile, tile), F32),
        ],
        compiler_params=_params("parallel", "parallel", "arbitrary"),
        name="diff_attention",
    )(qkv, qkv, qkv, slopes, lam_vecs, subln.reshape(1, LANES))


def _suffix_sum_matrix():
    j = np.arange(LANES)
    m = (j[:, None] >= j[None, :]).astype(np.float32)
    m = np.concatenate([m, np.ones((LANES, LANES), np.float32)], axis=1)
    return jnp.asarray(np.concatenate([m, m], axis=0), dtype=BF16)


def _stick_attn_kernel(q_ref, k_ref, v_ref, mt_ref, o_ref, rem_ref, acc_ref, z_ref, *, tile):
    i = pl.program_id(2)
    qq = _scaled_halves(q_ref)
    mt = mt_ref[...]
    rem_ref[...] = jnp.zeros(rem_ref.shape, F32)
    acc_ref[...] = jnp.zeros(acc_ref.shape, F32)

    def scores(kb):
        return _dot_nt(qq, k_ref[pl.ds(pl.multiple_of(kb * tile, tile), tile), :])

    def accumulate(z, kb, diagonal):
        v = v_ref[pl.ds(pl.multiple_of(kb * tile, tile), tile), :]
        softplus = jnp.maximum(z, 0.0) + jnp.log2(1.0 + jnp.exp2(-jnp.abs(z)))
        if diagonal:
            row, col = _tile_row_col(tile, tile)
            strict = col < row
            softplus = jnp.where(strict, softplus, 0.0)
        rem = rem_ref[...]
        chunks = []
        for sl in reversed(_lane_chunks(tile)):
            x = softplus[:, sl]
            hi = x.astype(BF16)
            lo = (x - hi.astype(F32)).astype(BF16)
            sums = jnp.dot(jnp.concatenate([hi, lo], axis=1), mt, preferred_element_type=F32)
            a = jnp.exp2(z[:, sl] - sums[:, :LANES] - rem)
            rem = rem + sums[:, LANES:]
            if diagonal:
                a = jnp.where(strict[:, sl], a, 0.0)
            chunks.append(a.astype(BF16))
        rem_ref[...] = rem
        acc_ref[...] += jnp.dot(jnp.concatenate(chunks[::-1], axis=1), v, preferred_element_type=F32)

    z_ref[...] = scores(i)

    @pl.when(i > 0)
    def _():
        z = z_ref[...]
        z_ref[...] = scores(i - 1)
        accumulate(z, i, True)

        def body(n, carry):
            z = z_ref[...]
            z_ref[...] = scores(i - 1 - n)
            accumulate(z, i - n, False)
            return carry

        lax.fori_loop(1, i, body, 0)
        accumulate(z_ref[...], 0, False)

    @pl.when(i == 0)
    def _():
        accumulate(z_ref[...], 0, True)

    acc = acc_ref[...]
    lane = lax.broadcasted_iota(jnp.int32, (tile, LANES), 1)
    o_ref[...] = jnp.where(lane < HEAD_DIM, acc[:tile], acc[tile:]).astype(o_ref.dtype)


def _stick_attention(q, kv, batch, seq):
    t, d = q.shape
    pairs = d // LANES
    tile = min(ATTN_TILE, seq)
    nq = seq // tile
    kernel = functools.partial(_stick_attn_kernel, tile=tile)
    return pl.pallas_call(
        kernel,
        out_shape=jax.ShapeDtypeStruct((t, d), BF16),
        grid=(batch, pairs, nq),
        in_specs=[
            pl.BlockSpec((tile, LANES), lambda b, j, i: (b * nq + i, j)),
            pl.BlockSpec((seq, LANES), lambda b, j, i: (b, j)),
            pl.BlockSpec((seq, LANES), lambda b, j, i: (b, pairs + j)),
            pl.BlockSpec((2 * LANES, 2 * LANES), lambda b, j, i: (0, 0)),
        ],
        out_specs=pl.BlockSpec((tile, LANES), lambda b, j, i: (b * nq + i, j)),
        scratch_shapes=[
            pltpu.VMEM((2 * tile, LANES), F32),
            pltpu.VMEM((2 * tile, LANES), F32),
            pltpu.VMEM((2 * tile, tile), F32),
        ],
        compiler_params=_params("parallel", "parallel", "arbitrary"),
        name="stick_attention",
    )(q, kv, kv, _suffix_sum_matrix())


def _moe_route_kernel(x_ref, g_ref, sh_ref, sc_ref, rw_ref, rb_ref, h_ref, idx_ref, gate_ref):
    h = _adaln(x_ref[...], g_ref[...], sh_ref[...], sc_ref[...])
    h_ref[...] = h.astype(h_ref.dtype)
    logits = jnp.dot(h, rw_ref[...], precision=lax.Precision.HIGHEST,
                     preferred_element_type=F32) + rb_ref[...]
    lane = lax.broadcasted_iota(jnp.int32, logits.shape, 1)
    lane_f = lane.astype(F32)
    vals, ids = [], []
    for _ in range(TOP_K):
        m = jnp.max(logits, axis=-1, keepdims=True)
        first = jnp.min(jnp.where(logits == m, lane_f, float(LANES)), axis=-1, keepdims=True)
        vals.append(m)
        ids.append(first)
        logits = jnp.where(lane_f == first, -jnp.inf, logits)
    exps = [jnp.exp(v - vals[0]) for v in vals]
    denom = exps[0]
    for e in exps[1:]:
        denom = denom + e
    gate = jnp.zeros(logits.shape, F32)
    idx = jnp.zeros(logits.shape, F32)
    for k in range(TOP_K):
        gate = jnp.where(lane == k, exps[k] / denom, gate)
        idx = jnp.where(lane == k, ids[k], idx)
    gate_ref[...] = gate
    idx_ref[...] = idx.T[:SUBLANES].astype(jnp.int32)


def _moe_route(x, gain, shift, scale, rw_pad, rb_pad, seq):
    t, d = x.shape
    tm = min(TOKEN_TILE, seq)
    per_seq = seq // tm
    b = shift.shape[0]
    return pl.pallas_call(
        _moe_route_kernel,
        out_shape=(jax.ShapeDtypeStruct((t, d), BF16),
                   jax.ShapeDtypeStruct((SUBLANES, t), jnp.int32),
                   jax.ShapeDtypeStruct((t, LANES), F32)),
        grid=(t // tm,),
        in_specs=[
            pl.BlockSpec((tm, d), lambda i: (i, 0)),
            pl.BlockSpec((1, d), lambda i: (0, 0)),
            pl.BlockSpec((None, 1, d), lambda i: (i // per_seq, 0, 0)),
            pl.BlockSpec((None, 1, d), lambda i: (i // per_seq, 0, 0)),
            pl.BlockSpec((d, LANES), lambda i: (0, 0)),
            pl.BlockSpec((1, LANES), lambda i: (0, 0)),
        ],
        out_specs=(pl.BlockSpec((tm, d), lambda i: (i, 0)),
                   pl.BlockSpec((SUBLANES, tm), lambda i: (0, i)),
                   pl.BlockSpec((tm, LANES), lambda i: (i, 0))),
        compiler_params=_params("parallel"),
        name="moe_route",
    )(x, gain.reshape(1, d), shift.reshape(b, 1, d), scale.reshape(b, 1, d), rw_pad, rb_pad)


def _expert_kernel(block_e_ref, n_used_ref, h_ref, wgu_ref, bgu_ref, wd_ref, bd_ref, o_ref):
    i = pl.program_id(0)

    @pl.when(i < n_used_ref[0])
    def _():
        d_e = wd_ref.shape[0]
        gu = jnp.dot(h_ref[...], wgu_ref[...].astype(BF16), preferred_element_type=F32) + bgu_ref[...]
        g = jnp.minimum(gu[:, :d_e], SWIGLU_LIMIT)
        u = jnp.clip(gu[:, d_e:], -SWIGLU_LIMIT, SWIGLU_LIMIT)
        act = g / (1.0 + jnp.exp(-SWIGLU_ALPHA * g)) * (u + 1.0)
        y = jnp.dot(act.astype(BF16), wd_ref[...].astype(BF16), preferred_element_type=F32) + bd_ref[...]
        o_ref[...] = y.astype(o_ref.dtype)

    @pl.when(i >= n_used_ref[0])
    def _():
        o_ref[...] = jnp.zeros(o_ref.shape, o_ref.dtype)


def _expert_mlp(block_e, n_used, rows, wgu, bgu, wd, bd, tile):
    n_rows, d = rows.shape
    n_e, _, two_de = wgu.shape
    d_e = two_de // 2
    n_blocks = n_rows // tile
    grid_spec = pltpu.PrefetchScalarGridSpec(
        num_scalar_prefetch=2,
        grid=(n_blocks,),
        in_specs=[
            pl.BlockSpec((tile, d), lambda i, be, nu: (i, 0)),
            pl.BlockSpec((None, d, two_de), lambda i, be, nu: (be[i], 0, 0)),
            pl.BlockSpec((None, 1, two_de), lambda i, be, nu: (be[i], 0, 0)),
            pl.BlockSpec((None, d_e, d), lambda i, be, nu: (be[i], 0, 0)),
            pl.BlockSpec((None, 1, d), lambda i, be, nu: (be[i], 0, 0)),
        ],
        out_specs=pl.BlockSpec((tile, d), lambda i, be, nu: (i, 0)),
    )
    return pl.pallas_call(
        _expert_kernel,
        out_shape=jax.ShapeDtypeStruct((n_rows, d), BF16),
        grid_spec=grid_spec,
        compiler_params=_params("arbitrary"),
        name="expert_mlp",
    )(block_e, n_used, rows, wgu, bgu.reshape(n_e, 1, two_de), wd, bd.reshape(n_e, 1, d))


def _combine_kernel(y_ref, w_ref, x_ref, g_ref, o_ref):
    w = w_ref[...]
    y = w[:, 0:1] * y_ref[0].astype(F32)
    for k in range(1, y_ref.shape[0]):
        y = y + w[:, k:k + 1] * y_ref[k].astype(F32)
    o_ref[...] = x_ref[...] + g_ref[...] * y


def _combine(y_k, weights, x, gate, seq):
    t, d = x.shape
    tm = min(TOKEN_TILE, seq)
    per_seq = seq // tm
    b = gate.shape[0]
    return pl.pallas_call(
        _combine_kernel,
        out_shape=jax.ShapeDtypeStruct((t, d), F32),
        grid=(t // tm,),
        in_specs=[
            pl.BlockSpec((y_k.shape[0], tm, d), lambda i: (0, i, 0)),
            pl.BlockSpec((tm, LANES), lambda i: (i, 0)),
            pl.BlockSpec((tm, d), lambda i: (i, 0)),
            pl.BlockSpec((None, 1, d), lambda i: (i // per_seq, 0, 0)),
        ],
        out_specs=pl.BlockSpec((tm, d), lambda i: (i, 0)),
        compiler_params=_params("parallel"),
        name="moe_combine",
    )(y_k, weights, x, gate.reshape(b, 1, d))


def _routing_tables(idx_t, n_experts, tile):
    top_k, t = idx_t.shape
    n_assign = top_k * t
    e_flat = idx_t.reshape(n_assign)
    experts = jnp.arange(n_experts, dtype=jnp.int32)
    onehot = e_flat[:, None] == experts[None, :]
    counts = jnp.sum(onehot, axis=0, dtype=jnp.int32)
    padded = (counts + tile - 1) // tile * tile
    start = jnp.cumsum(counts) - counts
    pend = jnp.cumsum(padded)
    pstart = pend - padded
    order = jnp.argsort(e_flat, stable=True).astype(jnp.int32)
    sorted_pos = jnp.argsort(order).astype(jnp.int32)
    shift = jnp.sum(jnp.where(onehot, (pstart - start)[None, :], 0), axis=1, dtype=jnp.int32)
    slot_row = sorted_pos + shift
    n_blocks = (n_assign + n_experts * (tile - 1) + tile - 1) // tile
    first_row = jnp.arange(n_blocks, dtype=jnp.int32) * tile
    block_e = jnp.minimum(jnp.sum(pend[None, :] <= first_row[:, None], axis=1), n_experts - 1)
    block_e = block_e.astype(jnp.int32)
    n_used = (pend[-1] // tile).astype(jnp.int32).reshape(1)
    within = (first_row - pstart[block_e])[:, None] + jnp.arange(tile, dtype=jnp.int32)[None, :]
    valid = within < counts[block_e][:, None]
    src = jnp.where(valid, start[block_e][:, None] + within, 0).reshape(-1)
    row_tok = jnp.where(valid.reshape(-1), order[src] % t, 0)
    return row_tok, block_e, n_used, slot_row


def _moe(x, gain, shift, scale, gate, rw, rb, wgu, bgu, wd, bd, seq):
    t, d = x.shape
    n_e = rw.shape[1]
    rw_pad = jnp.pad(rw, ((0, 0), (0, LANES - n_e)))
    rb_pad = jnp.pad(rb, (0, LANES - n_e), constant_values=MASK_VALUE).reshape(1, LANES)
    h, idx_t, weights = _moe_route(x, gain, shift, scale, rw_pad, rb_pad, seq)
    tile = min(EXPERT_TILE, t * TOP_K // n_e)
    row_tok, block_e, n_used, slot_row = _routing_tables(idx_t[:TOP_K], n_e, tile)
    rows = jnp.take(h, row_tok, axis=0)
    ys = _expert_mlp(block_e, n_used, rows, wgu, bgu, wd, bd, tile)
    y_k = jnp.take(ys, slot_row, axis=0).reshape(TOP_K, t, d)
    return _combine(y_k, weights, x, gate, seq)


def _rmsnorm_kernel(x_ref, g_ref, o_ref):
    x = x_ref[...]
    ms = jnp.mean(x * x, axis=-1, keepdims=True)
    o_ref[...] = x * lax.rsqrt(ms + EPS) * g_ref[...]


def _rmsnorm(x, gain):
    t, d = x.shape
    tm = min(TOKEN_TILE, t)
    return pl.pallas_call(
        _rmsnorm_kernel,
        out_shape=jax.ShapeDtypeStruct((t, d), F32),
        grid=(t // tm,),
        in_specs=[pl.BlockSpec((tm, d), lambda i: (i, 0)),
                  pl.BlockSpec((1, d), lambda i: (0, 0))],
        out_specs=pl.BlockSpec((tm, d), lambda i: (i, 0)),
        compiler_params=_params("parallel"),
        name="final_rmsnorm",
    )(x, gain.reshape(1, d))


def kernel(x, c, ada_w, ada_b, norm_mix, norm_moe, a_wqkv, a_wo, a_lambda, a_subln, kv_norm,
           kv_ada_w, kv_ada_b, kv_w, b_wq, b_wo, router_w, router_b, w_gate_up, b_gate_up,
           w_down, b_down, final_norm):
    b, s, d = x.shape
    depth = ada_w.shape[0]
    n_a = a_wqkv.shape[0]
    xt = x.reshape(b * s, d)

    c_pad = jnp.pad(c, ((0, SUBLANES - b), (0, 0)))
    mod = _modulation(c_pad, ada_w, ada_b)[:, :b]
    kv_mod = _modulation(c_pad, kv_ada_w[None], kv_ada_b[None])[0, :b]

    for l in range(depth):
        sh1, sc1, g1, sh2, sc2, g2 = jnp.split(mod[l], 6, axis=-1)
        if l < n_a:
            qkv = _norm_linear(xt, norm_mix[l], sh1, sc1, a_wqkv[l].astype(BF16), s)
            o = _diff_attention(qkv, a_lambda[l], a_subln[l], b, s, l)
            xt = _proj_residual(o, a_wo[l].astype(BF16), xt, g1, s)
        else:
            if l == n_a:
                kv_sh, kv_sc = jnp.split(kv_mod, 2, axis=-1)
                kv = _norm_linear(xt, kv_norm, kv_sh, kv_sc, kv_w.astype(BF16), s)
            j = l - n_a
            q = _norm_linear(xt, norm_mix[l], sh1, sc1, b_wq[j].astype(BF16), s)
            o = _stick_attention(q, kv, b, s)
            xt = _proj_residual(o, b_wo[j].astype(BF16), xt, g1, s)
        xt = _moe(xt, norm_moe[l], sh2, sc2, g2, router_w[l], router_b[l],
                  w_gate_up[l], b_gate_up[l], w_down[l], b_down[l], s)
    return _rmsnorm(xt, final_norm).reshape(b, s, d)
```

```python
import functools
import math

import jax
import jax.numpy as jnp
import numpy as np
from jax import lax
from jax.experimental import pallas as pl
from jax.experimental.pallas import tpu as pltpu

F32 = jnp.float32
BF16 = jnp.bfloat16
EPS = 1e-6
MASK_VALUE = -1e30
TOP_K = 4
SWIGLU_ALPHA = 1.702
SWIGLU_LIMIT = 7.0
LOG2E = math.log2(math.e)

LANES = 128
SUBLANES = 8
HEAD_DIM = 64
V7X_VMEM_BYTES = 64 * 2 ** 20
VMEM_LIMIT = V7X_VMEM_BYTES * 7 // 8

TOKEN_TILE = 512
ATTN_TILE = 512
EXPERT_TILE = 512


def _params(*semantics):
    return pltpu.CompilerParams(dimension_semantics=semantics, vmem_limit_bytes=VMEM_LIMIT)


def _largest_tile(n, cap):
    t = min(n, cap) // LANES * LANES
    while n % t:
        t -= LANES
    return t


def _adaln(x, gain, shift, scale):
    ms = jnp.mean(x * x, axis=-1, keepdims=True)
    return x * lax.rsqrt(ms + EPS) * gain * (1.0 + scale) + shift


def _mod_kernel(c_ref, w_ref, b_ref, o_ref):
    c = c_ref[...]
    a = c / (1.0 + jnp.exp(-c))
    o_ref[...] = jnp.dot(a, w_ref[...], precision=lax.Precision.HIGHEST,
                         preferred_element_type=F32) + b_ref[...]


def _modulation(c_pad, w, b):
    n_l, d, n = w.shape
    tn = _largest_tile(n, 2048)
    return pl.pallas_call(
        _mod_kernel,
        out_shape=jax.ShapeDtypeStruct((n_l, SUBLANES, n), F32),
        grid=(n_l, n // tn),
        in_specs=[
            pl.BlockSpec((SUBLANES, d), lambda l, j: (0, 0)),
            pl.BlockSpec((None, d, tn), lambda l, j: (l, 0, j)),
            pl.BlockSpec((None, 1, tn), lambda l, j: (l, 0, j)),
        ],
        out_specs=pl.BlockSpec((None, SUBLANES, tn), lambda l, j: (l, 0, j)),
        compiler_params=_params("parallel", "parallel"),
        name="modulation",
    )(c_pad, w, b.reshape(n_l, 1, n))


def _norm_linear_kernel(x_ref, g_ref, sh_ref, sc_ref, w_ref, o_ref):
    h = _adaln(x_ref[...], g_ref[...], sh_ref[...], sc_ref[...])
    o_ref[...] = jnp.dot(h.astype(BF16), w_ref[...], preferred_element_type=F32).astype(o_ref.dtype)


def _norm_linear(x, gain, shift, scale, w, seq):
    t, d = x.shape
    n = w.shape[1]
    tm = min(TOKEN_TILE, seq)
    per_seq = seq // tm
    b = shift.shape[0]
    return pl.pallas_call(
        _norm_linear_kernel,
        out_shape=jax.ShapeDtypeStruct((t, n), BF16),
        grid=(t // tm,),
        in_specs=[
            pl.BlockSpec((tm, d), lambda i: (i, 0)),
            pl.BlockSpec((1, d), lambda i: (0, 0)),
            pl.BlockSpec((None, 1, d), lambda i: (i // per_seq, 0, 0)),
            pl.BlockSpec((None, 1, d), lambda i: (i // per_seq, 0, 0)),
            pl.BlockSpec((d, n), lambda i: (0, 0)),
        ],
        out_specs=pl.BlockSpec((tm, n), lambda i: (i, 0)),
        compiler_params=_params("parallel"),
        name="norm_linear",
    )(x, gain.reshape(1, d), shift.reshape(b, 1, d), scale.reshape(b, 1, d), w)


def _proj_residual_kernel(a_ref, w_ref, x_ref, g_ref, o_ref):
    y = jnp.dot(a_ref[...], w_ref[...], preferred_element_type=F32)
    o_ref[...] = x_ref[...] + g_ref[...] * y


def _proj_residual(a, w, x, gate, seq):
    t, d = x.shape
    k = a.shape[1]
    tm = min(TOKEN_TILE, seq)
    per_seq = seq // tm
    b = gate.shape[0]
    return pl.pallas_call(
        _proj_residual_kernel,
        out_shape=jax.ShapeDtypeStruct((t, d), F32),
        grid=(t // tm,),
        in_specs=[
            pl.BlockSpec((tm, k), lambda i: (i, 0)),
            pl.BlockSpec((k, d), lambda i: (0, 0)),
            pl.BlockSpec((tm, d), lambda i: (i, 0)),
            pl.BlockSpec((None, 1, d), lambda i: (i // per_seq, 0, 0)),
        ],
        out_specs=pl.BlockSpec((tm, d), lambda i: (i, 0)),
        compiler_params=_params("parallel"),
        name="proj_residual",
    )(a, w, x, gate.reshape(b, 1, d))


def _scaled_halves(q_ref):
    q = (q_ref[...].astype(F32) * (HEAD_DIM ** -0.5 * LOG2E)).astype(BF16)
    lane = lax.broadcasted_iota(jnp.int32, q.shape, 1)
    zero = jnp.zeros_like(q)
    return jnp.concatenate([jnp.where(lane < HEAD_DIM, q, zero),
                            jnp.where(lane >= HEAD_DIM, q, zero)], axis=0)


def _dot_nt(a, b):
    return lax.dot_general(a, b, (((1,), (1,)), ((), ())), preferred_element_type=F32)


def _tile_row_col(tq, tk):
    row = lax.broadcasted_iota(jnp.int32, (2 * tq, tk), 0)
    row = jnp.where(row >= tq, row - tq, row)
    col = lax.broadcasted_iota(jnp.int32, (2 * tq, tk), 1)
    return row, col


def _lane_chunks(n):
    return [slice(c * LANES, (c + 1) * LANES) for c in range(n // LANES)]


def _diff_attn_kernel(q_ref, k_ref, v_ref, slope_ref, lam_ref, subln_ref, o_ref,
                      m_ref, l_ref, acc_ref, s_ref, *, tile, lam_init):
    i = pl.program_id(2)
    qq = _scaled_halves(q_ref)
    slope = slope_ref[...]
    col_i = lax.broadcasted_iota(jnp.int32, (1, tile), 1)
    ones = jnp.ones((tile, LANES), BF16)
    m_ref[...] = jnp.full(m_ref.shape, MASK_VALUE, F32)
    l_ref[...] = jnp.zeros(l_ref.shape, F32)
    acc_ref[...] = jnp.zeros(acc_ref.shape, F32)

    def scores(kb):
        k = k_ref[pl.ds(pl.multiple_of(kb * tile, tile), tile), :]
        return _dot_nt(qq, k) + slope * (col_i + (kb - i) * tile).astype(F32)

    def accumulate(s, kb, diagonal):
        v = v_ref[pl.ds(pl.multiple_of(kb * tile, tile), tile), :]
        if diagonal:
            row, col = _tile_row_col(tile, tile)
            s = jnp.where(col <= row, s, MASK_VALUE)
        m_prev = m_ref[...]
        m_new = jnp.maximum(m_prev, jnp.max(s, axis=-1, keepdims=True))
        alpha = jnp.exp2(m_prev - m_new)
        p = jnp.concatenate([jnp.exp2(s[:, sl] - m_new).astype(BF16) for sl in _lane_chunks(tile)],
                            axis=1)
        pv = jnp.dot(p, jnp.concatenate([v, ones], axis=1),
                     preferred_element_type=F32)
        acc_ref[...] = alpha * acc_ref[...] + pv[:, :LANES]
        l_ref[...] = alpha * l_ref[...] + pv[:, LANES:]
        m_ref[...] = m_new

    s_ref[...] = scores(0)

    def body(kb, carry):
        s = s_ref[...]
        s_ref[...] = scores(kb + 1)
        accumulate(s, kb, False)
        return carry

    lax.fori_loop(0, i, body, 0)
    accumulate(s_ref[...], i, True)

    o_all = acc_ref[...] / l_ref[...]
    lv = lam_ref[...]
    lam = (jnp.exp(jnp.sum(lv[0:1] * lv[1:2], axis=1, keepdims=True))
           - jnp.exp(jnp.sum(lv[2:3] * lv[3:4], axis=1, keepdims=True)) + lam_init)
    o = o_all[:tile] - lam * o_all[tile:]
    ms = jnp.mean(o * o, axis=-1, keepdims=True)
    o = o * lax.rsqrt(ms + EPS) * subln_ref[...] * (1.0 - lam_init)
    o_ref[...] = o.astype(o_ref.dtype)


def _diff_attention(qkv, lam_vecs, subln, batch, seq, layer_idx):
    t = qkv.shape[0]
    d = qkv.shape[1] // 3
    heads = d // LANES
    tile = min(ATTN_TILE, seq)
    nq = seq // tile
    lam_init = 0.8 - 0.6 * math.exp(-0.3 * layer_idx)
    slopes = np.array([2.0 ** (-8.0 * (h + 1) / heads) * LOG2E for h in range(heads)], np.float32)
    slopes = jnp.asarray(np.broadcast_to(slopes[:, None, None], (heads, 1, tile)))
    kernel = functools.partial(_diff_attn_kernel, tile=tile, lam_init=lam_init)
    return pl.pallas_call(
        kernel,
        out_shape=jax.ShapeDtypeStruct((t, d), BF16),
        grid=(batch, heads, nq),
        in_specs=[
            pl.BlockSpec((tile, LANES), lambda b, h, i: (b * nq + i, h)),
            pl.BlockSpec((seq, LANES), lambda b, h, i: (b, heads + h)),
            pl.BlockSpec((seq, LANES), lambda b, h, i: (b, 2 * heads + h)),
            pl.BlockSpec((None, 1, tile), lambda b, h, i: (h, 0, 0)),
            pl.BlockSpec(lam_vecs.shape, lambda b, h, i: (0, 0)),
            pl.BlockSpec((1, LANES), lambda b, h, i: (0, 0)),
        ],
        out_specs=pl.BlockSpec((tile, LANES), lambda b, h, i: (b * nq + i, h)),
        scratch_shapes=[
            pltpu.VMEM((2 * tile, LANES), F32),
            pltpu.VMEM((2 * tile, LANES), F32),
            pltpu.VMEM((2 * tile, LANES), F32),
            pltpu.VMEM((2 * tile, tile), F32),
        ],
        compiler_params=_params("parallel", "parallel", "arbitrary"),
        name="diff_attention",
    )(qkv, qkv, qkv, slopes, lam_vecs, subln.reshape(1, LANES))


CHUNK = 256


def _suffix_sum_matrix():
    j = np.arange(CHUNK)
    return jnp.asarray((j[:, None] >= j[None, :]).astype(np.float32), dtype=BF16)


def _stick_attn_kernel(q_ref, k_ref, v_ref, mt_ref, o_ref, rem_ref, acc_ref, z_ref, *, tile):
    i = pl.program_id(2)
    qq = _scaled_halves(q_ref)
    mt = mt_ref[...]
    rem_ref[...] = jnp.zeros(rem_ref.shape, F32)
    acc_ref[...] = jnp.zeros(acc_ref.shape, F32)

    def scores(kb):
        return _dot_nt(qq, k_ref[pl.ds(pl.multiple_of(kb * tile, tile), tile), :])

    def accumulate(z, kb, diagonal):
        v = v_ref[pl.ds(pl.multiple_of(kb * tile, tile), tile), :]
        softplus = jnp.maximum(z, 0.0) + jnp.log2(1.0 + jnp.exp2(-jnp.abs(z)))
        if diagonal:
            row, col = _tile_row_col(tile, tile)
            strict = col < row
            softplus = jnp.where(strict, softplus, 0.0)
        rem = rem_ref[...]
        chunks = []
        for c in reversed(range(tile // CHUNK)):
            x = softplus[:, c * CHUNK:(c + 1) * CHUNK].astype(BF16)
            sums = jnp.dot(x, mt, preferred_element_type=F32)
            for h in reversed(range(CHUNK // LANES)):
                sl = slice(c * CHUNK + h * LANES, c * CHUNK + (h + 1) * LANES)
                a = jnp.exp2(z[:, sl] - sums[:, h * LANES:(h + 1) * LANES] - rem)
                if diagonal:
                    a = jnp.where(strict[:, sl], a, 0.0)
                chunks.append(a.astype(BF16))
            rem = rem + jnp.broadcast_to(sums[:, 0:1], rem.shape)
        rem_ref[...] = rem
        acc_ref[...] += jnp.dot(jnp.concatenate(chunks[::-1], axis=1), v, preferred_element_type=F32)

    z_ref[...] = scores(i)

    @pl.when(i > 0)
    def _():
        z = z_ref[...]
        z_ref[...] = scores(i - 1)
        accumulate(z, i, True)

        def body(n, carry):
            z = z_ref[...]
            z_ref[...] = scores(i - 1 - n)
            accumulate(z, i - n, False)
            return carry

        lax.fori_loop(1, i, body, 0)
        accumulate(z_ref[...], 0, False)

    @pl.when(i == 0)
    def _():
        accumulate(z_ref[...], 0, True)

    acc = acc_ref[...]
    lane = lax.broadcasted_iota(jnp.int32, (tile, LANES), 1)
    o_ref[...] = jnp.where(lane < HEAD_DIM, acc[:tile], acc[tile:]).astype(o_ref.dtype)


def _stick_attention(q, kv, batch, seq):
    t, d = q.shape
    pairs = d // LANES
    tile = min(ATTN_TILE, seq)
    nq = seq // tile
    kernel = functools.partial(_stick_attn_kernel, tile=tile)
    return pl.pallas_call(
        kernel,
        out_shape=jax.ShapeDtypeStruct((t, d), BF16),
        grid=(batch, pairs, nq),
        in_specs=[
            pl.BlockSpec((tile, LANES), lambda b, j, i: (b * nq + i, j)),
            pl.BlockSpec((seq, LANES), lambda b, j, i: (b, j)),
            pl.BlockSpec((seq, LANES), lambda b, j, i: (b, pairs + j)),
            pl.BlockSpec((CHUNK, CHUNK), lambda b, j, i: (0, 0)),
        ],
        out_specs=pl.BlockSpec((tile, LANES), lambda b, j, i: (b * nq + i, j)),
        scratch_shapes=[
            pltpu.VMEM((2 * tile, LANES), F32),
            pltpu.VMEM((2 * tile, LANES), F32),
            pltpu.VMEM((2 * tile, tile), F32),
        ],
        compiler_params=_params("parallel", "parallel", "arbitrary"),
        name="stick_attention",
    )(q, kv, kv, _suffix_sum_matrix())


def _moe_route_kernel(x_ref, g_ref, sh_ref, sc_ref, rw_ref, rb_ref, h_ref, idx_ref, gate_ref):
    h = _adaln(x_ref[...], g_ref[...], sh_ref[...], sc_ref[...])
    h_ref[...] = h.astype(h_ref.dtype)
    logits = jnp.dot(h, rw_ref[...], precision=lax.Precision.HIGHEST,
                     preferred_element_type=F32) + rb_ref[...]
    lane = lax.broadcasted_iota(jnp.int32, logits.shape, 1)
    lane_f = lane.astype(F32)
    vals, ids = [], []
    for _ in range(TOP_K):
        m = jnp.max(logits, axis=-1, keepdims=True)
        first = jnp.min(jnp.where(logits == m, lane_f, float(LANES)), axis=-1, keepdims=True)
        vals.append(m)
        ids.append(first)
        logits = jnp.where(lane_f == first, -jnp.inf, logits)
    exps = [jnp.exp(v - vals[0]) for v in vals]
    denom = exps[0]
    for e in exps[1:]:
        denom = denom + e
    gate = jnp.zeros(logits.shape, F32)
    idx = jnp.zeros(logits.shape, F32)
    for k in range(TOP_K):
        gate = jnp.where(lane == k, exps[k] / denom, gate)
        idx = jnp.where(lane == k, ids[k], idx)
    gate_ref[...] = gate
    idx_ref[...] = idx.T[:SUBLANES].astype(jnp.int32)


def _moe_route(x, gain, shift, scale, rw_pad, rb_pad, seq):
    t, d = x.shape
    tm = min(TOKEN_TILE, seq)
    per_seq = seq // tm
    b = shift.shape[0]
    return pl.pallas_call(
        _moe_route_kernel,
        out_shape=(jax.ShapeDtypeStruct((t, d), BF16),
                   jax.ShapeDtypeStruct((SUBLANES, t), jnp.int32),
                   jax.ShapeDtypeStruct((t, LANES), F32)),
        grid=(t // tm,),
        in_specs=[
            pl.BlockSpec((tm, d), lambda i: (i, 0)),
            pl.BlockSpec((1, d), lambda i: (0, 0)),
            pl.BlockSpec((None, 1, d), lambda i: (i // per_seq, 0, 0)),
            pl.BlockSpec((None, 1, d), lambda i: (i // per_seq, 0, 0)),
            pl.BlockSpec((d, LANES), lambda i: (0, 0)),
            pl.BlockSpec((1, LANES), lambda i: (0, 0)),
        ],
        out_specs=(pl.BlockSpec((tm, d), lambda i: (i, 0)),
                   pl.BlockSpec((SUBLANES, tm), lambda i: (0, i)),
                   pl.BlockSpec((tm, LANES), lambda i: (i, 0))),
        compiler_params=_params("parallel"),
        name="moe_route",
    )(x, gain.reshape(1, d), shift.reshape(b, 1, d), scale.reshape(b, 1, d), rw_pad, rb_pad)


def _expert_kernel(block_e_ref, n_used_ref, h_ref, wgu_ref, bgu_ref, wd_ref, bd_ref, o_ref):
    i = pl.program_id(0)

    @pl.when(i < n_used_ref[0])
    def _():
        d_e = wd_ref.shape[0]
        gu = jnp.dot(h_ref[...], wgu_ref[...].astype(BF16), preferred_element_type=F32) + bgu_ref[...]
        g = jnp.minimum(gu[:, :d_e], SWIGLU_LIMIT)
        u = jnp.clip(gu[:, d_e:], -SWIGLU_LIMIT, SWIGLU_LIMIT)
        act = g / (1.0 + jnp.exp(-SWIGLU_ALPHA * g)) * (u + 1.0)
        y = jnp.dot(act.astype(BF16), wd_ref[...].astype(BF16), preferred_element_type=F32) + bd_ref[...]
        o_ref[...] = y.astype(o_ref.dtype)

    @pl.when(i >= n_used_ref[0])
    def _():
        o_ref[...] = jnp.zeros(o_ref.shape, o_ref.dtype)


def _expert_mlp(block_e, n_used, rows, wgu, bgu, wd, bd, layer, tile):
    n_rows, d = rows.shape
    n_l, n_e, _, two_de = wgu.shape
    d_e = two_de // 2
    n_blocks = n_rows // tile
    grid_spec = pltpu.PrefetchScalarGridSpec(
        num_scalar_prefetch=2,
        grid=(n_blocks,),
        in_specs=[
            pl.BlockSpec((tile, d), lambda i, be, nu: (i, 0)),
            pl.BlockSpec((None, None, d, two_de), lambda i, be, nu: (layer, be[i], 0, 0)),
            pl.BlockSpec((None, None, 1, two_de), lambda i, be, nu: (layer, be[i], 0, 0)),
            pl.BlockSpec((None, None, d_e, d), lambda i, be, nu: (layer, be[i], 0, 0)),
            pl.BlockSpec((None, None, 1, d), lambda i, be, nu: (layer, be[i], 0, 0)),
        ],
        out_specs=pl.BlockSpec((tile, d), lambda i, be, nu: (i, 0)),
    )
    return pl.pallas_call(
        _expert_kernel,
        out_shape=jax.ShapeDtypeStruct((n_rows, d), BF16),
        grid_spec=grid_spec,
        compiler_params=_params("arbitrary"),
        name="expert_mlp",
    )(block_e, n_used, rows, wgu, bgu.reshape(n_l, n_e, 1, two_de), wd, bd.reshape(n_l, n_e, 1, d))


def _combine_kernel(y_ref, w_ref, x_ref, g_ref, o_ref):
    w = w_ref[...]
    y = w[:, 0:1] * y_ref[0].astype(F32)
    for k in range(1, y_ref.shape[0]):
        y = y + w[:, k:k + 1] * y_ref[k].astype(F32)
    o_ref[...] = x_ref[...] + g_ref[...] * y


def _combine(y_k, weights, x, gate, seq):
    t, d = x.shape
    tm = min(TOKEN_TILE, seq)
    per_seq = seq // tm
    b = gate.shape[0]
    return pl.pallas_call(
        _combine_kernel,
        out_shape=jax.ShapeDtypeStruct((t, d), F32),
        grid=(t // tm,),
        in_specs=[
            pl.BlockSpec((y_k.shape[0], tm, d), lambda i: (0, i, 0)),
            pl.BlockSpec((tm, LANES), lambda i: (i, 0)),
            pl.BlockSpec((tm, d), lambda i: (i, 0)),
            pl.BlockSpec((None, 1, d), lambda i: (i // per_seq, 0, 0)),
        ],
        out_specs=pl.BlockSpec((tm, d), lambda i: (i, 0)),
        compiler_params=_params("parallel"),
        name="moe_combine",
    )(y_k, weights, x, gate.reshape(b, 1, d))


def _routing_tables(idx_t, n_experts, tile):
    top_k, t = idx_t.shape
    n_assign = top_k * t
    e_flat = idx_t.reshape(n_assign)
    experts = jnp.arange(n_experts, dtype=jnp.int32)
    onehot = e_flat[:, None] == experts[None, :]
    counts = jnp.sum(onehot, axis=0, dtype=jnp.int32)
    padded = (counts + tile - 1) // tile * tile
    start = jnp.cumsum(counts) - counts
    pend = jnp.cumsum(padded)
    pstart = pend - padded
    order = jnp.argsort(e_flat, stable=True).astype(jnp.int32)
    sorted_pos = jnp.argsort(order).astype(jnp.int32)
    shift = jnp.sum(jnp.where(onehot, (pstart - start)[None, :], 0), axis=1, dtype=jnp.int32)
    slot_row = sorted_pos + shift
    n_blocks = (n_assign + n_experts * (tile - 1) + tile - 1) // tile
    first_row = jnp.arange(n_blocks, dtype=jnp.int32) * tile
    block_e = jnp.minimum(jnp.sum(pend[None, :] <= first_row[:, None], axis=1), n_experts - 1)
    block_e = block_e.astype(jnp.int32)
    n_used = (pend[-1] // tile).astype(jnp.int32).reshape(1)
    within = (first_row - pstart[block_e])[:, None] + jnp.arange(tile, dtype=jnp.int32)[None, :]
    valid = within < counts[block_e][:, None]
    src = jnp.where(valid, start[block_e][:, None] + within, 0).reshape(-1)
    row_tok = jnp.where(valid.reshape(-1), order[src], jnp.arange(n_blocks * tile, dtype=jnp.int32)) % t
    return row_tok, block_e, n_used, slot_row


def _moe(x, gain, shift, scale, gate, rw, rb, wgu, bgu, wd, bd, layer, seq):
    t, d = x.shape
    n_e = rw.shape[1]
    rw_pad = jnp.pad(rw, ((0, 0), (0, LANES - n_e)))
    rb_pad = jnp.pad(rb, (0, LANES - n_e), constant_values=MASK_VALUE).reshape(1, LANES)
    h, idx_t, weights = _moe_route(x, gain, shift, scale, rw_pad, rb_pad, seq)
    tile = min(EXPERT_TILE, t * TOP_K // n_e)
    row_tok, block_e, n_used, slot_row = _routing_tables(idx_t[:TOP_K], n_e, tile)
    rows = h.at[row_tok].get(mode="promise_in_bounds")
    ys = _expert_mlp(block_e, n_used, rows, wgu, bgu, wd, bd, layer, tile)
    y_k = ys.at[slot_row].get(mode="promise_in_bounds").reshape(TOP_K, t, d)
    return _combine(y_k, weights, x, gate, seq)


def _rmsnorm_kernel(x_ref, g_ref, o_ref):
    x = x_ref[...]
    ms = jnp.mean(x * x, axis=-1, keepdims=True)
    o_ref[...] = x * lax.rsqrt(ms + EPS) * g_ref[...]


def _rmsnorm(x, gain):
    t, d = x.shape
    tm = min(TOKEN_TILE, t)
    return pl.pallas_call(
        _rmsnorm_kernel,
        out_shape=jax.ShapeDtypeStruct((t, d), F32),
        grid=(t // tm,),
        in_specs=[pl.BlockSpec((tm, d), lambda i: (i, 0)),
                  pl.BlockSpec((1, d), lambda i: (0, 0))],
        out_specs=pl.BlockSpec((tm, d), lambda i: (i, 0)),
        compiler_params=_params("parallel"),
        name="final_rmsnorm",
    )(x, gain.reshape(1, d))


def kernel(x, c, ada_w, ada_b, norm_mix, norm_moe, a_wqkv, a_wo, a_lambda, a_subln, kv_norm,
           kv_ada_w, kv_ada_b, kv_w, b_wq, b_wo, router_w, router_b, w_gate_up, b_gate_up,
           w_down, b_down, final_norm):
    b, s, d = x.shape
    depth = ada_w.shape[0]
    n_a = a_wqkv.shape[0]
    xt = x.reshape(b * s, d)

    c_pad = jnp.pad(c, ((0, SUBLANES - b), (0, 0)))
    mod = _modulation(c_pad, ada_w, ada_b)[:, :b]
    kv_mod = _modulation(c_pad, kv_ada_w[None], kv_ada_b[None])[0, :b]

    for l in range(depth):
        sh1, sc1, g1, sh2, sc2, g2 = jnp.split(mod[l], 6, axis=-1)
        if l < n_a:
            qkv = _norm_linear(xt, norm_mix[l], sh1, sc1, a_wqkv[l].astype(BF16), s)
            o = _diff_attention(qkv, a_lambda[l], a_subln[l], b, s, l)
            xt = _proj_residual(o, a_wo[l].astype(BF16), xt, g1, s)
        else:
            if l == n_a:
                kv_sh, kv_sc = jnp.split(kv_mod, 2, axis=-1)
                kv = _norm_linear(xt, kv_norm, kv_sh, kv_sc, kv_w.astype(BF16), s)
            j = l - n_a
            q = _norm_linear(xt, norm_mix[l], sh1, sc1, b_wq[j].astype(BF16), s)
            o = _stick_attention(q, kv, b, s)
            xt = _proj_residual(o, b_wo[j].astype(BF16), xt, g1, s)
        xt = _moe(xt, norm_moe[l], sh2, sc2, g2, router_w[l], router_b[l],
                  w_gate_up, b_gate_up, w_down, b_down, l, s)
    return _rmsnorm(xt, final_norm).reshape(b, s, d)
```

```python
import functools
import math

import jax
import jax.numpy as jnp
import numpy as np
from jax import lax
from jax.experimental import pallas as pl
from jax.experimental.pallas import tpu as pltpu
from jax.experimental.pallas import tpu_sc as plsc

F32 = jnp.float32
BF16 = jnp.bfloat16
EPS = 1e-6
MASK_VALUE = -1e30
TOP_K = 4
SWIGLU_ALPHA = 1.702
SWIGLU_LIMIT = 7.0
LOG2E = math.log2(math.e)

LANES = 128
SUBLANES = 8
HEAD_DIM = 64
V7X_VMEM_BYTES = 64 * 2 ** 20
VMEM_LIMIT = V7X_VMEM_BYTES * 7 // 8

TOKEN_TILE = 512
ATTN_TILE = 512
EXPERT_TILE = 512


def _params(*semantics):
    return pltpu.CompilerParams(dimension_semantics=semantics, vmem_limit_bytes=VMEM_LIMIT)


def _largest_tile(n, cap):
    t = min(n, cap) // LANES * LANES
    while n % t:
        t -= LANES
    return t


def _adaln(x, gain, shift, scale):
    ms = jnp.mean(x * x, axis=-1, keepdims=True)
    return x * lax.rsqrt(ms + EPS) * gain * (1.0 + scale) + shift


def _pack_bf16_pair(lo, hi):
    lo_bits = lax.bitcast_convert_type(lo.astype(BF16).astype(F32), jnp.uint32) >> 16
    hi_bits = lax.bitcast_convert_type(hi.astype(BF16).astype(F32), jnp.uint32) & jnp.uint32(0xFFFF0000)
    return lo_bits | hi_bits


def _unpack_bf16_pair(packed):
    lo = lax.bitcast_convert_type(packed << 16, F32)
    hi = lax.bitcast_convert_type(packed & jnp.uint32(0xFFFF0000), F32)
    return lo, hi


def _pack_rows(x):
    q = x.shape[1] // 4
    return (_pack_bf16_pair(x[:, :q], x[:, 2 * q:3 * q]),
            _pack_bf16_pair(x[:, q:2 * q], x[:, 3 * q:]))


def _unpack_rows(p0, p1):
    a0, a2 = _unpack_bf16_pair(p0)
    a1, a3 = _unpack_bf16_pair(p1)
    return a0, a1, a2, a3


def _mod_kernel(c_ref, w_ref, b_ref, o_ref):
    c = c_ref[...]
    a = c / (1.0 + jnp.exp(-c))
    o_ref[...] = jnp.dot(a, w_ref[...], precision=lax.Precision.HIGHEST,
                         preferred_element_type=F32) + b_ref[...]


def _modulation(c_pad, w, b):
    n_l, d, n = w.shape
    tn = _largest_tile(n, 2048)
    return pl.pallas_call(
        _mod_kernel,
        out_shape=jax.ShapeDtypeStruct((n_l, SUBLANES, n), F32),
        grid=(n_l, n // tn),
        in_specs=[
            pl.BlockSpec((SUBLANES, d), lambda l, j: (0, 0)),
            pl.BlockSpec((None, d, tn), lambda l, j: (l, 0, j)),
            pl.BlockSpec((None, 1, tn), lambda l, j: (l, 0, j)),
        ],
        out_specs=pl.BlockSpec((None, SUBLANES, tn), lambda l, j: (l, 0, j)),
        compiler_params=_params("parallel", "parallel"),
        name="modulation",
    )(c_pad, w, b.reshape(n_l, 1, n))


def _norm_linear_kernel(x_ref, g_ref, sh_ref, sc_ref, w_ref, o_ref):
    h = _adaln(x_ref[...], g_ref[...], sh_ref[...], sc_ref[...])
    o_ref[...] = jnp.dot(h.astype(BF16), w_ref[...], preferred_element_type=F32).astype(o_ref.dtype)


def _norm_linear(x, gain, shift, scale, w, seq):
    t, d = x.shape
    n = w.shape[1]
    tm = min(TOKEN_TILE, seq)
    per_seq = seq // tm
    b = shift.shape[0]
    return pl.pallas_call(
        _norm_linear_kernel,
        out_shape=jax.ShapeDtypeStruct((t, n), BF16),
        grid=(t // tm,),
        in_specs=[
            pl.BlockSpec((tm, d), lambda i: (i, 0)),
            pl.BlockSpec((1, d), lambda i: (0, 0)),
            pl.BlockSpec((None, 1, d), lambda i: (i // per_seq, 0, 0)),
            pl.BlockSpec((None, 1, d), lambda i: (i // per_seq, 0, 0)),
            pl.BlockSpec((d, n), lambda i: (0, 0)),
        ],
        out_specs=pl.BlockSpec((tm, n), lambda i: (i, 0)),
        compiler_params=_params("parallel"),
        name="norm_linear",
    )(x, gain.reshape(1, d), shift.reshape(b, 1, d), scale.reshape(b, 1, d), w)


def _proj_residual_kernel(a_ref, w_ref, x_ref, g_ref, o_ref):
    y = jnp.dot(a_ref[...], w_ref[...], preferred_element_type=F32)
    o_ref[...] = x_ref[...] + g_ref[...] * y


def _proj_residual(a, w, x, gate, seq):
    t, d = x.shape
    k = a.shape[1]
    tm = min(TOKEN_TILE, seq)
    per_seq = seq // tm
    b = gate.shape[0]
    return pl.pallas_call(
        _proj_residual_kernel,
        out_shape=jax.ShapeDtypeStruct((t, d), F32),
        grid=(t // tm,),
        in_specs=[
            pl.BlockSpec((tm, k), lambda i: (i, 0)),
            pl.BlockSpec((k, d), lambda i: (0, 0)),
            pl.BlockSpec((tm, d), lambda i: (i, 0)),
            pl.BlockSpec((None, 1, d), lambda i: (i // per_seq, 0, 0)),
        ],
        out_specs=pl.BlockSpec((tm, d), lambda i: (i, 0)),
        compiler_params=_params("parallel"),
        name="proj_residual",
    )(a, w, x, gate.reshape(b, 1, d))


def _scaled_halves(q_ref):
    q = (q_ref[...].astype(F32) * (HEAD_DIM ** -0.5 * LOG2E)).astype(BF16)
    lane = lax.broadcasted_iota(jnp.int32, q.shape, 1)
    zero = jnp.zeros_like(q)
    return jnp.concatenate([jnp.where(lane < HEAD_DIM, q, zero),
                            jnp.where(lane >= HEAD_DIM, q, zero)], axis=0)


def _dot_nt(a, b):
    return lax.dot_general(a, b, (((1,), (1,)), ((), ())), preferred_element_type=F32)


def _tile_row_col(tq, tk):
    row = lax.broadcasted_iota(jnp.int32, (2 * tq, tk), 0)
    row = jnp.where(row >= tq, row - tq, row)
    col = lax.broadcasted_iota(jnp.int32, (2 * tq, tk), 1)
    return row, col


def _lane_chunks(n):
    return [slice(c * LANES, (c + 1) * LANES) for c in range(n // LANES)]


def _diff_attn_kernel(q_ref, k_ref, v_ref, slope_ref, lam_ref, subln_ref, o_ref,
                      m_ref, l_ref, acc_ref, s_ref, *, tile, lam_init):
    i = pl.program_id(2)
    qq = _scaled_halves(q_ref)
    slope = slope_ref[...]
    col_i = lax.broadcasted_iota(jnp.int32, (1, tile), 1)
    ones = jnp.ones((tile, LANES), BF16)
    m_ref[...] = jnp.full(m_ref.shape, MASK_VALUE, F32)
    l_ref[...] = jnp.zeros(l_ref.shape, F32)
    acc_ref[...] = jnp.zeros(acc_ref.shape, F32)

    def scores(kb):
        k = k_ref[pl.ds(pl.multiple_of(kb * tile, tile), tile), :]
        return _dot_nt(qq, k) + slope * (col_i + (kb - i) * tile).astype(F32)

    def accumulate(s, kb, diagonal):
        v = v_ref[pl.ds(pl.multiple_of(kb * tile, tile), tile), :]
        if diagonal:
            row, col = _tile_row_col(tile, tile)
            s = jnp.where(col <= row, s, MASK_VALUE)
        m_prev = m_ref[...]
        m_new = jnp.maximum(m_prev, jnp.max(s, axis=-1, keepdims=True))
        alpha = jnp.exp2(m_prev - m_new)
        p = jnp.concatenate([jnp.exp2(s[:, sl] - m_new).astype(BF16) for sl in _lane_chunks(tile)],
                            axis=1)
        pv = jnp.dot(p, jnp.concatenate([v, ones], axis=1),
                     preferred_element_type=F32)
        acc_ref[...] = alpha * acc_ref[...] + pv[:, :LANES]
        l_ref[...] = alpha * l_ref[...] + pv[:, LANES:]
        m_ref[...] = m_new

    s_ref[...] = scores(0)

    def body(kb, carry):
        s = s_ref[...]
        s_ref[...] = scores(kb + 1)
        accumulate(s, kb, False)
        return carry

    lax.fori_loop(0, i, body, 0)
    accumulate(s_ref[...], i, True)

    o_all = acc_ref[...] / l_ref[...]
    lv = lam_ref[...]
    lam = (jnp.exp(jnp.sum(lv[0:1] * lv[1:2], axis=1, keepdims=True))
           - jnp.exp(jnp.sum(lv[2:3] * lv[3:4], axis=1, keepdims=True)) + lam_init)
    o = o_all[:tile] - lam * o_all[tile:]
    ms = jnp.mean(o * o, axis=-1, keepdims=True)
    o = o * lax.rsqrt(ms + EPS) * subln_ref[...] * (1.0 - lam_init)
    o_ref[...] = o.astype(o_ref.dtype)


def _diff_attention(qkv, lam_vecs, subln, batch, seq, layer_idx):
    t = qkv.shape[0]
    d = qkv.shape[1] // 3
    heads = d // LANES
    tile = min(ATTN_TILE, seq)
    nq = seq // tile
    lam_init = 0.8 - 0.6 * math.exp(-0.3 * layer_idx)
    slopes = np.array([2.0 ** (-8.0 * (h + 1) / heads) * LOG2E for h in range(heads)], np.float32)
    slopes = jnp.asarray(np.broadcast_to(slopes[:, None, None], (heads, 1, tile)))
    kernel = functools.partial(_diff_attn_kernel, tile=tile, lam_init=lam_init)
    return pl.pallas_call(
        kernel,
        out_shape=jax.ShapeDtypeStruct((t, d), BF16),
        grid=(batch, heads, nq),
        in_specs=[
            pl.BlockSpec((tile, LANES), lambda b, h, i: (b * nq + i, h)),
            pl.BlockSpec((seq, LANES), lambda b, h, i: (b, heads + h)),
            pl.BlockSpec((seq, LANES), lambda b, h, i: (b, 2 * heads + h)),
            pl.BlockSpec((None, 1, tile), lambda b, h, i: (h, 0, 0)),
            pl.BlockSpec(lam_vecs.shape, lambda b, h, i: (0, 0)),
            pl.BlockSpec((1, LANES), lambda b, h, i: (0, 0)),
        ],
        out_specs=pl.BlockSpec((tile, LANES), lambda b, h, i: (b * nq + i, h)),
        scratch_shapes=[
            pltpu.VMEM((2 * tile, LANES), F32),
            pltpu.VMEM((2 * tile, LANES), F32),
            pltpu.VMEM((2 * tile, LANES), F32),
            pltpu.VMEM((2 * tile, tile), F32),
        ],
        compiler_params=_params("parallel", "parallel", "arbitrary"),
        name="diff_attention",
    )(qkv, qkv, qkv, slopes, lam_vecs, subln.reshape(1, LANES))


CHUNK = 256


def _suffix_sum_matrix():
    j = np.arange(CHUNK)
    return jnp.asarray((j[:, None] >= j[None, :]).astype(np.float32), dtype=BF16)


def _stick_attn_kernel(q_ref, k_ref, v_ref, mt_ref, o_ref, rem_ref, acc_ref, z_ref, *, tile):
    i = pl.program_id(2)
    qq = _scaled_halves(q_ref)
    mt = mt_ref[...]
    rem_ref[...] = jnp.zeros(rem_ref.shape, F32)
    acc_ref[...] = jnp.zeros(acc_ref.shape, F32)

    def scores(kb):
        return _dot_nt(qq, k_ref[pl.ds(pl.multiple_of(kb * tile, tile), tile), :])

    def accumulate(z, kb, diagonal):
        v = v_ref[pl.ds(pl.multiple_of(kb * tile, tile), tile), :]
        softplus = jnp.maximum(z, 0.0) + jnp.log2(1.0 + jnp.exp2(-jnp.abs(z)))
        if diagonal:
            row, col = _tile_row_col(tile, tile)
            strict = col < row
            softplus = jnp.where(strict, softplus, 0.0)
        rem = rem_ref[...]
        chunks = []
        for c in reversed(range(tile // CHUNK)):
            x = softplus[:, c * CHUNK:(c + 1) * CHUNK].astype(BF16)
            sums = jnp.dot(x, mt, preferred_element_type=F32)
            for h in reversed(range(CHUNK // LANES)):
                sl = slice(c * CHUNK + h * LANES, c * CHUNK + (h + 1) * LANES)
                a = jnp.exp2(z[:, sl] - sums[:, h * LANES:(h + 1) * LANES] - rem)
                if diagonal:
                    a = jnp.where(strict[:, sl], a, 0.0)
                chunks.append(a.astype(BF16))
            rem = rem + jnp.broadcast_to(sums[:, 0:1], rem.shape)
        rem_ref[...] = rem
        acc_ref[...] += jnp.dot(jnp.concatenate(chunks[::-1], axis=1), v, preferred_element_type=F32)

    z_ref[...] = scores(i)

    @pl.when(i > 0)
    def _():
        z = z_ref[...]
        z_ref[...] = scores(i - 1)
        accumulate(z, i, True)

        def body(n, carry):
            z = z_ref[...]
            z_ref[...] = scores(i - 1 - n)
            accumulate(z, i - n, False)
            return carry

        lax.fori_loop(1, i, body, 0)
        accumulate(z_ref[...], 0, False)

    @pl.when(i == 0)
    def _():
        accumulate(z_ref[...], 0, True)

    acc = acc_ref[...]
    lane = lax.broadcasted_iota(jnp.int32, (tile, LANES), 1)
    o_ref[...] = jnp.where(lane < HEAD_DIM, acc[:tile], acc[tile:]).astype(o_ref.dtype)


def _stick_attention(q, kv, batch, seq):
    t, d = q.shape
    pairs = d // LANES
    tile = min(ATTN_TILE, seq)
    nq = seq // tile
    kernel = functools.partial(_stick_attn_kernel, tile=tile)
    return pl.pallas_call(
        kernel,
        out_shape=jax.ShapeDtypeStruct((t, d), BF16),
        grid=(batch, pairs, nq),
        in_specs=[
            pl.BlockSpec((tile, LANES), lambda b, j, i: (b * nq + i, j)),
            pl.BlockSpec((seq, LANES), lambda b, j, i: (b, j)),
            pl.BlockSpec((seq, LANES), lambda b, j, i: (b, pairs + j)),
            pl.BlockSpec((CHUNK, CHUNK), lambda b, j, i: (0, 0)),
        ],
        out_specs=pl.BlockSpec((tile, LANES), lambda b, j, i: (b * nq + i, j)),
        scratch_shapes=[
            pltpu.VMEM((2 * tile, LANES), F32),
            pltpu.VMEM((2 * tile, LANES), F32),
            pltpu.VMEM((2 * tile, tile), F32),
        ],
        compiler_params=_params("parallel", "parallel", "arbitrary"),
        name="stick_attention",
    )(q, kv, kv, _suffix_sum_matrix())


SC_WINDOW = 128


def _sc_gather_rows(tab0, tab1, idx):
    m = idx.shape[0]
    w = tab0.shape[1]
    mesh = plsc.VectorSubcoreMesh(core_axis_name="core", subcore_axis_name="subcore")
    assert m % (SC_WINDOW * mesh.num_cores * mesh.num_subcores) == 0, m
    out = jax.ShapeDtypeStruct((m, w), tab0.dtype)

    @pl.kernel(out_type=(out, out), mesh=mesh)
    def gather(t0_hbm, t1_hbm, i_hbm, o0_hbm, o1_hbm):
        for t_hbm, o_hbm in ((t0_hbm, o0_hbm), (t1_hbm, o1_hbm)):
            def body(i_vmem, o_vmem, t_hbm=t_hbm):
                pltpu.sync_copy(t_hbm.at[i_vmem.at[0]], o_vmem)

            pltpu.emit_pipeline(
                body,
                grid=(m // SC_WINDOW,),
                in_specs=[pl.BlockSpec((1, SC_WINDOW), lambda i: (0, i))],
                out_specs=[pl.BlockSpec((SC_WINDOW, w), lambda i: (i, 0))],
                core_axis_name=("core", "subcore"),
                dimension_semantics=(pltpu.PARALLEL,),
            )(i_hbm, o_hbm)

    return gather(tab0, tab1, idx.reshape(1, m))


def _moe_route_kernel(x_ref, g_ref, sh_ref, sc_ref, rw_ref, rb_ref, h0_ref, h1_ref, idx_ref, gate_ref):
    h = _adaln(x_ref[...], g_ref[...], sh_ref[...], sc_ref[...])
    h0_ref[...], h1_ref[...] = _pack_rows(h)
    logits = jnp.dot(h, rw_ref[...], precision=lax.Precision.HIGHEST,
                     preferred_element_type=F32) + rb_ref[...]
    lane = lax.broadcasted_iota(jnp.int32, logits.shape, 1)
    lane_f = lane.astype(F32)
    vals, ids = [], []
    for _ in range(TOP_K):
        m = jnp.max(logits, axis=-1, keepdims=True)
        first = jnp.min(jnp.where(logits == m, lane_f, float(LANES)), axis=-1, keepdims=True)
        vals.append(m)
        ids.append(first)
        logits = jnp.where(lane_f == first, -jnp.inf, logits)
    exps = [jnp.exp(v - vals[0]) for v in vals]
    denom = exps[0]
    for e in exps[1:]:
        denom = denom + e
    gate = jnp.zeros(logits.shape, F32)
    idx = jnp.zeros(logits.shape, F32)
    for k in range(TOP_K):
        gate = jnp.where(lane == k, exps[k] / denom, gate)
        idx = jnp.where(lane == k, ids[k], idx)
    gate_ref[...] = gate
    idx_ref[...] = idx.T[:SUBLANES].astype(jnp.int32)


def _moe_route(x, gain, shift, scale, rw_pad, rb_pad, seq):
    t, d = x.shape
    tm = min(TOKEN_TILE, seq)
    per_seq = seq // tm
    b = shift.shape[0]
    return pl.pallas_call(
        _moe_route_kernel,
        out_shape=(jax.ShapeDtypeStruct((t, d // 4), jnp.uint32),
                   jax.ShapeDtypeStruct((t, d // 4), jnp.uint32),
                   jax.ShapeDtypeStruct((SUBLANES, t), jnp.int32),
                   jax.ShapeDtypeStruct((t, LANES), F32)),
        grid=(t // tm,),
        in_specs=[
            pl.BlockSpec((tm, d), lambda i: (i, 0)),
            pl.BlockSpec((1, d), lambda i: (0, 0)),
            pl.BlockSpec((None, 1, d), lambda i: (i // per_seq, 0, 0)),
            pl.BlockSpec((None, 1, d), lambda i: (i // per_seq, 0, 0)),
            pl.BlockSpec((d, LANES), lambda i: (0, 0)),
            pl.BlockSpec((1, LANES), lambda i: (0, 0)),
        ],
        out_specs=(pl.BlockSpec((tm, d // 4), lambda i: (i, 0)),
                   pl.BlockSpec((tm, d // 4), lambda i: (i, 0)),
                   pl.BlockSpec((SUBLANES, tm), lambda i: (0, i)),
                   pl.BlockSpec((tm, LANES), lambda i: (i, 0))),
        compiler_params=_params("parallel"),
        name="moe_route",
    )(x, gain.reshape(1, d), shift.reshape(b, 1, d), scale.reshape(b, 1, d), rw_pad, rb_pad)


def _expert_kernel(block_e_ref, n_used_ref, h0_ref, h1_ref, wgu_ref, bgu_ref, wd_ref, bd_ref,
                   o0_ref, o1_ref):
    i = pl.program_id(0)

    @pl.when(i < n_used_ref[0])
    def _():
        d_e = wd_ref.shape[0]
        h = jnp.concatenate([q.astype(BF16) for q in _unpack_rows(h0_ref[...], h1_ref[...])], axis=1)
        gu = jnp.dot(h, wgu_ref[...].astype(BF16), preferred_element_type=F32) + bgu_ref[...]
        g = jnp.minimum(gu[:, :d_e], SWIGLU_LIMIT)
        u = jnp.clip(gu[:, d_e:], -SWIGLU_LIMIT, SWIGLU_LIMIT)
        act = g / (1.0 + jnp.exp(-SWIGLU_ALPHA * g)) * (u + 1.0)
        y = jnp.dot(act.astype(BF16), wd_ref[...].astype(BF16), preferred_element_type=F32) + bd_ref[...]
        o0_ref[...], o1_ref[...] = _pack_rows(y)

    @pl.when(i >= n_used_ref[0])
    def _():
        o0_ref[...] = jnp.zeros(o0_ref.shape, o0_ref.dtype)
        o1_ref[...] = jnp.zeros(o1_ref.shape, o1_ref.dtype)


def _expert_mlp(block_e, n_used, rows0, rows1, wgu, bgu, wd, bd, layer, tile):
    n_rows, dq = rows0.shape
    d = 4 * dq
    n_l, n_e, _, two_de = wgu.shape
    d_e = two_de // 2
    n_blocks = n_rows // tile
    grid_spec = pltpu.PrefetchScalarGridSpec(
        num_scalar_prefetch=2,
        grid=(n_blocks,),
        in_specs=[
            pl.BlockSpec((tile, dq), lambda i, be, nu: (i, 0)),
            pl.BlockSpec((tile, dq), lambda i, be, nu: (i, 0)),
            pl.BlockSpec((None, None, d, two_de), lambda i, be, nu: (layer, be[i], 0, 0)),
            pl.BlockSpec((None, None, 1, two_de), lambda i, be, nu: (layer, be[i], 0, 0)),
            pl.BlockSpec((None, None, d_e, d), lambda i, be, nu: (layer, be[i], 0, 0)),
            pl.BlockSpec((None, None, 1, d), lambda i, be, nu: (layer, be[i], 0, 0)),
        ],
        out_specs=(pl.BlockSpec((tile, dq), lambda i, be, nu: (i, 0)),
                   pl.BlockSpec((tile, dq), lambda i, be, nu: (i, 0))),
    )
    packed = jax.ShapeDtypeStruct((n_rows, dq), jnp.uint32)
    return pl.pallas_call(
        _expert_kernel,
        out_shape=(packed, packed),
        grid_spec=grid_spec,
        compiler_params=_params("arbitrary"),
        name="expert_mlp",
    )(block_e, n_used, rows0, rows1, wgu, bgu.reshape(n_l, n_e, 1, two_de), wd,
      bd.reshape(n_l, n_e, 1, d))


def _combine_kernel(y0_ref, y1_ref, w_ref, x_ref, g_ref, o_ref):
    w = w_ref[...]
    acc = None
    for k in range(y0_ref.shape[0]):
        quarters = _unpack_rows(y0_ref[k], y1_ref[k])
        scaled = [w[:, k:k + 1] * q for q in quarters]
        acc = scaled if acc is None else [a + s for a, s in zip(acc, scaled)]
    o_ref[...] = x_ref[...] + g_ref[...] * jnp.concatenate(acc, axis=1)


def _combine(y0, y1, weights, x, gate, seq):
    t, d = x.shape
    tm = min(TOKEN_TILE, seq)
    per_seq = seq // tm
    b = gate.shape[0]
    return pl.pallas_call(
        _combine_kernel,
        out_shape=jax.ShapeDtypeStruct((t, d), F32),
        grid=(t // tm,),
        in_specs=[
            pl.BlockSpec((y0.shape[0], tm, d // 4), lambda i: (0, i, 0)),
            pl.BlockSpec((y0.shape[0], tm, d // 4), lambda i: (0, i, 0)),
            pl.BlockSpec((tm, LANES), lambda i: (i, 0)),
            pl.BlockSpec((tm, d), lambda i: (i, 0)),
            pl.BlockSpec((None, 1, d), lambda i: (i // per_seq, 0, 0)),
        ],
        out_specs=pl.BlockSpec((tm, d), lambda i: (i, 0)),
        compiler_params=_params("parallel"),
        name="moe_combine",
    )(y0, y1, weights, x, gate.reshape(b, 1, d))


ROW_BLOCK_MULTIPLE = 8


def _routing_tables(idx_t, n_experts, tile):
    top_k, t = idx_t.shape
    n_assign = top_k * t
    e_flat = idx_t.reshape(n_assign)
    experts = jnp.arange(n_experts, dtype=jnp.int32)
    onehot = e_flat[:, None] == experts[None, :]
    counts = jnp.sum(onehot, axis=0, dtype=jnp.int32)
    padded = (counts + tile - 1) // tile * tile
    start = jnp.cumsum(counts) - counts
    pend = jnp.cumsum(padded)
    pstart = pend - padded
    order = jnp.argsort(e_flat, stable=True).astype(jnp.int32)
    sorted_pos = jnp.argsort(order).astype(jnp.int32)
    shift = jnp.sum(jnp.where(onehot, (pstart - start)[None, :], 0), axis=1, dtype=jnp.int32)
    slot_row = sorted_pos + shift
    n_blocks = (n_assign + n_experts * (tile - 1) + tile - 1) // tile
    n_blocks = -(-n_blocks // ROW_BLOCK_MULTIPLE) * ROW_BLOCK_MULTIPLE
    first_row = jnp.arange(n_blocks, dtype=jnp.int32) * tile
    block_e = jnp.minimum(jnp.sum(pend[None, :] <= first_row[:, None], axis=1), n_experts - 1)
    block_e = block_e.astype(jnp.int32)
    n_used = (pend[-1] // tile).astype(jnp.int32).reshape(1)
    within = (first_row - pstart[block_e])[:, None] + jnp.arange(tile, dtype=jnp.int32)[None, :]
    valid = within < counts[block_e][:, None]
    src = jnp.where(valid, start[block_e][:, None] + within, 0).reshape(-1)
    row_tok = jnp.where(valid.reshape(-1), order[src], jnp.arange(n_blocks * tile, dtype=jnp.int32)) % t
    return row_tok, block_e, n_used, slot_row


def _moe(x, gain, shift, scale, gate, rw, rb, wgu, bgu, wd, bd, layer, seq):
    t, d = x.shape
    n_e = rw.shape[1]
    rw_pad = jnp.pad(rw, ((0, 0), (0, LANES - n_e)))
    rb_pad = jnp.pad(rb, (0, LANES - n_e), constant_values=MASK_VALUE).reshape(1, LANES)
    h0, h1, idx_t, weights = _moe_route(x, gain, shift, scale, rw_pad, rb_pad, seq)
    tile = min(EXPERT_TILE, t * TOP_K // n_e)
    row_tok, block_e, n_used, slot_row = _routing_tables(idx_t[:TOP_K], n_e, tile)
    rows0, rows1 = _sc_gather_rows(h0, h1, row_tok)
    ys0, ys1 = _expert_mlp(block_e, n_used, rows0, rows1, wgu, bgu, wd, bd, layer, tile)
    y0, y1 = _sc_gather_rows(ys0, ys1, slot_row)
    return _combine(y0.reshape(TOP_K, t, d // 4), y1.reshape(TOP_K, t, d // 4), weights, x, gate, seq)


def _rmsnorm_kernel(x_ref, g_ref, o_ref):
    x = x_ref[...]
    ms = jnp.mean(x * x, axis=-1, keepdims=True)
    o_ref[...] = x * lax.rsqrt(ms + EPS) * g_ref[...]


def _rmsnorm(x, gain):
    t, d = x.shape
    tm = min(TOKEN_TILE, t)
    return pl.pallas_call(
        _rmsnorm_kernel,
        out_shape=jax.ShapeDtypeStruct((t, d), F32),
        grid=(t // tm,),
        in_specs=[pl.BlockSpec((tm, d), lambda i: (i, 0)),
                  pl.BlockSpec((1, d), lambda i: (0, 0))],
        out_specs=pl.BlockSpec((tm, d), lambda i: (i, 0)),
        compiler_params=_params("parallel"),
        name="final_rmsnorm",
    )(x, gain.reshape(1, d))


def kernel(x, c, ada_w, ada_b, norm_mix, norm_moe, a_wqkv, a_wo, a_lambda, a_subln, kv_norm,
           kv_ada_w, kv_ada_b, kv_w, b_wq, b_wo, router_w, router_b, w_gate_up, b_gate_up,
           w_down, b_down, final_norm):
    b, s, d = x.shape
    depth = ada_w.shape[0]
    n_a = a_wqkv.shape[0]
    xt = x.reshape(b * s, d)

    c_pad = jnp.pad(c, ((0, SUBLANES - b), (0, 0)))
    mod = _modulation(c_pad, ada_w, ada_b)[:, :b]
    kv_mod = _modulation(c_pad, kv_ada_w[None], kv_ada_b[None])[0, :b]

    for l in range(depth):
        sh1, sc1, g1, sh2, sc2, g2 = jnp.split(mod[l], 6, axis=-1)
        if l < n_a:
            qkv = _norm_linear(xt, norm_mix[l], sh1, sc1, a_wqkv[l].astype(BF16), s)
            o = _diff_attention(qkv, a_lambda[l], a_subln[l], b, s, l)
            xt = _proj_residual(o, a_wo[l].astype(BF16), xt, g1, s)
        else:
            if l == n_a:
                kv_sh, kv_sc = jnp.split(kv_mod, 2, axis=-1)
                kv = _norm_linear(xt, kv_norm, kv_sh, kv_sc, kv_w.astype(BF16), s)
            j = l - n_a
            q = _norm_linear(xt, norm_mix[l], sh1, sc1, b_wq[j].astype(BF16), s)
            o = _stick_attention(q, kv, b, s)
            xt = _proj_residual(o, b_wo[j].astype(BF16), xt, g1, s)
        xt = _moe(xt, norm_moe[l], sh2, sc2, g2, router_w[l], router_b[l],
                  w_gate_up, b_gate_up, w_down, b_down, l, s)
    return _rmsnorm(xt, final_norm).reshape(b, s, d)
```

```python
import functools
import math

import jax
import jax.numpy as jnp
import numpy as np
from jax import lax
from jax.experimental import pallas as pl
from jax.experimental.pallas import tpu as pltpu
from jax.experimental.pallas import tpu_sc as plsc

F32 = jnp.float32
BF16 = jnp.bfloat16
EPS = 1e-6
MASK_VALUE = -1e30
TOP_K = 4
SWIGLU_ALPHA = 1.702
SWIGLU_LIMIT = 7.0
LOG2E = math.log2(math.e)

LANES = 128
SUBLANES = 8
HEAD_DIM = 64
V7X_VMEM_BYTES = 64 * 2 ** 20
VMEM_LIMIT = V7X_VMEM_BYTES * 7 // 8

TOKEN_TILE = 512
ATTN_TILE = 512
EXPERT_TILE = 512


def _params(*semantics):
    return pltpu.CompilerParams(dimension_semantics=semantics, vmem_limit_bytes=VMEM_LIMIT)


def _largest_tile(n, cap):
    t = min(n, cap) // LANES * LANES
    while n % t:
        t -= LANES
    return t


def _adaln(x, gain, shift, scale):
    ms = jnp.mean(x * x, axis=-1, keepdims=True)
    return x * lax.rsqrt(ms + EPS) * gain * (1.0 + scale) + shift


def _pack_bf16_pair(lo, hi):
    lo_bits = lax.bitcast_convert_type(lo.astype(BF16).astype(F32), jnp.uint32) >> 16
    hi_bits = lax.bitcast_convert_type(hi.astype(BF16).astype(F32), jnp.uint32) & jnp.uint32(0xFFFF0000)
    return lo_bits | hi_bits


def _unpack_bf16_pair(packed):
    lo = lax.bitcast_convert_type(packed << 16, F32)
    hi = lax.bitcast_convert_type(packed & jnp.uint32(0xFFFF0000), F32)
    return lo, hi


def _pack_rows(x):
    q = x.shape[1] // 4
    return (_pack_bf16_pair(x[:, :q], x[:, 2 * q:3 * q]),
            _pack_bf16_pair(x[:, q:2 * q], x[:, 3 * q:]))


def _unpack_rows(p0, p1):
    a0, a2 = _unpack_bf16_pair(p0)
    a1, a3 = _unpack_bf16_pair(p1)
    return a0, a1, a2, a3


def _mod_kernel(c_ref, w_ref, b_ref, o_ref):
    c = c_ref[...]
    a = c / (1.0 + jnp.exp(-c))
    o_ref[...] = jnp.dot(a, w_ref[...], precision=lax.Precision.HIGHEST,
                         preferred_element_type=F32) + b_ref[...]


def _modulation(c_pad, w, b):
    n_l, d, n = w.shape
    tn = _largest_tile(n, 2048)
    return pl.pallas_call(
        _mod_kernel,
        out_shape=jax.ShapeDtypeStruct((n_l, SUBLANES, n), F32),
        grid=(n_l, n // tn),
        in_specs=[
            pl.BlockSpec((SUBLANES, d), lambda l, j: (0, 0)),
            pl.BlockSpec((None, d, tn), lambda l, j: (l, 0, j)),
            pl.BlockSpec((None, 1, tn), lambda l, j: (l, 0, j)),
        ],
        out_specs=pl.BlockSpec((None, SUBLANES, tn), lambda l, j: (l, 0, j)),
        compiler_params=_params("parallel", "parallel"),
        name="modulation",
    )(c_pad, w, b.reshape(n_l, 1, n))


def _norm_linear_kernel(x_ref, g_ref, sh_ref, sc_ref, w_ref, o_ref):
    h = _adaln(x_ref[...], g_ref[...], sh_ref[...], sc_ref[...])
    o_ref[...] = jnp.dot(h.astype(BF16), w_ref[...], preferred_element_type=F32).astype(o_ref.dtype)


def _norm_linear(x, gain, shift, scale, w, seq):
    t, d = x.shape
    n = w.shape[1]
    tm = min(TOKEN_TILE, seq)
    per_seq = seq // tm
    b = shift.shape[0]
    return pl.pallas_call(
        _norm_linear_kernel,
        out_shape=jax.ShapeDtypeStruct((t, n), BF16),
        grid=(t // tm,),
        in_specs=[
            pl.BlockSpec((tm, d), lambda i: (i, 0)),
            pl.BlockSpec((1, d), lambda i: (0, 0)),
            pl.BlockSpec((None, 1, d), lambda i: (i // per_seq, 0, 0)),
            pl.BlockSpec((None, 1, d), lambda i: (i // per_seq, 0, 0)),
            pl.BlockSpec((d, n), lambda i: (0, 0)),
        ],
        out_specs=pl.BlockSpec((tm, n), lambda i: (i, 0)),
        compiler_params=_params("parallel"),
        name="norm_linear",
    )(x, gain.reshape(1, d), shift.reshape(b, 1, d), scale.reshape(b, 1, d), w)


def _proj_residual_kernel(a_ref, w_ref, x_ref, g_ref, o_ref):
    y = jnp.dot(a_ref[...], w_ref[...], preferred_element_type=F32)
    o_ref[...] = x_ref[...] + g_ref[...] * y


def _proj_residual(a, w, x, gate, seq):
    t, d = x.shape
    k = a.shape[1]
    tm = min(TOKEN_TILE, seq)
    per_seq = seq // tm
    b = gate.shape[0]
    return pl.pallas_call(
        _proj_residual_kernel,
        out_shape=jax.ShapeDtypeStruct((t, d), F32),
        grid=(t // tm,),
        in_specs=[
            pl.BlockSpec((tm, k), lambda i: (i, 0)),
            pl.BlockSpec((k, d), lambda i: (0, 0)),
            pl.BlockSpec((tm, d), lambda i: (i, 0)),
            pl.BlockSpec((None, 1, d), lambda i: (i // per_seq, 0, 0)),
        ],
        out_specs=pl.BlockSpec((tm, d), lambda i: (i, 0)),
        compiler_params=_params("parallel"),
        name="proj_residual",
    )(a, w, x, gate.reshape(b, 1, d))


def _scaled_halves(q_ref):
    q = (q_ref[...].astype(F32) * (HEAD_DIM ** -0.5 * LOG2E)).astype(BF16)
    lane = lax.broadcasted_iota(jnp.int32, q.shape, 1)
    zero = jnp.zeros_like(q)
    return jnp.concatenate([jnp.where(lane < HEAD_DIM, q, zero),
                            jnp.where(lane >= HEAD_DIM, q, zero)], axis=0)


def _dot_nt(a, b):
    return lax.dot_general(a, b, (((1,), (1,)), ((), ())), preferred_element_type=F32)


def _tile_row_col(tq, tk):
    row = lax.broadcasted_iota(jnp.int32, (2 * tq, tk), 0)
    row = jnp.where(row >= tq, row - tq, row)
    col = lax.broadcasted_iota(jnp.int32, (2 * tq, tk), 1)
    return row, col


def _lane_chunks(n):
    return [slice(c * LANES, (c + 1) * LANES) for c in range(n // LANES)]


def _diff_attn_kernel(q_ref, k_ref, v_ref, slope_ref, lam_ref, subln_ref, o_ref,
                      m_ref, l_ref, acc_ref, sa_ref, sb_ref, *, tile, lam_init):
    i = pl.program_id(2)
    qq = _scaled_halves(q_ref)
    slope = slope_ref[...]
    col_i = lax.broadcasted_iota(jnp.int32, (1, tile), 1)
    ones = jnp.ones((tile, LANES), BF16)
    m_ref[...] = jnp.full(m_ref.shape, MASK_VALUE, F32)
    l_ref[...] = jnp.zeros(l_ref.shape, F32)
    acc_ref[...] = jnp.zeros(acc_ref.shape, F32)

    def scores(s_ref, kb):
        k = k_ref[pl.ds(pl.multiple_of(kb * tile, tile), tile), :]
        s_ref[...] = _dot_nt(qq, k) + slope * (col_i + (kb - i) * tile).astype(F32)

    def accumulate(s_ref, kb, diagonal):
        v = v_ref[pl.ds(pl.multiple_of(kb * tile, tile), tile), :]
        if diagonal:
            row, col = _tile_row_col(tile, tile)
            s_ref[...] = jnp.where(col <= row, s_ref[...], MASK_VALUE)
        m_prev = m_ref[...]
        m_new = jnp.maximum(m_prev, jnp.max(s_ref[...], axis=-1, keepdims=True))
        alpha = jnp.exp2(m_prev - m_new)
        p = jnp.concatenate([jnp.exp2(s_ref[:, sl] - m_new).astype(BF16) for sl in _lane_chunks(tile)],
                            axis=1)
        pv = jnp.dot(p, jnp.concatenate([v, ones], axis=1),
                     preferred_element_type=F32)
        acc_ref[...] = alpha * acc_ref[...] + pv[:, :LANES]
        l_ref[...] = alpha * l_ref[...] + pv[:, LANES:]
        m_ref[...] = m_new

    scores(sa_ref, 0)

    def body(j, carry):
        kb = 2 * j
        scores(sb_ref, kb + 1)
        accumulate(sa_ref, kb, False)
        scores(sa_ref, kb + 2)
        accumulate(sb_ref, kb + 1, False)
        return carry

    lax.fori_loop(0, i // 2, body, 0)

    @pl.when(i % 2 == 0)
    def _():
        accumulate(sa_ref, i, True)

    @pl.when(i % 2 == 1)
    def _():
        scores(sb_ref, i)
        accumulate(sa_ref, i - 1, False)
        accumulate(sb_ref, i, True)

    o_all = acc_ref[...] / l_ref[...]
    lv = lam_ref[...]
    lam = (jnp.exp(jnp.sum(lv[0:1] * lv[1:2], axis=1, keepdims=True))
           - jnp.exp(jnp.sum(lv[2:3] * lv[3:4], axis=1, keepdims=True)) + lam_init)
    o = o_all[:tile] - lam * o_all[tile:]
    ms = jnp.mean(o * o, axis=-1, keepdims=True)
    o = o * lax.rsqrt(ms + EPS) * subln_ref[...] * (1.0 - lam_init)
    o_ref[...] = o.astype(o_ref.dtype)


def _diff_attention(qkv, lam_vecs, subln, batch, seq, layer_idx):
    t = qkv.shape[0]
    d = qkv.shape[1] // 3
    heads = d // LANES
    tile = min(ATTN_TILE, seq)
    nq = seq // tile
    lam_init = 0.8 - 0.6 * math.exp(-0.3 * layer_idx)
    slopes = np.array([2.0 ** (-8.0 * (h + 1) / heads) * LOG2E for h in range(heads)], np.float32)
    slopes = jnp.asarray(np.broadcast_to(slopes[:, None, None], (heads, 1, tile)))
    kernel = functools.partial(_diff_attn_kernel, tile=tile, lam_init=lam_init)
    return pl.pallas_call(
        kernel,
        out_shape=jax.ShapeDtypeStruct((t, d), BF16),
        grid=(batch, heads, nq),
        in_specs=[
            pl.BlockSpec((tile, LANES), lambda b, h, i: (b * nq + i, h)),
            pl.BlockSpec((seq, LANES), lambda b, h, i: (b, heads + h)),
            pl.BlockSpec((seq, LANES), lambda b, h, i: (b, 2 * heads + h)),
            pl.BlockSpec((None, 1, tile), lambda b, h, i: (h, 0, 0)),
            pl.BlockSpec(lam_vecs.shape, lambda b, h, i: (0, 0)),
            pl.BlockSpec((1, LANES), lambda b, h, i: (0, 0)),
        ],
        out_specs=pl.BlockSpec((tile, LANES), lambda b, h, i: (b * nq + i, h)),
        scratch_shapes=[
            pltpu.VMEM((2 * tile, LANES), F32),
            pltpu.VMEM((2 * tile, LANES), F32),
            pltpu.VMEM((2 * tile, LANES), F32),
            pltpu.VMEM((2 * tile, tile), F32),
            pltpu.VMEM((2 * tile, tile), F32),
        ],
        compiler_params=_params("parallel", "parallel", "arbitrary"),
        name="diff_attention",
    )(qkv, qkv, qkv, slopes, lam_vecs, subln.reshape(1, LANES))


CHUNK = 256


def _suffix_sum_matrix():
    j = np.arange(CHUNK)
    return jnp.asarray((j[:, None] >= j[None, :]).astype(np.float32), dtype=BF16)


def _stick_attn_kernel(q_ref, k_ref, v_ref, mt_ref, o_ref, rem_ref, acc_ref, za_ref, zb_ref, *, tile):
    i = pl.program_id(2)
    qq = _scaled_halves(q_ref)
    mt = mt_ref[...]
    rem_ref[...] = jnp.zeros(rem_ref.shape, F32)
    acc_ref[...] = jnp.zeros(acc_ref.shape, F32)

    def scores(z_ref, kb):
        z_ref[...] = _dot_nt(qq, k_ref[pl.ds(pl.multiple_of(kb * tile, tile), tile), :])

    def accumulate(z_ref, kb, diagonal):
        v = v_ref[pl.ds(pl.multiple_of(kb * tile, tile), tile), :]
        rem = rem_ref[...]
        chunks = []
        for c in reversed(range(tile // CHUNK)):
            z = z_ref[:, c * CHUNK:(c + 1) * CHUNK]
            softplus = jnp.maximum(z, 0.0) + jnp.log2(1.0 + jnp.exp2(-jnp.abs(z)))
            if diagonal:
                row = lax.broadcasted_iota(jnp.int32, z.shape, 0)
                row = jnp.where(row >= tile, row - tile, row)
                strict = lax.broadcasted_iota(jnp.int32, z.shape, 1) + c * CHUNK < row
                softplus = jnp.where(strict, softplus, 0.0)
            sums = jnp.dot(softplus.astype(BF16), mt, preferred_element_type=F32)
            for h in reversed(range(CHUNK // LANES)):
                sl = slice(h * LANES, (h + 1) * LANES)
                a = jnp.exp2(z_ref[:, c * CHUNK + h * LANES:c * CHUNK + (h + 1) * LANES] - sums[:, sl] - rem)
                if diagonal:
                    a = jnp.where(strict[:, sl], a, 0.0)
                chunks.append(a.astype(BF16))
            rem = rem + jnp.broadcast_to(sums[:, 0:1], rem.shape)
        rem_ref[...] = rem
        acc_ref[...] += jnp.dot(jnp.concatenate(chunks[::-1], axis=1), v, preferred_element_type=F32)

    scores(za_ref, i)

    @pl.when(i == 0)
    def _():
        accumulate(za_ref, 0, True)

    @pl.when(i > 0)
    def _():
        scores(zb_ref, i - 1)
        accumulate(za_ref, i, True)

        def body(j, carry):
            kb = i - 1 - 2 * j
            scores(za_ref, kb - 1)
            accumulate(zb_ref, kb, False)
            scores(zb_ref, kb - 2)
            accumulate(za_ref, kb - 1, False)
            return carry

        lax.fori_loop(0, (i - 1) // 2, body, 0)

        @pl.when((i - 1) % 2 == 0)
        def _():
            accumulate(zb_ref, 0, False)

        @pl.when((i - 1) % 2 == 1)
        def _():
            scores(za_ref, 0)
            accumulate(zb_ref, 1, False)
            accumulate(za_ref, 0, False)

    acc = acc_ref[...]
    lane = lax.broadcasted_iota(jnp.int32, (tile, LANES), 1)
    o_ref[...] = jnp.where(lane < HEAD_DIM, acc[:tile], acc[tile:]).astype(o_ref.dtype)


def _stick_attention(q, kv, batch, seq):
    t, d = q.shape
    pairs = d // LANES
    tile = min(ATTN_TILE, seq)
    nq = seq // tile
    kernel = functools.partial(_stick_attn_kernel, tile=tile)
    return pl.pallas_call(
        kernel,
        out_shape=jax.ShapeDtypeStruct((t, d), BF16),
        grid=(batch, pairs, nq),
        in_specs=[
            pl.BlockSpec((tile, LANES), lambda b, j, i: (b * nq + i, j)),
            pl.BlockSpec((seq, LANES), lambda b, j, i: (b, j)),
            pl.BlockSpec((seq, LANES), lambda b, j, i: (b, pairs + j)),
            pl.BlockSpec((CHUNK, CHUNK), lambda b, j, i: (0, 0)),
        ],
        out_specs=pl.BlockSpec((tile, LANES), lambda b, j, i: (b * nq + i, j)),
        scratch_shapes=[
            pltpu.VMEM((2 * tile, LANES), F32),
            pltpu.VMEM((2 * tile, LANES), F32),
            pltpu.VMEM((2 * tile, tile), F32),
            pltpu.VMEM((2 * tile, tile), F32),
        ],
        compiler_params=_params("parallel", "parallel", "arbitrary"),
        name="stick_attention",
    )(q, kv, kv, _suffix_sum_matrix())


SC_WINDOW = 128


def _sc_gather_rows(tab0, tab1, idx):
    m = idx.shape[0]
    w = tab0.shape[1]
    mesh = plsc.VectorSubcoreMesh(core_axis_name="core", subcore_axis_name="subcore")
    assert m % (SC_WINDOW * mesh.num_cores * mesh.num_subcores) == 0, m
    out = jax.ShapeDtypeStruct((m, w), tab0.dtype)

    @pl.kernel(out_type=(out, out), mesh=mesh)
    def gather(t0_hbm, t1_hbm, i_hbm, o0_hbm, o1_hbm):
        for t_hbm, o_hbm in ((t0_hbm, o0_hbm), (t1_hbm, o1_hbm)):
            def body(i_vmem, o_vmem, t_hbm=t_hbm):
                pltpu.sync_copy(t_hbm.at[i_vmem.at[0]], o_vmem)

            pltpu.emit_pipeline(
                body,
                grid=(m // SC_WINDOW,),
                in_specs=[pl.BlockSpec((1, SC_WINDOW), lambda i: (0, i))],
                out_specs=[pl.BlockSpec((SC_WINDOW, w), lambda i: (i, 0))],
                core_axis_name=("core", "subcore"),
                dimension_semantics=(pltpu.PARALLEL,),
            )(i_hbm, o_hbm)

    return gather(tab0, tab1, idx.reshape(1, m))


def _moe_route_kernel(x_ref, g_ref, sh_ref, sc_ref, rw_ref, rb_ref, h0_ref, h1_ref, idx_ref, gate_ref):
    h = _adaln(x_ref[...], g_ref[...], sh_ref[...], sc_ref[...])
    h0_ref[...], h1_ref[...] = _pack_rows(h)
    logits = jnp.dot(h, rw_ref[...], precision=lax.Precision.HIGHEST,
                     preferred_element_type=F32) + rb_ref[...]
    lane = lax.broadcasted_iota(jnp.int32, logits.shape, 1)
    lane_f = lane.astype(F32)
    vals, ids = [], []
    for _ in range(TOP_K):
        m = jnp.max(logits, axis=-1, keepdims=True)
        first = jnp.min(jnp.where(logits == m, lane_f, float(LANES)), axis=-1, keepdims=True)
        vals.append(m)
        ids.append(first)
        logits = jnp.where(lane_f == first, -jnp.inf, logits)
    exps = [jnp.exp(v - vals[0]) for v in vals]
    denom = exps[0]
    for e in exps[1:]:
        denom = denom + e
    gate = jnp.zeros(logits.shape, F32)
    idx = jnp.zeros(logits.shape, F32)
    for k in range(TOP_K):
        gate = jnp.where(lane == k, exps[k] / denom, gate)
        idx = jnp.where(lane == k, ids[k], idx)
    gate_ref[...] = gate
    idx_ref[...] = idx.T[:SUBLANES].astype(jnp.int32)


def _moe_route(x, gain, shift, scale, rw_pad, rb_pad, seq):
    t, d = x.shape
    tm = min(TOKEN_TILE, seq)
    per_seq = seq // tm
    b = shift.shape[0]
    return pl.pallas_call(
        _moe_route_kernel,
        out_shape=(jax.ShapeDtypeStruct((t, d // 4), jnp.uint32),
                   jax.ShapeDtypeStruct((t, d // 4), jnp.uint32),
                   jax.ShapeDtypeStruct((SUBLANES, t), jnp.int32),
                   jax.ShapeDtypeStruct((t, LANES), F32)),
        grid=(t // tm,),
        in_specs=[
            pl.BlockSpec((tm, d), lambda i: (i, 0)),
            pl.BlockSpec((1, d), lambda i: (0, 0)),
            pl.BlockSpec((None, 1, d), lambda i: (i // per_seq, 0, 0)),
            pl.BlockSpec((None, 1, d), lambda i: (i // per_seq, 0, 0)),
            pl.BlockSpec((d, LANES), lambda i: (0, 0)),
            pl.BlockSpec((1, LANES), lambda i: (0, 0)),
        ],
        out_specs=(pl.BlockSpec((tm, d // 4), lambda i: (i, 0)),
                   pl.BlockSpec((tm, d // 4), lambda i: (i, 0)),
                   pl.BlockSpec((SUBLANES, tm), lambda i: (0, i)),
                   pl.BlockSpec((tm, LANES), lambda i: (i, 0))),
        compiler_params=_params("parallel"),
        name="moe_route",
    )(x, gain.reshape(1, d), shift.reshape(b, 1, d), scale.reshape(b, 1, d), rw_pad, rb_pad)


def _expert_kernel(block_e_ref, n_used_ref, h0_ref, h1_ref, wgu_ref, bgu_ref, wd_ref, bd_ref,
                   o0_ref, o1_ref):
    i = pl.program_id(0)

    @pl.when(i < n_used_ref[0])
    def _():
        d_e = wd_ref.shape[0]
        h = jnp.concatenate([q.astype(BF16) for q in _unpack_rows(h0_ref[...], h1_ref[...])], axis=1)
        gu = jnp.dot(h, wgu_ref[...].astype(BF16), preferred_element_type=F32) + bgu_ref[...]
        g = jnp.minimum(gu[:, :d_e], SWIGLU_LIMIT)
        u = jnp.clip(gu[:, d_e:], -SWIGLU_LIMIT, SWIGLU_LIMIT)
        act = g / (1.0 + jnp.exp(-SWIGLU_ALPHA * g)) * (u + 1.0)
        y = jnp.dot(act.astype(BF16), wd_ref[...].astype(BF16), preferred_element_type=F32) + bd_ref[...]
        o0_ref[...], o1_ref[...] = _pack_rows(y)

    @pl.when(i >= n_used_ref[0])
    def _():
        o0_ref[...] = jnp.zeros(o0_ref.shape, o0_ref.dtype)
        o1_ref[...] = jnp.zeros(o1_ref.shape, o1_ref.dtype)


def _expert_mlp(block_e, n_used, rows0, rows1, wgu, bgu, wd, bd, layer, tile):
    n_rows, dq = rows0.shape
    d = 4 * dq
    n_l, n_e, _, two_de = wgu.shape
    d_e = two_de // 2
    n_blocks = n_rows // tile
    grid_spec = pltpu.PrefetchScalarGridSpec(
        num_scalar_prefetch=2,
        grid=(n_blocks,),
        in_specs=[
            pl.BlockSpec((tile, dq), lambda i, be, nu: (i, 0)),
            pl.BlockSpec((tile, dq), lambda i, be, nu: (i, 0)),
            pl.BlockSpec((None, None, d, two_de), lambda i, be, nu: (layer, be[i], 0, 0)),
            pl.BlockSpec((None, None, 1, two_de), lambda i, be, nu: (layer, be[i], 0, 0)),
            pl.BlockSpec((None, None, d_e, d), lambda i, be, nu: (layer, be[i], 0, 0)),
            pl.BlockSpec((None, None, 1, d), lambda i, be, nu: (layer, be[i], 0, 0)),
        ],
        out_specs=(pl.BlockSpec((tile, dq), lambda i, be, nu: (i, 0)),
                   pl.BlockSpec((tile, dq), lambda i, be, nu: (i, 0))),
    )
    packed = jax.ShapeDtypeStruct((n_rows, dq), jnp.uint32)
    return pl.pallas_call(
        _expert_kernel,
        out_shape=(packed, packed),
        grid_spec=grid_spec,
        compiler_params=_params("arbitrary"),
        name="expert_mlp",
    )(block_e, n_used, rows0, rows1, wgu, bgu.reshape(n_l, n_e, 1, two_de), wd,
      bd.reshape(n_l, n_e, 1, d))


def _combine_kernel(y0_ref, y1_ref, w_ref, x_ref, g_ref, o_ref):
    w = w_ref[...]
    acc = None
    for k in range(y0_ref.shape[0]):
        quarters = _unpack_rows(y0_ref[k], y1_ref[k])
        scaled = [w[:, k:k + 1] * q for q in quarters]
        acc = scaled if acc is None else [a + s for a, s in zip(acc, scaled)]
    o_ref[...] = x_ref[...] + g_ref[...] * jnp.concatenate(acc, axis=1)


def _combine(y0, y1, weights, x, gate, seq):
    t, d = x.shape
    tm = min(TOKEN_TILE, seq)
    per_seq = seq // tm
    b = gate.shape[0]
    return pl.pallas_call(
        _combine_kernel,
        out_shape=jax.ShapeDtypeStruct((t, d), F32),
        grid=(t // tm,),
        in_specs=[
            pl.BlockSpec((y0.shape[0], tm, d // 4), lambda i: (0, i, 0)),
            pl.BlockSpec((y0.shape[0], tm, d // 4), lambda i: (0, i, 0)),
            pl.BlockSpec((tm, LANES), lambda i: (i, 0)),
            pl.BlockSpec((tm, d), lambda i: (i, 0)),
            pl.BlockSpec((None, 1, d), lambda i: (i // per_seq, 0, 0)),
        ],
        out_specs=pl.BlockSpec((tm, d), lambda i: (i, 0)),
        compiler_params=_params("parallel"),
        name="moe_combine",
    )(y0, y1, weights, x, gate.reshape(b, 1, d))


ROW_BLOCK_MULTIPLE = 8


def _routing_tables(idx_t, n_experts, tile):
    top_k, t = idx_t.shape
    n_assign = top_k * t
    e_flat = idx_t.reshape(n_assign)
    experts = jnp.arange(n_experts, dtype=jnp.int32)
    onehot = e_flat[:, None] == experts[None, :]
    counts = jnp.sum(onehot, axis=0, dtype=jnp.int32)
    padded = (counts + tile - 1) // tile * tile
    start = jnp.cumsum(counts) - counts
    pend = jnp.cumsum(padded)
    pstart = pend - padded
    order = jnp.argsort(e_flat, stable=True).astype(jnp.int32)
    sorted_pos = jnp.argsort(order).astype(jnp.int32)
    shift = jnp.sum(jnp.where(onehot, (pstart - start)[None, :], 0), axis=1, dtype=jnp.int32)
    slot_row = sorted_pos + shift
    n_blocks = (n_assign + n_experts * (tile - 1) + tile - 1) // tile
    n_blocks = -(-n_blocks // ROW_BLOCK_MULTIPLE) * ROW_BLOCK_MULTIPLE
    first_row = jnp.arange(n_blocks, dtype=jnp.int32) * tile
    block_e = jnp.minimum(jnp.sum(pend[None, :] <= first_row[:, None], axis=1), n_experts - 1)
    block_e = block_e.astype(jnp.int32)
    n_used = (pend[-1] // tile).astype(jnp.int32).reshape(1)
    within = (first_row - pstart[block_e])[:, None] + jnp.arange(tile, dtype=jnp.int32)[None, :]
    valid = within < counts[block_e][:, None]
    src = jnp.where(valid, start[block_e][:, None] + within, 0).reshape(-1)
    row_tok = jnp.where(valid.reshape(-1), order[src], jnp.arange(n_blocks * tile, dtype=jnp.int32)) % t
    return row_tok, block_e, n_used, slot_row


def _moe(x, gain, shift, scale, gate, rw, rb, wgu, bgu, wd, bd, layer, seq):
    t, d = x.shape
    n_e = rw.shape[1]
    rw_pad = jnp.pad(rw, ((0, 0), (0, LANES - n_e)))
    rb_pad = jnp.pad(rb, (0, LANES - n_e), constant_values=MASK_VALUE).reshape(1, LANES)
    h0, h1, idx_t, weights = _moe_route(x, gain, shift, scale, rw_pad, rb_pad, seq)
    tile = min(EXPERT_TILE, t * TOP_K // n_e)
    row_tok, block_e, n_used, slot_row = _routing_tables(idx_t[:TOP_K], n_e, tile)
    rows0, rows1 = _sc_gather_rows(h0, h1, row_tok)
    ys0, ys1 = _expert_mlp(block_e, n_used, rows0, rows1, wgu, bgu, wd, bd, layer, tile)
    y0, y1 = _sc_gather_rows(ys0, ys1, slot_row)
    return _combine(y0.reshape(TOP_K, t, d // 4), y1.reshape(TOP_K, t, d // 4), weights, x, gate, seq)


def _rmsnorm_kernel(x_ref, g_ref, o_ref):
    x = x_ref[...]
    ms = jnp.mean(x * x, axis=-1, keepdims=True)
    o_ref[...] = x * lax.rsqrt(ms + EPS) * g_ref[...]


def _rmsnorm(x, gain):
    t, d = x.shape
    tm = min(TOKEN_TILE, t)
    return pl.pallas_call(
        _rmsnorm_kernel,
        out_shape=jax.ShapeDtypeStruct((t, d), F32),
        grid=(t // tm,),
        in_specs=[pl.BlockSpec((tm, d), lambda i: (i, 0)),
                  pl.BlockSpec((1, d), lambda i: (0, 0))],
        out_specs=pl.BlockSpec((tm, d), lambda i: (i, 0)),
        compiler_params=_params("parallel"),
        name="final_rmsnorm",
    )(x, gain.reshape(1, d))


def kernel(x, c, ada_w, ada_b, norm_mix, norm_moe, a_wqkv, a_wo, a_lambda, a_subln, kv_norm,
           kv_ada_w, kv_ada_b, kv_w, b_wq, b_wo, router_w, router_b, w_gate_up, b_gate_up,
           w_down, b_down, final_norm):
    b, s, d = x.shape
    depth = ada_w.shape[0]
    n_a = a_wqkv.shape[0]
    xt = x.reshape(b * s, d)

    c_pad = jnp.pad(c, ((0, SUBLANES - b), (0, 0)))
    mod = _modulation(c_pad, ada_w, ada_b)[:, :b]
    kv_mod = _modulation(c_pad, kv_ada_w[None], kv_ada_b[None])[0, :b]

    for l in range(depth):
        sh1, sc1, g1, sh2, sc2, g2 = jnp.split(mod[l], 6, axis=-1)
        if l < n_a:
            qkv = _norm_linear(xt, norm_mix[l], sh1, sc1, a_wqkv[l].astype(BF16), s)
            o = _diff_attention(qkv, a_lambda[l], a_subln[l], b, s, l)
            xt = _proj_residual(o, a_wo[l].astype(BF16), xt, g1, s)
        else:
            if l == n_a:
                kv_sh, kv_sc = jnp.split(kv_mod, 2, axis=-1)
                kv = _norm_linear(xt, kv_norm, kv_sh, kv_sc, kv_w.astype(BF16), s)
            j = l - n_a
            q = _norm_linear(xt, norm_mix[l], sh1, sc1, b_wq[j].astype(BF16), s)
            o = _stick_attention(q, kv, b, s)
            xt = _proj_residual(o, b_wo[j].astype(BF16), xt, g1, s)
        xt = _moe(xt, norm_moe[l], sh2, sc2, g2, router_w[l], router_b[l],
                  w_gate_up, b_gate_up, w_down, b_down, l, s)
    return _rmsnorm(xt, final_norm).reshape(b, s, d)
```

```python
import functools
import math

import jax
import jax.numpy as jnp
import numpy as np
from jax import lax
from jax.experimental import pallas as pl
from jax.experimental.pallas import tpu as pltpu
from jax.experimental.pallas import tpu_sc as plsc

F32 = jnp.float32
BF16 = jnp.bfloat16
EPS = 1e-6
MASK_VALUE = -1e30
TOP_K = 4
SWIGLU_ALPHA = 1.702
SWIGLU_LIMIT = 7.0
LOG2E = math.log2(math.e)

LANES = 128
SUBLANES = 8
HEAD_DIM = 64
V7X_VMEM_BYTES = 64 * 2 ** 20
VMEM_LIMIT = V7X_VMEM_BYTES * 7 // 8

TOKEN_TILE = 512
ATTN_TILE = 512
EXPERT_TILE = 512


def _params(*semantics):
    return pltpu.CompilerParams(dimension_semantics=semantics, vmem_limit_bytes=VMEM_LIMIT)


def _largest_tile(n, cap):
    t = min(n, cap) // LANES * LANES
    while n % t:
        t -= LANES
    return t


def _adaln(x, gain, shift, scale):
    ms = jnp.mean(x * x, axis=-1, keepdims=True)
    return x * lax.rsqrt(ms + EPS) * gain * (1.0 + scale) + shift


def _pack_bf16_pair(lo, hi):
    lo_bits = lax.bitcast_convert_type(lo.astype(BF16).astype(F32), jnp.uint32) >> 16
    hi_bits = lax.bitcast_convert_type(hi.astype(BF16).astype(F32), jnp.uint32) & jnp.uint32(0xFFFF0000)
    return lo_bits | hi_bits


def _unpack_bf16_pair(packed):
    lo = lax.bitcast_convert_type(packed << 16, F32)
    hi = lax.bitcast_convert_type(packed & jnp.uint32(0xFFFF0000), F32)
    return lo, hi


def _pack_rows(x):
    q = x.shape[1] // 4
    return (_pack_bf16_pair(x[:, :q], x[:, 2 * q:3 * q]),
            _pack_bf16_pair(x[:, q:2 * q], x[:, 3 * q:]))


def _unpack_rows(p0, p1):
    a0, a2 = _unpack_bf16_pair(p0)
    a1, a3 = _unpack_bf16_pair(p1)
    return a0, a1, a2, a3


def _mod_kernel(c_ref, w_ref, b_ref, o_ref):
    c = c_ref[...]
    a = c / (1.0 + jnp.exp(-c))
    o_ref[...] = jnp.dot(a, w_ref[...], precision=lax.Precision.HIGHEST,
                         preferred_element_type=F32) + b_ref[...]


def _modulation(c_pad, w, b):
    n_l, d, n = w.shape
    tn = _largest_tile(n, 2048)
    return pl.pallas_call(
        _mod_kernel,
        out_shape=jax.ShapeDtypeStruct((n_l, SUBLANES, n), F32),
        grid=(n_l, n // tn),
        in_specs=[
            pl.BlockSpec((SUBLANES, d), lambda l, j: (0, 0)),
            pl.BlockSpec((None, d, tn), lambda l, j: (l, 0, j)),
            pl.BlockSpec((None, 1, tn), lambda l, j: (l, 0, j)),
        ],
        out_specs=pl.BlockSpec((None, SUBLANES, tn), lambda l, j: (l, 0, j)),
        compiler_params=_params("parallel", "parallel"),
        name="modulation",
    )(c_pad, w, b.reshape(n_l, 1, n))


def _norm_linear_kernel(x_ref, g_ref, sh_ref, sc_ref, w_ref, o_ref):
    h = _adaln(x_ref[...], g_ref[...], sh_ref[...], sc_ref[...])
    o_ref[...] = jnp.dot(h.astype(BF16), w_ref[...], preferred_element_type=F32).astype(o_ref.dtype)


def _norm_linear(x, gain, shift, scale, w, seq):
    t, d = x.shape
    n = w.shape[1]
    tm = min(TOKEN_TILE, seq)
    per_seq = seq // tm
    b = shift.shape[0]
    return pl.pallas_call(
        _norm_linear_kernel,
        out_shape=jax.ShapeDtypeStruct((t, n), BF16),
        grid=(t // tm,),
        in_specs=[
            pl.BlockSpec((tm, d), lambda i: (i, 0)),
            pl.BlockSpec((1, d), lambda i: (0, 0)),
            pl.BlockSpec((None, 1, d), lambda i: (i // per_seq, 0, 0)),
            pl.BlockSpec((None, 1, d), lambda i: (i // per_seq, 0, 0)),
            pl.BlockSpec((d, n), lambda i: (0, 0)),
        ],
        out_specs=pl.BlockSpec((tm, n), lambda i: (i, 0)),
        compiler_params=_params("parallel"),
        name="norm_linear",
    )(x, gain.reshape(1, d), shift.reshape(b, 1, d), scale.reshape(b, 1, d), w)


def _proj_residual_kernel(a_ref, w_ref, x_ref, g_ref, o_ref):
    y = jnp.dot(a_ref[...], w_ref[...], preferred_element_type=F32)
    o_ref[...] = x_ref[...] + g_ref[...] * y


def _proj_residual(a, w, x, gate, seq):
    t, d = x.shape
    k = a.shape[1]
    tm = min(TOKEN_TILE, seq)
    per_seq = seq // tm
    b = gate.shape[0]
    return pl.pallas_call(
        _proj_residual_kernel,
        out_shape=jax.ShapeDtypeStruct((t, d), F32),
        grid=(t // tm,),
        in_specs=[
            pl.BlockSpec((tm, k), lambda i: (i, 0)),
            pl.BlockSpec((k, d), lambda i: (0, 0)),
            pl.BlockSpec((tm, d), lambda i: (i, 0)),
            pl.BlockSpec((None, 1, d), lambda i: (i // per_seq, 0, 0)),
        ],
        out_specs=pl.BlockSpec((tm, d), lambda i: (i, 0)),
        compiler_params=_params("parallel"),
        name="proj_residual",
    )(a, w, x, gate.reshape(b, 1, d))


def _scaled_halves(q_ref):
    q = (q_ref[...].astype(F32) * (HEAD_DIM ** -0.5 * LOG2E)).astype(BF16)
    lane = lax.broadcasted_iota(jnp.int32, q.shape, 1)
    zero = jnp.zeros_like(q)
    return jnp.concatenate([jnp.where(lane < HEAD_DIM, q, zero),
                            jnp.where(lane >= HEAD_DIM, q, zero)], axis=0)


def _dot_nt(a, b):
    return lax.dot_general(a, b, (((1,), (1,)), ((), ())), preferred_element_type=F32)


def _tile_row_col(tq, tk):
    row = lax.broadcasted_iota(jnp.int32, (2 * tq, tk), 0)
    row = jnp.where(row >= tq, row - tq, row)
    col = lax.broadcasted_iota(jnp.int32, (2 * tq, tk), 1)
    return row, col


def _lane_chunks(n):
    return [slice(c * LANES, (c + 1) * LANES) for c in range(n // LANES)]


def _diff_attn_kernel(q_ref, k_ref, v_ref, slope_ref, lam_ref, subln_ref, o_ref,
                      m_ref, l_ref, acc_ref, sa_ref, sb_ref, *, tile, lam_init):
    i = pl.program_id(2)
    qq = _scaled_halves(q_ref)
    slope = slope_ref[...]
    col_i = lax.broadcasted_iota(jnp.int32, (1, tile), 1)
    ones = jnp.ones((tile, LANES), BF16)
    m_ref[...] = jnp.full(m_ref.shape, MASK_VALUE, F32)
    l_ref[...] = jnp.zeros(l_ref.shape, F32)
    acc_ref[...] = jnp.zeros(acc_ref.shape, F32)

    def scores(s_ref, kb):
        k = k_ref[pl.ds(pl.multiple_of(kb * tile, tile), tile), :]
        s_ref[...] = _dot_nt(qq, k) + slope * (col_i + (kb - i) * tile).astype(F32)

    def accumulate(s_ref, kb, diagonal):
        v = v_ref[pl.ds(pl.multiple_of(kb * tile, tile), tile), :]
        if diagonal:
            row, col = _tile_row_col(tile, tile)
            s_ref[...] = jnp.where(col <= row, s_ref[...], MASK_VALUE)
        m_prev = m_ref[...]
        m_new = jnp.maximum(m_prev, jnp.max(s_ref[...], axis=-1, keepdims=True))
        alpha = jnp.exp2(m_prev - m_new)
        p = jnp.concatenate([jnp.exp2(s_ref[:, sl] - m_new).astype(BF16) for sl in _lane_chunks(tile)],
                            axis=1)
        pv = jnp.dot(p, jnp.concatenate([v, ones], axis=1),
                     preferred_element_type=F32)
        acc_ref[...] = alpha * acc_ref[...] + pv[:, :LANES]
        l_ref[...] = alpha * l_ref[...] + pv[:, LANES:]
        m_ref[...] = m_new

    scores(sa_ref, 0)

    def body(j, carry):
        kb = 2 * j
        scores(sb_ref, kb + 1)
        accumulate(sa_ref, kb, False)
        scores(sa_ref, kb + 2)
        accumulate(sb_ref, kb + 1, False)
        return carry

    lax.fori_loop(0, i // 2, body, 0)

    @pl.when(i % 2 == 0)
    def _():
        accumulate(sa_ref, i, True)

    @pl.when(i % 2 == 1)
    def _():
        scores(sb_ref, i)
        accumulate(sa_ref, i - 1, False)
        accumulate(sb_ref, i, True)

    o_all = acc_ref[...] / l_ref[...]
    lv = lam_ref[...]
    lam = (jnp.exp(jnp.sum(lv[0:1] * lv[1:2], axis=1, keepdims=True))
           - jnp.exp(jnp.sum(lv[2:3] * lv[3:4], axis=1, keepdims=True)) + lam_init)
    o = o_all[:tile] - lam * o_all[tile:]
    ms = jnp.mean(o * o, axis=-1, keepdims=True)
    o = o * lax.rsqrt(ms + EPS) * subln_ref[...] * (1.0 - lam_init)
    o_ref[...] = o.astype(o_ref.dtype)


def _diff_attention(qkv, lam_vecs, subln, batch, seq, layer_idx):
    t = qkv.shape[0]
    d = qkv.shape[1] // 3
    heads = d // LANES
    tile = min(ATTN_TILE, seq)
    nq = seq // tile
    lam_init = 0.8 - 0.6 * math.exp(-0.3 * layer_idx)
    slopes = np.array([2.0 ** (-8.0 * (h + 1) / heads) * LOG2E for h in range(heads)], np.float32)
    slopes = jnp.asarray(np.broadcast_to(slopes[:, None, None], (heads, 1, tile)))
    kernel = functools.partial(_diff_attn_kernel, tile=tile, lam_init=lam_init)
    return pl.pallas_call(
        kernel,
        out_shape=jax.ShapeDtypeStruct((t, d), BF16),
        grid=(batch, heads, nq),
        in_specs=[
            pl.BlockSpec((tile, LANES), lambda b, h, i: (b * nq + i, h)),
            pl.BlockSpec((seq, LANES), lambda b, h, i: (b, heads + h)),
            pl.BlockSpec((seq, LANES), lambda b, h, i: (b, 2 * heads + h)),
            pl.BlockSpec((None, 1, tile), lambda b, h, i: (h, 0, 0)),
            pl.BlockSpec(lam_vecs.shape, lambda b, h, i: (0, 0)),
            pl.BlockSpec((1, LANES), lambda b, h, i: (0, 0)),
        ],
        out_specs=pl.BlockSpec((tile, LANES), lambda b, h, i: (b * nq + i, h)),
        scratch_shapes=[
            pltpu.VMEM((2 * tile, LANES), F32),
            pltpu.VMEM((2 * tile, LANES), F32),
            pltpu.VMEM((2 * tile, LANES), F32),
            pltpu.VMEM((2 * tile, tile), F32),
            pltpu.VMEM((2 * tile, tile), F32),
        ],
        compiler_params=_params("parallel", "parallel", "arbitrary"),
        name="diff_attention",
    )(qkv, qkv, qkv, slopes, lam_vecs, subln.reshape(1, LANES))


CHUNK = 256
UNDERFLOW_BITS = 160.0


def _suffix_sum_matrix():
    j = np.arange(CHUNK)
    return jnp.asarray((j[:, None] >= j[None, :]).astype(np.float32), dtype=BF16)


def _stick_attn_kernel(q_ref, k_ref, v_ref, mt_ref, o_ref, rem_ref, acc_ref, za_ref, zb_ref, *, tile):
    i = pl.program_id(2)
    qq = _scaled_halves(q_ref)
    mt = mt_ref[...]
    rem_ref[...] = jnp.zeros(rem_ref.shape, F32)
    acc_ref[...] = jnp.zeros(acc_ref.shape, F32)

    def scores(z_ref, kb):
        z_ref[...] = _dot_nt(qq, k_ref[pl.ds(pl.multiple_of(kb * tile, tile), tile), :])

    def accumulate(z_ref, kb, diagonal):
        v = v_ref[pl.ds(pl.multiple_of(kb * tile, tile), tile), :]
        rem = rem_ref[...]
        chunks = []
        for c in reversed(range(tile // CHUNK)):
            z = z_ref[:, c * CHUNK:(c + 1) * CHUNK]
            softplus = jnp.maximum(z, 0.0) + jnp.log2(1.0 + jnp.exp2(-jnp.abs(z)))
            if diagonal:
                row = lax.broadcasted_iota(jnp.int32, z.shape, 0)
                row = jnp.where(row >= tile, row - tile, row)
                strict = lax.broadcasted_iota(jnp.int32, z.shape, 1) + c * CHUNK < row
                softplus = jnp.where(strict, softplus, 0.0)
            sums = jnp.dot(softplus.astype(BF16), mt, preferred_element_type=F32)
            for h in reversed(range(CHUNK // LANES)):
                sl = slice(h * LANES, (h + 1) * LANES)
                a = jnp.exp2(z_ref[:, c * CHUNK + h * LANES:c * CHUNK + (h + 1) * LANES] - sums[:, sl] - rem)
                if diagonal:
                    a = jnp.where(strict[:, sl], a, 0.0)
                chunks.append(a.astype(BF16))
            rem = rem + jnp.broadcast_to(sums[:, 0:1], rem.shape)
        rem_ref[...] = rem
        acc_ref[...] += jnp.dot(jnp.concatenate(chunks[::-1], axis=1), v, preferred_element_type=F32)

    def live():
        return jnp.min(rem_ref[...]) < UNDERFLOW_BITS

    scores(za_ref, i)

    @pl.when(i == 0)
    def _():
        accumulate(za_ref, 0, True)

    @pl.when(i == 1)
    def _():
        scores(zb_ref, 0)
        accumulate(za_ref, 1, True)
        accumulate(zb_ref, 0, False)

    @pl.when(i >= 2)
    def _():
        scores(zb_ref, i - 1)
        accumulate(za_ref, i, True)
        scores(za_ref, i - 2)
        accumulate(zb_ref, i - 1, False)
        n_pairs = (i - 2) // 2

        def cond(carry):
            j, go = carry
            return jnp.logical_and(j < n_pairs, go)

        def body(carry):
            j, _ = carry
            kb = i - 2 - 2 * j
            scores(zb_ref, kb - 1)
            accumulate(za_ref, kb, False)
            scores(za_ref, kb - 2)
            accumulate(zb_ref, kb - 1, False)
            return j + 1, live()

        _, go = lax.while_loop(cond, body, (jnp.int32(0), live()))

        @pl.when(jnp.logical_and(go, i % 2 == 0))
        def _():
            accumulate(za_ref, 0, False)

        @pl.when(jnp.logical_and(go, i % 2 == 1))
        def _():
            scores(zb_ref, 0)
            accumulate(za_ref, 1, False)
            accumulate(zb_ref, 0, False)

    acc = acc_ref[...]
    lane = lax.broadcasted_iota(jnp.int32, (tile, LANES), 1)
    o_ref[...] = jnp.where(lane < HEAD_DIM, acc[:tile], acc[tile:]).astype(o_ref.dtype)


def _stick_attention(q, kv, batch, seq):
    t, d = q.shape
    pairs = d // LANES
    tile = min(ATTN_TILE, seq)
    nq = seq // tile
    kernel = functools.partial(_stick_attn_kernel, tile=tile)
    return pl.pallas_call(
        kernel,
        out_shape=jax.ShapeDtypeStruct((t, d), BF16),
        grid=(batch, pairs, nq),
        in_specs=[
            pl.BlockSpec((tile, LANES), lambda b, j, i: (b * nq + i, j)),
            pl.BlockSpec((seq, LANES), lambda b, j, i: (b, j)),
            pl.BlockSpec((seq, LANES), lambda b, j, i: (b, pairs + j)),
            pl.BlockSpec((CHUNK, CHUNK), lambda b, j, i: (0, 0)),
        ],
        out_specs=pl.BlockSpec((tile, LANES), lambda b, j, i: (b * nq + i, j)),
        scratch_shapes=[
            pltpu.VMEM((2 * tile, LANES), F32),
            pltpu.VMEM((2 * tile, LANES), F32),
            pltpu.VMEM((2 * tile, tile), F32),
            pltpu.VMEM((2 * tile, tile), F32),
        ],
        compiler_params=_params("parallel", "parallel", "arbitrary"),
        name="stick_attention",
    )(q, kv, kv, _suffix_sum_matrix())


SC_WINDOW = 128


def _sc_gather_rows(tab0, tab1, idx):
    m = idx.shape[0]
    w = tab0.shape[1]
    mesh = plsc.VectorSubcoreMesh(core_axis_name="core", subcore_axis_name="subcore")
    assert m % (SC_WINDOW * mesh.num_cores * mesh.num_subcores) == 0, m
    out = jax.ShapeDtypeStruct((m, w), tab0.dtype)

    @pl.kernel(out_type=(out, out), mesh=mesh)
    def gather(t0_hbm, t1_hbm, i_hbm, o0_hbm, o1_hbm):
        for t_hbm, o_hbm in ((t0_hbm, o0_hbm), (t1_hbm, o1_hbm)):
            def body(i_vmem, o_vmem, t_hbm=t_hbm):
                pltpu.sync_copy(t_hbm.at[i_vmem.at[0]], o_vmem)

            pltpu.emit_pipeline(
                body,
                grid=(m // SC_WINDOW,),
                in_specs=[pl.BlockSpec((1, SC_WINDOW), lambda i: (0, i))],
                out_specs=[pl.BlockSpec((SC_WINDOW, w), lambda i: (i, 0))],
                core_axis_name=("core", "subcore"),
                dimension_semantics=(pltpu.PARALLEL,),
            )(i_hbm, o_hbm)

    return gather(tab0, tab1, idx.reshape(1, m))


def _moe_route_kernel(x_ref, g_ref, sh_ref, sc_ref, rw_ref, rb_ref, h0_ref, h1_ref, idx_ref, gate_ref):
    h = _adaln(x_ref[...], g_ref[...], sh_ref[...], sc_ref[...])
    h0_ref[...], h1_ref[...] = _pack_rows(h)
    logits = jnp.dot(h, rw_ref[...], precision=lax.Precision.HIGHEST,
                     preferred_element_type=F32) + rb_ref[...]
    lane = lax.broadcasted_iota(jnp.int32, logits.shape, 1)
    lane_f = lane.astype(F32)
    vals, ids = [], []
    for _ in range(TOP_K):
        m = jnp.max(logits, axis=-1, keepdims=True)
        first = jnp.min(jnp.where(logits == m, lane_f, float(LANES)), axis=-1, keepdims=True)
        vals.append(m)
        ids.append(first)
        logits = jnp.where(lane_f == first, -jnp.inf, logits)
    exps = [jnp.exp(v - vals[0]) for v in vals]
    denom = exps[0]
    for e in exps[1:]:
        denom = denom + e
    gate = jnp.zeros(logits.shape, F32)
    idx = jnp.zeros(logits.shape, F32)
    for k in range(TOP_K):
        gate = jnp.where(lane == k, exps[k] / denom, gate)
        idx = jnp.where(lane == k, ids[k], idx)
    gate_ref[...] = gate
    idx_ref[...] = idx.T[:SUBLANES].astype(jnp.int32)


def _moe_route(x, gain, shift, scale, rw_pad, rb_pad, seq):
    t, d = x.shape
    tm = min(TOKEN_TILE, seq)
    per_seq = seq // tm
    b = shift.shape[0]
    return pl.pallas_call(
        _moe_route_kernel,
        out_shape=(jax.ShapeDtypeStruct((t, d // 4), jnp.uint32),
                   jax.ShapeDtypeStruct((t, d // 4), jnp.uint32),
                   jax.ShapeDtypeStruct((SUBLANES, t), jnp.int32),
                   jax.ShapeDtypeStruct((t, LANES), F32)),
        grid=(t // tm,),
        in_specs=[
            pl.BlockSpec((tm, d), lambda i: (i, 0)),
            pl.BlockSpec((1, d), lambda i: (0, 0)),
            pl.BlockSpec((None, 1, d), lambda i: (i // per_seq, 0, 0)),
            pl.BlockSpec((None, 1, d), lambda i: (i // per_seq, 0, 0)),
            pl.BlockSpec((d, LANES), lambda i: (0, 0)),
            pl.BlockSpec((1, LANES), lambda i: (0, 0)),
        ],
        out_specs=(pl.BlockSpec((tm, d // 4), lambda i: (i, 0)),
                   pl.BlockSpec((tm, d // 4), lambda i: (i, 0)),
                   pl.BlockSpec((SUBLANES, tm), lambda i: (0, i)),
                   pl.BlockSpec((tm, LANES), lambda i: (i, 0))),
        compiler_params=_params("parallel"),
        name="moe_route",
    )(x, gain.reshape(1, d), shift.reshape(b, 1, d), scale.reshape(b, 1, d), rw_pad, rb_pad)


def _expert_kernel(block_e_ref, n_used_ref, h0_ref, h1_ref, wgu_ref, bgu_ref, wd_ref, bd_ref,
                   o0_ref, o1_ref):
    i = pl.program_id(0)

    @pl.when(i < n_used_ref[0])
    def _():
        d_e = wd_ref.shape[0]
        h = jnp.concatenate([q.astype(BF16) for q in _unpack_rows(h0_ref[...], h1_ref[...])], axis=1)
        gu = jnp.dot(h, wgu_ref[...].astype(BF16), preferred_element_type=F32) + bgu_ref[...]
        g = jnp.minimum(gu[:, :d_e], SWIGLU_LIMIT)
        u = jnp.clip(gu[:, d_e:], -SWIGLU_LIMIT, SWIGLU_LIMIT)
        act = g / (1.0 + jnp.exp(-SWIGLU_ALPHA * g)) * (u + 1.0)
        y = jnp.dot(act.astype(BF16), wd_ref[...].astype(BF16), preferred_element_type=F32) + bd_ref[...]
        o0_ref[...], o1_ref[...] = _pack_rows(y)

    @pl.when(i >= n_used_ref[0])
    def _():
        o0_ref[...] = jnp.zeros(o0_ref.shape, o0_ref.dtype)
        o1_ref[...] = jnp.zeros(o1_ref.shape, o1_ref.dtype)


def _expert_mlp(block_e, n_used, rows0, rows1, wgu, bgu, wd, bd, layer, tile):
    n_rows, dq = rows0.shape
    d = 4 * dq
    n_l, n_e, _, two_de = wgu.shape
    d_e = two_de // 2
    n_blocks = n_rows // tile
    grid_spec = pltpu.PrefetchScalarGridSpec(
        num_scalar_prefetch=2,
        grid=(n_blocks,),
        in_specs=[
            pl.BlockSpec((tile, dq), lambda i, be, nu: (i, 0)),
            pl.BlockSpec((tile, dq), lambda i, be, nu: (i, 0)),
            pl.BlockSpec((None, None, d, two_de), lambda i, be, nu: (layer, be[i], 0, 0)),
            pl.BlockSpec((None, None, 1, two_de), lambda i, be, nu: (layer, be[i], 0, 0)),
            pl.BlockSpec((None, None, d_e, d), lambda i, be, nu: (layer, be[i], 0, 0)),
            pl.BlockSpec((None, None, 1, d), lambda i, be, nu: (layer, be[i], 0, 0)),
        ],
        out_specs=(pl.BlockSpec((tile, dq), lambda i, be, nu: (i, 0)),
                   pl.BlockSpec((tile, dq), lambda i, be, nu: (i, 0))),
    )
    packed = jax.ShapeDtypeStruct((n_rows, dq), jnp.uint32)
    return pl.pallas_call(
        _expert_kernel,
        out_shape=(packed, packed),
        grid_spec=grid_spec,
        compiler_params=_params("arbitrary"),
        name="expert_mlp",
    )(block_e, n_used, rows0, rows1, wgu, bgu.reshape(n_l, n_e, 1, two_de), wd,
      bd.reshape(n_l, n_e, 1, d))


def _combine_kernel(y0_ref, y1_ref, w_ref, x_ref, g_ref, o_ref):
    w = w_ref[...]
    acc = None
    for k in range(y0_ref.shape[0]):
        quarters = _unpack_rows(y0_ref[k], y1_ref[k])
        scaled = [w[:, k:k + 1] * q for q in quarters]
        acc = scaled if acc is None else [a + s for a, s in zip(acc, scaled)]
    o_ref[...] = x_ref[...] + g_ref[...] * jnp.concatenate(acc, axis=1)


def _combine(y0, y1, weights, x, gate, seq):
    t, d = x.shape
    tm = min(TOKEN_TILE, seq)
    per_seq = seq // tm
    b = gate.shape[0]
    return pl.pallas_call(
        _combine_kernel,
        out_shape=jax.ShapeDtypeStruct((t, d), F32),
        grid=(t // tm,),
        in_specs=[
            pl.BlockSpec((y0.shape[0], tm, d // 4), lambda i: (0, i, 0)),
            pl.BlockSpec((y0.shape[0], tm, d // 4), lambda i: (0, i, 0)),
            pl.BlockSpec((tm, LANES), lambda i: (i, 0)),
            pl.BlockSpec((tm, d), lambda i: (i, 0)),
            pl.BlockSpec((None, 1, d), lambda i: (i // per_seq, 0, 0)),
        ],
        out_specs=pl.BlockSpec((tm, d), lambda i: (i, 0)),
        compiler_params=_params("parallel"),
        name="moe_combine",
    )(y0, y1, weights, x, gate.reshape(b, 1, d))


ROW_BLOCK_MULTIPLE = 8


def _routing_tables(idx_t, n_experts, tile):
    top_k, t = idx_t.shape
    n_assign = top_k * t
    e_flat = idx_t.reshape(n_assign)
    experts = jnp.arange(n_experts, dtype=jnp.int32)
    onehot = e_flat[:, None] == experts[None, :]
    counts = jnp.sum(onehot, axis=0, dtype=jnp.int32)
    padded = (counts + tile - 1) // tile * tile
    start = jnp.cumsum(counts) - counts
    pend = jnp.cumsum(padded)
    pstart = pend - padded
    order = jnp.argsort(e_flat, stable=True).astype(jnp.int32)
    sorted_pos = jnp.argsort(order).astype(jnp.int32)
    shift = jnp.sum(jnp.where(onehot, (pstart - start)[None, :], 0), axis=1, dtype=jnp.int32)
    slot_row = sorted_pos + shift
    n_blocks = (n_assign + n_experts * (tile - 1) + tile - 1) // tile
    n_blocks = -(-n_blocks // ROW_BLOCK_MULTIPLE) * ROW_BLOCK_MULTIPLE
    first_row = jnp.arange(n_blocks, dtype=jnp.int32) * tile
    block_e = jnp.minimum(jnp.sum(pend[None, :] <= first_row[:, None], axis=1), n_experts - 1)
    block_e = block_e.astype(jnp.int32)
    n_used = (pend[-1] // tile).astype(jnp.int32).reshape(1)
    within = (first_row - pstart[block_e])[:, None] + jnp.arange(tile, dtype=jnp.int32)[None, :]
    valid = within < counts[block_e][:, None]
    src = jnp.where(valid, start[block_e][:, None] + within, 0).reshape(-1)
    row_tok = jnp.where(valid.reshape(-1), order[src], jnp.arange(n_blocks * tile, dtype=jnp.int32)) % t
    return row_tok, block_e, n_used, slot_row


def _moe(x, gain, shift, scale, gate, rw, rb, wgu, bgu, wd, bd, layer, seq):
    t, d = x.shape
    n_e = rw.shape[1]
    rw_pad = jnp.pad(rw, ((0, 0), (0, LANES - n_e)))
    rb_pad = jnp.pad(rb, (0, LANES - n_e), constant_values=MASK_VALUE).reshape(1, LANES)
    h0, h1, idx_t, weights = _moe_route(x, gain, shift, scale, rw_pad, rb_pad, seq)
    tile = min(EXPERT_TILE, t * TOP_K // n_e)
    row_tok, block_e, n_used, slot_row = _routing_tables(idx_t[:TOP_K], n_e, tile)
    rows0, rows1 = _sc_gather_rows(h0, h1, row_tok)
    ys0, ys1 = _expert_mlp(block_e, n_used, rows0, rows1, wgu, bgu, wd, bd, layer, tile)
    y0, y1 = _sc_gather_rows(ys0, ys1, slot_row)
    return _combine(y0.reshape(TOP_K, t, d // 4), y1.reshape(TOP_K, t, d // 4), weights, x, gate, seq)


def _rmsnorm_kernel(x_ref, g_ref, o_ref):
    x = x_ref[...]
    ms = jnp.mean(x * x, axis=-1, keepdims=True)
    o_ref[...] = x * lax.rsqrt(ms + EPS) * g_ref[...]


def _rmsnorm(x, gain):
    t, d = x.shape
    tm = min(TOKEN_TILE, t)
    return pl.pallas_call(
        _rmsnorm_kernel,
        out_shape=jax.ShapeDtypeStruct((t, d), F32),
        grid=(t // tm,),
        in_specs=[pl.BlockSpec((tm, d), lambda i: (i, 0)),
                  pl.BlockSpec((1, d), lambda i: (0, 0))],
        out_specs=pl.BlockSpec((tm, d), lambda i: (i, 0)),
        compiler_params=_params("parallel"),
        name="final_rmsnorm",
    )(x, gain.reshape(1, d))


def kernel(x, c, ada_w, ada_b, norm_mix, norm_moe, a_wqkv, a_wo, a_lambda, a_subln, kv_norm,
           kv_ada_w, kv_ada_b, kv_w, b_wq, b_wo, router_w, router_b, w_gate_up, b_gate_up,
           w_down, b_down, final_norm):
    b, s, d = x.shape
    depth = ada_w.shape[0]
    n_a = a_wqkv.shape[0]
    xt = x.reshape(b * s, d)

    c_pad = jnp.pad(c, ((0, SUBLANES - b), (0, 0)))
    mod = _modulation(c_pad, ada_w, ada_b)[:, :b]
    kv_mod = _modulation(c_pad, kv_ada_w[None], kv_ada_b[None])[0, :b]

    for l in range(depth):
        sh1, sc1, g1, sh2, sc2, g2 = jnp.split(mod[l], 6, axis=-1)
        if l < n_a:
            qkv = _norm_linear(xt, norm_mix[l], sh1, sc1, a_wqkv[l].astype(BF16), s)
            o = _diff_attention(qkv, a_lambda[l], a_subln[l], b, s, l)
            xt = _proj_residual(o, a_wo[l].astype(BF16), xt, g1, s)
        else:
            if l == n_a:
                kv_sh, kv_sc = jnp.split(kv_mod, 2, axis=-1)
                kv = _norm_linear(xt, kv_norm, kv_sh, kv_sc, kv_w.astype(BF16), s)
            j = l - n_a
            q = _norm_linear(xt, norm_mix[l], sh1, sc1, b_wq[j].astype(BF16), s)
            o = _stick_attention(q, kv, b, s)
            xt = _proj_residual(o, b_wo[j].astype(BF16), xt, g1, s)
        xt = _moe(xt, norm_moe[l], sh2, sc2, g2, router_w[l], router_b[l],
                  w_gate_up, b_gate_up, w_down, b_down, l, s)
    return _rmsnorm(xt, final_norm).reshape(b, s, d)
```

```python
import functools
import math

import jax
import jax.numpy as jnp
import numpy as np
from jax import lax
from jax.experimental import pallas as pl
from jax.experimental.pallas import tpu as pltpu
from jax.experimental.pallas import tpu_sc as plsc

F32 = jnp.float32
BF16 = jnp.bfloat16
EPS = 1e-6
MASK_VALUE = -1e30
TOP_K = 4
SWIGLU_ALPHA = 1.702
SWIGLU_LIMIT = 7.0
LOG2E = math.log2(math.e)
UNDERFLOW_BITS = 160.0
NORM_SLACK = 1.03

LANES = 128
SUBLANES = 8
HEAD_DIM = 64
V7X_VMEM_BYTES = 64 * 2 ** 20
VMEM_LIMIT = V7X_VMEM_BYTES * 7 // 8

TOKEN_TILE = 512
ATTN_TILE = 512
EXPERT_TILE = 512


def _params(*semantics):
    return pltpu.CompilerParams(dimension_semantics=semantics, vmem_limit_bytes=VMEM_LIMIT)


def _largest_tile(n, cap):
    t = min(n, cap) // LANES * LANES
    while n % t:
        t -= LANES
    return t


def _adaln(x, gain, shift, scale):
    ms = jnp.mean(x * x, axis=-1, keepdims=True)
    return x * lax.rsqrt(ms + EPS) * gain * (1.0 + scale) + shift


def _pack_bf16_pair(lo, hi):
    lo_bits = lax.bitcast_convert_type(lo.astype(BF16).astype(F32), jnp.uint32) >> 16
    hi_bits = lax.bitcast_convert_type(hi.astype(BF16).astype(F32), jnp.uint32) & jnp.uint32(0xFFFF0000)
    return lo_bits | hi_bits


def _unpack_bf16_pair(packed):
    lo = lax.bitcast_convert_type(packed << 16, F32)
    hi = lax.bitcast_convert_type(packed & jnp.uint32(0xFFFF0000), F32)
    return lo, hi


def _pack_rows(x):
    q = x.shape[1] // 4
    return (_pack_bf16_pair(x[:, :q], x[:, 2 * q:3 * q]),
            _pack_bf16_pair(x[:, q:2 * q], x[:, 3 * q:]))


def _unpack_rows(p0, p1):
    a0, a2 = _unpack_bf16_pair(p0)
    a1, a3 = _unpack_bf16_pair(p1)
    return a0, a1, a2, a3


def _mod_kernel(c_ref, w_ref, b_ref, o_ref):
    c = c_ref[...]
    a = c / (1.0 + jnp.exp(-c))
    o_ref[...] = jnp.dot(a, w_ref[...], precision=lax.Precision.HIGHEST,
                         preferred_element_type=F32) + b_ref[...]


def _modulation(c_pad, w, b):
    n_l, d, n = w.shape
    tn = _largest_tile(n, 2048)
    return pl.pallas_call(
        _mod_kernel,
        out_shape=jax.ShapeDtypeStruct((n_l, SUBLANES, n), F32),
        grid=(n_l, n // tn),
        in_specs=[
            pl.BlockSpec((SUBLANES, d), lambda l, j: (0, 0)),
            pl.BlockSpec((None, d, tn), lambda l, j: (l, 0, j)),
            pl.BlockSpec((None, 1, tn), lambda l, j: (l, 0, j)),
        ],
        out_specs=pl.BlockSpec((None, SUBLANES, tn), lambda l, j: (l, 0, j)),
        compiler_params=_params("parallel", "parallel"),
        name="modulation",
    )(c_pad, w, b.reshape(n_l, 1, n))


def _norm_linear_kernel(x_ref, g_ref, sh_ref, sc_ref, w_ref, o_ref):
    h = _adaln(x_ref[...], g_ref[...], sh_ref[...], sc_ref[...])
    o_ref[...] = jnp.dot(h.astype(BF16), w_ref[...], preferred_element_type=F32).astype(o_ref.dtype)


def _norm_linear(x, gain, shift, scale, w, seq):
    t, d = x.shape
    n = w.shape[1]
    tm = min(TOKEN_TILE, seq)
    per_seq = seq // tm
    b = shift.shape[0]
    return pl.pallas_call(
        _norm_linear_kernel,
        out_shape=jax.ShapeDtypeStruct((t, n), BF16),
        grid=(t // tm,),
        in_specs=[
            pl.BlockSpec((tm, d), lambda i: (i, 0)),
            pl.BlockSpec((1, d), lambda i: (0, 0)),
            pl.BlockSpec((None, 1, d), lambda i: (i // per_seq, 0, 0)),
            pl.BlockSpec((None, 1, d), lambda i: (i // per_seq, 0, 0)),
            pl.BlockSpec((d, n), lambda i: (0, 0)),
        ],
        out_specs=pl.BlockSpec((tm, n), lambda i: (i, 0)),
        compiler_params=_params("parallel"),
        name="norm_linear",
    )(x, gain.reshape(1, d), shift.reshape(b, 1, d), scale.reshape(b, 1, d), w)


def _proj_residual_kernel(a_ref, w_ref, x_ref, g_ref, o_ref):
    y = jnp.dot(a_ref[...], w_ref[...], preferred_element_type=F32)
    o_ref[...] = x_ref[...] + g_ref[...] * y


def _proj_residual(a, w, x, gate, seq):
    t, d = x.shape
    k = a.shape[1]
    tm = min(TOKEN_TILE, seq)
    per_seq = seq // tm
    b = gate.shape[0]
    return pl.pallas_call(
        _proj_residual_kernel,
        out_shape=jax.ShapeDtypeStruct((t, d), F32),
        grid=(t // tm,),
        in_specs=[
            pl.BlockSpec((tm, k), lambda i: (i, 0)),
            pl.BlockSpec((k, d), lambda i: (0, 0)),
            pl.BlockSpec((tm, d), lambda i: (i, 0)),
            pl.BlockSpec((None, 1, d), lambda i: (i // per_seq, 0, 0)),
        ],
        out_specs=pl.BlockSpec((tm, d), lambda i: (i, 0)),
        compiler_params=_params("parallel"),
        name="proj_residual",
    )(a, w, x, gate.reshape(b, 1, d))


def _scaled_halves(q_ref):
    q = (q_ref[...].astype(F32) * (HEAD_DIM ** -0.5 * LOG2E)).astype(BF16)
    lane = lax.broadcasted_iota(jnp.int32, q.shape, 1)
    zero = jnp.zeros_like(q)
    return jnp.concatenate([jnp.where(lane < HEAD_DIM, q, zero),
                            jnp.where(lane >= HEAD_DIM, q, zero)], axis=0)


def _dot_nt(a, b):
    return lax.dot_general(a, b, (((1,), (1,)), ((), ())), preferred_element_type=F32)


def _tile_row_col(tq, tk):
    row = lax.broadcasted_iota(jnp.int32, (2 * tq, tk), 0)
    row = jnp.where(row >= tq, row - tq, row)
    col = lax.broadcasted_iota(jnp.int32, (2 * tq, tk), 1)
    return row, col


def _lane_chunks(n):
    return [slice(c * LANES, (c + 1) * LANES) for c in range(n // LANES)]


def _diff_attn_kernel(q_ref, k_ref, v_ref, slope_ref, lam_ref, subln_ref, o_ref,
                      m_ref, l_ref, acc_ref, sa_ref, sb_ref, kmax_ref, *, tile, lam_init):
    i = pl.program_id(2)
    qq = _scaled_halves(q_ref)
    slope = slope_ref[...]
    col_i = lax.broadcasted_iota(jnp.int32, (1, tile), 1)
    ones = jnp.ones((tile, LANES), BF16)
    m_ref[...] = jnp.full(m_ref.shape, MASK_VALUE, F32)
    l_ref[...] = jnp.zeros(l_ref.shape, F32)
    acc_ref[...] = jnp.zeros(acc_ref.shape, F32)

    def scores(s_ref, kb):
        k = k_ref[pl.ds(pl.multiple_of(kb * tile, tile), tile), :]
        s_ref[...] = _dot_nt(qq, k) + slope * (col_i + (kb - i) * tile).astype(F32)

    def accumulate(s_ref, kb, diagonal):
        v = v_ref[pl.ds(pl.multiple_of(kb * tile, tile), tile), :]
        if diagonal:
            row, col = _tile_row_col(tile, tile)
            s_ref[...] = jnp.where(col <= row, s_ref[...], MASK_VALUE)
        m_prev = m_ref[...]
        m_new = jnp.maximum(m_prev, jnp.max(s_ref[...], axis=-1, keepdims=True))
        alpha = jnp.exp2(m_prev - m_new)
        p = jnp.concatenate([jnp.exp2(s_ref[:, sl] - m_new).astype(BF16) for sl in _lane_chunks(tile)],
                            axis=1)
        pv = jnp.dot(p, jnp.concatenate([v, ones], axis=1),
                     preferred_element_type=F32)
        acc_ref[...] = alpha * acc_ref[...] + pv[:, :LANES]
        l_ref[...] = alpha * l_ref[...] + pv[:, LANES:]
        m_ref[...] = m_new

    lane = lax.broadcasted_iota(jnp.int32, (1, LANES), 1)
    half_ones = (lax.broadcasted_iota(jnp.int32, (LANES, LANES), 0) < HEAD_DIM) == (
        lax.broadcasted_iota(jnp.int32, (LANES, LANES), 1) < HEAD_DIM)
    half_ones = jnp.where(half_ones, 1.0, 0.0).astype(BF16)

    @pl.when(i == 0)
    def _():
        def chunk_max(c, best):
            k = k_ref[pl.ds(pl.multiple_of(c * tile, tile), tile), :].astype(F32)
            n2 = jnp.dot((k * k).astype(BF16), half_ones, preferred_element_type=F32)
            return jnp.maximum(best, jnp.max(n2, axis=0, keepdims=True))

        best = lax.fori_loop(0, k_ref.shape[0] // tile, chunk_max, jnp.zeros((1, LANES), F32))
        kmax_ref[...] = jnp.broadcast_to(best * NORM_SLACK, kmax_ref.shape)

    qf = qq.astype(F32)
    qn2 = jnp.dot((qf * qf).astype(BF16), jnp.ones((LANES, LANES), BF16), preferred_element_type=F32)
    q_half = jnp.where(lane < HEAD_DIM, jnp.max(qn2[:tile], axis=0, keepdims=True),
                       jnp.max(qn2[tile:], axis=0, keepdims=True))
    raw_bound = jnp.sqrt(jnp.max(q_half * NORM_SLACK * kmax_ref[0:1, :], axis=1, keepdims=True))

    def live(kb):
        far = jnp.zeros((1, LANES), jnp.int32) + ((kb + 1 - i) * tile - 1)
        best_left = raw_bound + slope[:, :LANES] * far.astype(F32)
        lowest_max = jnp.min(m_ref[...], axis=0, keepdims=True)
        return jnp.max(best_left - lowest_max) > -UNDERFLOW_BITS

    scores(sa_ref, i)

    @pl.when(i == 0)
    def _():
        accumulate(sa_ref, 0, True)

    @pl.when(i == 1)
    def _():
        scores(sb_ref, 0)
        accumulate(sa_ref, 1, True)
        accumulate(sb_ref, 0, False)

    @pl.when(i >= 2)
    def _():
        scores(sb_ref, i - 1)
        accumulate(sa_ref, i, True)
        scores(sa_ref, i - 2)
        accumulate(sb_ref, i - 1, False)
        n_pairs = (i - 2) // 2

        def cond(carry):
            j, go = carry
            return jnp.logical_and(j < n_pairs, go)

        def body(carry):
            j, _ = carry
            kb = i - 2 - 2 * j
            scores(sb_ref, kb - 1)
            accumulate(sa_ref, kb, False)
            scores(sa_ref, kb - 2)
            accumulate(sb_ref, kb - 1, False)
            return j + 1, live(kb - 2)

        _, go = lax.while_loop(cond, body, (jnp.int32(0), live(i - 2)))

        @pl.when(jnp.logical_and(go, i % 2 == 0))
        def _():
            accumulate(sa_ref, 0, False)

        @pl.when(jnp.logical_and(go, i % 2 == 1))
        def _():
            scores(sb_ref, 0)
            accumulate(sa_ref, 1, False)
            accumulate(sb_ref, 0, False)

    o_all = acc_ref[...] / l_ref[...]
    lv = lam_ref[...]
    lam = (jnp.exp(jnp.sum(lv[0:1] * lv[1:2], axis=1, keepdims=True))
           - jnp.exp(jnp.sum(lv[2:3] * lv[3:4], axis=1, keepdims=True)) + lam_init)
    o = o_all[:tile] - lam * o_all[tile:]
    ms = jnp.mean(o * o, axis=-1, keepdims=True)
    o = o * lax.rsqrt(ms + EPS) * subln_ref[...] * (1.0 - lam_init)
    o_ref[...] = o.astype(o_ref.dtype)


def _diff_attention(qkv, lam_vecs, subln, batch, seq, layer_idx):
    t = qkv.shape[0]
    d = qkv.shape[1] // 3
    heads = d // LANES
    tile = min(ATTN_TILE, seq)
    nq = seq // tile
    lam_init = 0.8 - 0.6 * math.exp(-0.3 * layer_idx)
    slopes = np.array([2.0 ** (-8.0 * (h + 1) / heads) * LOG2E for h in range(heads)], np.float32)
    slopes = jnp.asarray(np.broadcast_to(slopes[:, None, None], (heads, 1, tile)))
    kernel = functools.partial(_diff_attn_kernel, tile=tile, lam_init=lam_init)
    return pl.pallas_call(
        kernel,
        out_shape=jax.ShapeDtypeStruct((t, d), BF16),
        grid=(batch, heads, nq),
        in_specs=[
            pl.BlockSpec((tile, LANES), lambda b, h, i: (b * nq + i, h)),
            pl.BlockSpec((seq, LANES), lambda b, h, i: (b, heads + h)),
            pl.BlockSpec((seq, LANES), lambda b, h, i: (b, 2 * heads + h)),
            pl.BlockSpec((None, 1, tile), lambda b, h, i: (h, 0, 0)),
            pl.BlockSpec(lam_vecs.shape, lambda b, h, i: (0, 0)),
            pl.BlockSpec((1, LANES), lambda b, h, i: (0, 0)),
        ],
        out_specs=pl.BlockSpec((tile, LANES), lambda b, h, i: (b * nq + i, h)),
        scratch_shapes=[
            pltpu.VMEM((2 * tile, LANES), F32),
            pltpu.VMEM((2 * tile, LANES), F32),
            pltpu.VMEM((2 * tile, LANES), F32),
            pltpu.VMEM((2 * tile, tile), F32),
            pltpu.VMEM((2 * tile, tile), F32),
            pltpu.VMEM((SUBLANES, LANES), F32),
        ],
        compiler_params=_params("parallel", "parallel", "arbitrary"),
        name="diff_attention",
    )(qkv, qkv, qkv, slopes, lam_vecs, subln.reshape(1, LANES))


CHUNK = 256


def _suffix_sum_matrix():
    j = np.arange(CHUNK)
    return jnp.asarray((j[:, None] >= j[None, :]).astype(np.float32), dtype=BF16)


def _stick_attn_kernel(q_ref, k_ref, v_ref, mt_ref, o_ref, rem_ref, acc_ref, za_ref, zb_ref, *, tile):
    i = pl.program_id(2)
    qq = _scaled_halves(q_ref)
    mt = mt_ref[...]
    rem_ref[...] = jnp.zeros(rem_ref.shape, F32)
    acc_ref[...] = jnp.zeros(acc_ref.shape, F32)

    def scores(z_ref, kb):
        z_ref[...] = _dot_nt(qq, k_ref[pl.ds(pl.multiple_of(kb * tile, tile), tile), :])

    def accumulate(z_ref, kb, diagonal):
        v = v_ref[pl.ds(pl.multiple_of(kb * tile, tile), tile), :]
        rem = rem_ref[...]
        chunks = []
        for c in reversed(range(tile // CHUNK)):
            z = z_ref[:, c * CHUNK:(c + 1) * CHUNK]
            softplus = jnp.maximum(z, 0.0) + jnp.log2(1.0 + jnp.exp2(-jnp.abs(z)))
            if diagonal:
                row = lax.broadcasted_iota(jnp.int32, z.shape, 0)
                row = jnp.where(row >= tile, row - tile, row)
                strict = lax.broadcasted_iota(jnp.int32, z.shape, 1) + c * CHUNK < row
                softplus = jnp.where(strict, softplus, 0.0)
            sums = jnp.dot(softplus.astype(BF16), mt, preferred_element_type=F32)
            for h in reversed(range(CHUNK // LANES)):
                sl = slice(h * LANES, (h + 1) * LANES)
                a = jnp.exp2(z_ref[:, c * CHUNK + h * LANES:c * CHUNK + (h + 1) * LANES] - sums[:, sl] - rem)
                if diagonal:
                    a = jnp.where(strict[:, sl], a, 0.0)
                chunks.append(a.astype(BF16))
            rem = rem + jnp.broadcast_to(sums[:, 0:1], rem.shape)
        rem_ref[...] = rem
        acc_ref[...] += jnp.dot(jnp.concatenate(chunks[::-1], axis=1), v, preferred_element_type=F32)

    def live():
        return jnp.min(rem_ref[...]) < UNDERFLOW_BITS

    scores(za_ref, i)

    @pl.when(i == 0)
    def _():
        accumulate(za_ref, 0, True)

    @pl.when(i == 1)
    def _():
        scores(zb_ref, 0)
        accumulate(za_ref, 1, True)
        accumulate(zb_ref, 0, False)

    @pl.when(i >= 2)
    def _():
        scores(zb_ref, i - 1)
        accumulate(za_ref, i, True)
        scores(za_ref, i - 2)
        accumulate(zb_ref, i - 1, False)
        n_pairs = (i - 2) // 2

        def cond(carry):
            j, go = carry
            return jnp.logical_and(j < n_pairs, go)

        def body(carry):
            j, _ = carry
            kb = i - 2 - 2 * j
            scores(zb_ref, kb - 1)
            accumulate(za_ref, kb, False)
            scores(za_ref, kb - 2)
            accumulate(zb_ref, kb - 1, False)
            return j + 1, live()

        _, go = lax.while_loop(cond, body, (jnp.int32(0), live()))

        @pl.when(jnp.logical_and(go, i % 2 == 0))
        def _():
            accumulate(za_ref, 0, False)

        @pl.when(jnp.logical_and(go, i % 2 == 1))
        def _():
            scores(zb_ref, 0)
            accumulate(za_ref, 1, False)
            accumulate(zb_ref, 0, False)

    acc = acc_ref[...]
    lane = lax.broadcasted_iota(jnp.int32, (tile, LANES), 1)
    o_ref[...] = jnp.where(lane < HEAD_DIM, acc[:tile], acc[tile:]).astype(o_ref.dtype)


def _stick_attention(q, kv, batch, seq):
    t, d = q.shape
    pairs = d // LANES
    tile = min(ATTN_TILE, seq)
    nq = seq // tile
    kernel = functools.partial(_stick_attn_kernel, tile=tile)
    return pl.pallas_call(
        kernel,
        out_shape=jax.ShapeDtypeStruct((t, d), BF16),
        grid=(batch, pairs, nq),
        in_specs=[
            pl.BlockSpec((tile, LANES), lambda b, j, i: (b * nq + i, j)),
            pl.BlockSpec((seq, LANES), lambda b, j, i: (b, j)),
            pl.BlockSpec((seq, LANES), lambda b, j, i: (b, pairs + j)),
            pl.BlockSpec((CHUNK, CHUNK), lambda b, j, i: (0, 0)),
        ],
        out_specs=pl.BlockSpec((tile, LANES), lambda b, j, i: (b * nq + i, j)),
        scratch_shapes=[
            pltpu.VMEM((2 * tile, LANES), F32),
            pltpu.VMEM((2 * tile, LANES), F32),
            pltpu.VMEM((2 * tile, tile), F32),
            pltpu.VMEM((2 * tile, tile), F32),
        ],
        compiler_params=_params("parallel", "parallel", "arbitrary"),
        name="stick_attention",
    )(q, kv, kv, _suffix_sum_matrix())


SC_WINDOW = 128


def _sc_gather_rows(tab0, tab1, idx):
    m = idx.shape[0]
    w = tab0.shape[1]
    mesh = plsc.VectorSubcoreMesh(core_axis_name="core", subcore_axis_name="subcore")
    assert m % (SC_WINDOW * mesh.num_cores * mesh.num_subcores) == 0, m
    out = jax.ShapeDtypeStruct((m, w), tab0.dtype)

    @pl.kernel(out_type=(out, out), mesh=mesh)
    def gather(t0_hbm, t1_hbm, i_hbm, o0_hbm, o1_hbm):
        for t_hbm, o_hbm in ((t0_hbm, o0_hbm), (t1_hbm, o1_hbm)):
            def body(i_vmem, o_vmem, t_hbm=t_hbm):
                pltpu.sync_copy(t_hbm.at[i_vmem.at[0]], o_vmem)

            pltpu.emit_pipeline(
                body,
                grid=(m // SC_WINDOW,),
                in_specs=[pl.BlockSpec((1, SC_WINDOW), lambda i: (0, i))],
                out_specs=[pl.BlockSpec((SC_WINDOW, w), lambda i: (i, 0))],
                core_axis_name=("core", "subcore"),
                dimension_semantics=(pltpu.PARALLEL,),
            )(i_hbm, o_hbm)

    return gather(tab0, tab1, idx.reshape(1, m))


def _moe_route_kernel(x_ref, g_ref, sh_ref, sc_ref, rw_ref, rb_ref, h0_ref, h1_ref, idx_ref, gate_ref):
    h = _adaln(x_ref[...], g_ref[...], sh_ref[...], sc_ref[...])
    h0_ref[...], h1_ref[...] = _pack_rows(h)
    logits = jnp.dot(h, rw_ref[...], precision=lax.Precision.HIGHEST,
                     preferred_element_type=F32) + rb_ref[...]
    lane = lax.broadcasted_iota(jnp.int32, logits.shape, 1)
    lane_f = lane.astype(F32)
    vals, ids = [], []
    for _ in range(TOP_K):
        m = jnp.max(logits, axis=-1, keepdims=True)
        first = jnp.min(jnp.where(logits == m, lane_f, float(LANES)), axis=-1, keepdims=True)
        vals.append(m)
        ids.append(first)
        logits = jnp.where(lane_f == first, -jnp.inf, logits)
    exps = [jnp.exp(v - vals[0]) for v in vals]
    denom = exps[0]
    for e in exps[1:]:
        denom = denom + e
    gate = jnp.zeros(logits.shape, F32)
    idx = jnp.zeros(logits.shape, F32)
    for k in range(TOP_K):
        gate = jnp.where(lane == k, exps[k] / denom, gate)
        idx = jnp.where(lane == k, ids[k], idx)
    gate_ref[...] = gate
    idx_ref[...] = idx.T[:SUBLANES].astype(jnp.int32)


def _moe_route(x, gain, shift, scale, rw_pad, rb_pad, seq):
    t, d = x.shape
    tm = min(TOKEN_TILE, seq)
    per_seq = seq // tm
    b = shift.shape[0]
    return pl.pallas_call(
        _moe_route_kernel,
        out_shape=(jax.ShapeDtypeStruct((t, d // 4), jnp.uint32),
                   jax.ShapeDtypeStruct((t, d // 4), jnp.uint32),
                   jax.ShapeDtypeStruct((SUBLANES, t), jnp.int32),
                   jax.ShapeDtypeStruct((t, LANES), F32)),
        grid=(t // tm,),
        in_specs=[
            pl.BlockSpec((tm, d), lambda i: (i, 0)),
            pl.BlockSpec((1, d), lambda i: (0, 0)),
            pl.BlockSpec((None, 1, d), lambda i: (i // per_seq, 0, 0)),
            pl.BlockSpec((None, 1, d), lambda i: (i // per_seq, 0, 0)),
            pl.BlockSpec((d, LANES), lambda i: (0, 0)),
            pl.BlockSpec((1, LANES), lambda i: (0, 0)),
        ],
        out_specs=(pl.BlockSpec((tm, d // 4), lambda i: (i, 0)),
                   pl.BlockSpec((tm, d // 4), lambda i: (i, 0)),
                   pl.BlockSpec((SUBLANES, tm), lambda i: (0, i)),
                   pl.BlockSpec((tm, LANES), lambda i: (i, 0))),
        compiler_params=_params("parallel"),
        name="moe_route",
    )(x, gain.reshape(1, d), shift.reshape(b, 1, d), scale.reshape(b, 1, d), rw_pad, rb_pad)


def _expert_kernel(block_e_ref, n_used_ref, h0_ref, h1_ref, wgu_ref, bgu_ref, wd_ref, bd_ref,
                   o0_ref, o1_ref):
    i = pl.program_id(0)

    @pl.when(i < n_used_ref[0])
    def _():
        d_e = wd_ref.shape[0]
        h = jnp.concatenate([q.astype(BF16) for q in _unpack_rows(h0_ref[...], h1_ref[...])], axis=1)
        gu = jnp.dot(h, wgu_ref[...].astype(BF16), preferred_element_type=F32) + bgu_ref[...]
        g = jnp.minimum(gu[:, :d_e], SWIGLU_LIMIT)
        u = jnp.clip(gu[:, d_e:], -SWIGLU_LIMIT, SWIGLU_LIMIT)
        act = g / (1.0 + jnp.exp(-SWIGLU_ALPHA * g)) * (u + 1.0)
        y = jnp.dot(act.astype(BF16), wd_ref[...].astype(BF16), preferred_element_type=F32) + bd_ref[...]
        o0_ref[...], o1_ref[...] = _pack_rows(y)

    @pl.when(i >= n_used_ref[0])
    def _():
        o0_ref[...] = jnp.zeros(o0_ref.shape, o0_ref.dtype)
        o1_ref[...] = jnp.zeros(o1_ref.shape, o1_ref.dtype)


def _expert_mlp(block_e, n_used, rows0, rows1, wgu, bgu, wd, bd, layer, tile):
    n_rows, dq = rows0.shape
    d = 4 * dq
    n_l, n_e, _, two_de = wgu.shape
    d_e = two_de // 2
    n_blocks = n_rows // tile
    grid_spec = pltpu.PrefetchScalarGridSpec(
        num_scalar_prefetch=2,
        grid=(n_blocks,),
        in_specs=[
            pl.BlockSpec((tile, dq), lambda i, be, nu: (i, 0)),
            pl.BlockSpec((tile, dq), lambda i, be, nu: (i, 0)),
            pl.BlockSpec((None, None, d, two_de), lambda i, be, nu: (layer, be[i], 0, 0)),
            pl.BlockSpec((None, None, 1, two_de), lambda i, be, nu: (layer, be[i], 0, 0)),
            pl.BlockSpec((None, None, d_e, d), lambda i, be, nu: (layer, be[i], 0, 0)),
            pl.BlockSpec((None, None, 1, d), lambda i, be, nu: (layer, be[i], 0, 0)),
        ],
        out_specs=(pl.BlockSpec((tile, dq), lambda i, be, nu: (i, 0)),
                   pl.BlockSpec((tile, dq), lambda i, be, nu: (i, 0))),
    )
    packed = jax.ShapeDtypeStruct((n_rows, dq), jnp.uint32)
    return pl.pallas_call(
        _expert_kernel,
        out_shape=(packed, packed),
        grid_spec=grid_spec,
        compiler_params=_params("arbitrary"),
        name="expert_mlp",
    )(block_e, n_used, rows0, rows1, wgu, bgu.reshape(n_l, n_e, 1, two_de), wd,
      bd.reshape(n_l, n_e, 1, d))


def _combine_kernel(y0_ref, y1_ref, w_ref, x_ref, g_ref, o_ref):
    w = w_ref[...]
    acc = None
    for k in range(y0_ref.shape[0]):
        quarters = _unpack_rows(y0_ref[k], y1_ref[k])
        scaled = [w[:, k:k + 1] * q for q in quarters]
        acc = scaled if acc is None else [a + s for a, s in zip(acc, scaled)]
    o_ref[...] = x_ref[...] + g_ref[...] * jnp.concatenate(acc, axis=1)


def _combine(y0, y1, weights, x, gate, seq):
    t, d = x.shape
    tm = min(TOKEN_TILE, seq)
    per_seq = seq // tm
    b = gate.shape[0]
    return pl.pallas_call(
        _combine_kernel,
        out_shape=jax.ShapeDtypeStruct((t, d), F32),
        grid=(t // tm,),
        in_specs=[
            pl.BlockSpec((y0.shape[0], tm, d // 4), lambda i: (0, i, 0)),
            pl.BlockSpec((y0.shape[0], tm, d // 4), lambda i: (0, i, 0)),
            pl.BlockSpec((tm, LANES), lambda i: (i, 0)),
            pl.BlockSpec((tm, d), lambda i: (i, 0)),
            pl.BlockSpec((None, 1, d), lambda i: (i // per_seq, 0, 0)),
        ],
        out_specs=pl.BlockSpec((tm, d), lambda i: (i, 0)),
        compiler_params=_params("parallel"),
        name="moe_combine",
    )(y0, y1, weights, x, gate.reshape(b, 1, d))


ROW_BLOCK_MULTIPLE = 8


def _routing_tables(idx_t, n_experts, tile):
    top_k, t = idx_t.shape
    n_assign = top_k * t
    e_flat = idx_t.reshape(n_assign)
    experts = jnp.arange(n_experts, dtype=jnp.int32)
    onehot = e_flat[:, None] == experts[None, :]
    counts = jnp.sum(onehot, axis=0, dtype=jnp.int32)
    padded = (counts + tile - 1) // tile * tile
    start = jnp.cumsum(counts) - counts
    pend = jnp.cumsum(padded)
    pstart = pend - padded
    order = jnp.argsort(e_flat, stable=True).astype(jnp.int32)
    sorted_pos = jnp.argsort(order).astype(jnp.int32)
    shift = jnp.sum(jnp.where(onehot, (pstart - start)[None, :], 0), axis=1, dtype=jnp.int32)
    slot_row = sorted_pos + shift
    n_blocks = (n_assign + n_experts * (tile - 1) + tile - 1) // tile
    n_blocks = -(-n_blocks // ROW_BLOCK_MULTIPLE) * ROW_BLOCK_MULTIPLE
    first_row = jnp.arange(n_blocks, dtype=jnp.int32) * tile
    block_e = jnp.minimum(jnp.sum(pend[None, :] <= first_row[:, None], axis=1), n_experts - 1)
    block_e = block_e.astype(jnp.int32)
    n_used = (pend[-1] // tile).astype(jnp.int32).reshape(1)
    within = (first_row - pstart[block_e])[:, None] + jnp.arange(tile, dtype=jnp.int32)[None, :]
    valid = within < counts[block_e][:, None]
    src = jnp.where(valid, start[block_e][:, None] + within, 0).reshape(-1)
    row_tok = jnp.where(valid.reshape(-1), order[src], jnp.arange(n_blocks * tile, dtype=jnp.int32)) % t
    return row_tok, block_e, n_used, slot_row


def _moe(x, gain, shift, scale, gate, rw, rb, wgu, bgu, wd, bd, layer, seq):
    t, d = x.shape
    n_e = rw.shape[1]
    rw_pad = jnp.pad(rw, ((0, 0), (0, LANES - n_e)))
    rb_pad = jnp.pad(rb, (0, LANES - n_e), constant_values=MASK_VALUE).reshape(1, LANES)
    h0, h1, idx_t, weights = _moe_route(x, gain, shift, scale, rw_pad, rb_pad, seq)
    tile = min(EXPERT_TILE, t * TOP_K // n_e)
    row_tok, block_e, n_used, slot_row = _routing_tables(idx_t[:TOP_K], n_e, tile)
    rows0, rows1 = _sc_gather_rows(h0, h1, row_tok)
    ys0, ys1 = _expert_mlp(block_e, n_used, rows0, rows1, wgu, bgu, wd, bd, layer, tile)
    y0, y1 = _sc_gather_rows(ys0, ys1, slot_row)
    return _combine(y0.reshape(TOP_K, t, d // 4), y1.reshape(TOP_K, t, d // 4), weights, x, gate, seq)


def _rmsnorm_kernel(x_ref, g_ref, o_ref):
    x = x_ref[...]
    ms = jnp.mean(x * x, axis=-1, keepdims=True)
    o_ref[...] = x * lax.rsqrt(ms + EPS) * g_ref[...]


def _rmsnorm(x, gain):
    t, d = x.shape
    tm = min(TOKEN_TILE, t)
    return pl.pallas_call(
        _rmsnorm_kernel,
        out_shape=jax.ShapeDtypeStruct((t, d), F32),
        grid=(t // tm,),
        in_specs=[pl.BlockSpec((tm, d), lambda i: (i, 0)),
                  pl.BlockSpec((1, d), lambda i: (0, 0))],
        out_specs=pl.BlockSpec((tm, d), lambda i: (i, 0)),
        compiler_params=_params("parallel"),
        name="final_rmsnorm",
    )(x, gain.reshape(1, d))


def kernel(x, c, ada_w, ada_b, norm_mix, norm_moe, a_wqkv, a_wo, a_lambda, a_subln, kv_norm,
           kv_ada_w, kv_ada_b, kv_w, b_wq, b_wo, router_w, router_b, w_gate_up, b_gate_up,
           w_down, b_down, final_norm):
    b, s, d = x.shape
    depth = ada_w.shape[0]
    n_a = a_wqkv.shape[0]
    xt = x.reshape(b * s, d)

    c_pad = jnp.pad(c, ((0, SUBLANES - b), (0, 0)))
    mod = _modulation(c_pad, ada_w, ada_b)[:, :b]
    kv_mod = _modulation(c_pad, kv_ada_w[None], kv_ada_b[None])[0, :b]

    for l in range(depth):
        sh1, sc1, g1, sh2, sc2, g2 = jnp.split(mod[l], 6, axis=-1)
        if l < n_a:
            qkv = _norm_linear(xt, norm_mix[l], sh1, sc1, a_wqkv[l].astype(BF16), s)
            o = _diff_attention(qkv, a_lambda[l], a_subln[l], b, s, l)
            xt = _proj_residual(o, a_wo[l].astype(BF16), xt, g1, s)
        else:
            if l == n_a:
                kv_sh, kv_sc = jnp.split(kv_mod, 2, axis=-1)
                kv = _norm_linear(xt, kv_norm, kv_sh, kv_sc, kv_w.astype(BF16), s)
            j = l - n_a
            q = _norm_linear(xt, norm_mix[l], sh1, sc1, b_wq[j].astype(BF16), s)
            o = _stick_attention(q, kv, b, s)
            xt = _proj_residual(o, b_wo[j].astype(BF16), xt, g1, s)
        xt = _moe(xt, norm_moe[l], sh2, sc2, g2, router_w[l], router_b[l],
                  w_gate_up, b_gate_up, w_down, b_down, l, s)
    return _rmsnorm(xt, final_norm).reshape(b, s, d)
```

```python
import functools
import math

import jax
import jax.numpy as jnp
import numpy as np
from jax import lax
from jax.experimental import pallas as pl
from jax.experimental.pallas import tpu as pltpu
from jax.experimental.pallas import tpu_sc as plsc

F32 = jnp.float32
BF16 = jnp.bfloat16
EPS = 1e-6
MASK_VALUE = -1e30
TOP_K = 4
SWIGLU_ALPHA = 1.702
SWIGLU_LIMIT = 7.0
LOG2E = math.log2(math.e)
UNDERFLOW_BITS = 160.0
NORM_SLACK = 1.03

LANES = 128
SUBLANES = 8
HEAD_DIM = 64
V7X_VMEM_BYTES = 64 * 2 ** 20
VMEM_LIMIT = V7X_VMEM_BYTES * 7 // 8

TOKEN_TILE = 512
ATTN_TILE = 512
EXPERT_TILE = 512


def _params(*semantics):
    return pltpu.CompilerParams(dimension_semantics=semantics, vmem_limit_bytes=VMEM_LIMIT)


def _largest_tile(n, cap):
    t = min(n, cap) // LANES * LANES
    while n % t:
        t -= LANES
    return t


def _adaln(x, gain, shift, scale):
    ms = jnp.mean(x * x, axis=-1, keepdims=True)
    return x * lax.rsqrt(ms + EPS) * gain * (1.0 + scale) + shift


def _pack_bf16_pair(lo, hi):
    lo_bits = lax.bitcast_convert_type(lo.astype(BF16).astype(F32), jnp.uint32) >> 16
    hi_bits = lax.bitcast_convert_type(hi.astype(BF16).astype(F32), jnp.uint32) & jnp.uint32(0xFFFF0000)
    return lo_bits | hi_bits


def _unpack_bf16_pair(packed):
    lo = lax.bitcast_convert_type(packed << 16, F32)
    hi = lax.bitcast_convert_type(packed & jnp.uint32(0xFFFF0000), F32)
    return lo, hi


def _pack_rows(x):
    q = x.shape[1] // 4
    return (_pack_bf16_pair(x[:, :q], x[:, 2 * q:3 * q]),
            _pack_bf16_pair(x[:, q:2 * q], x[:, 3 * q:]))


def _unpack_rows(p0, p1):
    a0, a2 = _unpack_bf16_pair(p0)
    a1, a3 = _unpack_bf16_pair(p1)
    return a0, a1, a2, a3


def _mod_kernel(c_ref, w_ref, b_ref, o_ref):
    c = c_ref[...]
    a = c / (1.0 + jnp.exp(-c))
    o_ref[...] = jnp.dot(a, w_ref[...], precision=lax.Precision.HIGHEST,
                         preferred_element_type=F32) + b_ref[...]


def _modulation(c_pad, w, b):
    n_l, d, n = w.shape
    tn = _largest_tile(n, 2048)
    return pl.pallas_call(
        _mod_kernel,
        out_shape=jax.ShapeDtypeStruct((n_l, SUBLANES, n), F32),
        grid=(n_l, n // tn),
        in_specs=[
            pl.BlockSpec((SUBLANES, d), lambda l, j: (0, 0)),
            pl.BlockSpec((None, d, tn), lambda l, j: (l, 0, j)),
            pl.BlockSpec((None, 1, tn), lambda l, j: (l, 0, j)),
        ],
        out_specs=pl.BlockSpec((None, SUBLANES, tn), lambda l, j: (l, 0, j)),
        compiler_params=_params("parallel", "parallel"),
        name="modulation",
    )(c_pad, w, b.reshape(n_l, 1, n))


def _norm_linear_kernel(x_ref, g_ref, sh_ref, sc_ref, w_ref, o_ref):
    h = _adaln(x_ref[...], g_ref[...], sh_ref[...], sc_ref[...])
    o_ref[...] = jnp.dot(h.astype(BF16), w_ref[...], preferred_element_type=F32).astype(o_ref.dtype)


def _norm_linear(x, gain, shift, scale, w, seq):
    t, d = x.shape
    n = w.shape[1]
    tm = min(TOKEN_TILE, seq)
    per_seq = seq // tm
    b = shift.shape[0]
    return pl.pallas_call(
        _norm_linear_kernel,
        out_shape=jax.ShapeDtypeStruct((t, n), BF16),
        grid=(t // tm,),
        in_specs=[
            pl.BlockSpec((tm, d), lambda i: (i, 0)),
            pl.BlockSpec((1, d), lambda i: (0, 0)),
            pl.BlockSpec((None, 1, d), lambda i: (i // per_seq, 0, 0)),
            pl.BlockSpec((None, 1, d), lambda i: (i // per_seq, 0, 0)),
            pl.BlockSpec((d, n), lambda i: (0, 0)),
        ],
        out_specs=pl.BlockSpec((tm, n), lambda i: (i, 0)),
        compiler_params=_params("parallel"),
        name="norm_linear",
    )(x, gain.reshape(1, d), shift.reshape(b, 1, d), scale.reshape(b, 1, d), w)


def _proj_residual_kernel(a_ref, w_ref, x_ref, g_ref, o_ref):
    y = jnp.dot(a_ref[...], w_ref[...], preferred_element_type=F32)
    o_ref[...] = x_ref[...] + g_ref[...] * y


def _proj_residual(a, w, x, gate, seq):
    t, d = x.shape
    k = a.shape[1]
    tm = min(TOKEN_TILE, seq)
    per_seq = seq // tm
    b = gate.shape[0]
    return pl.pallas_call(
        _proj_residual_kernel,
        out_shape=jax.ShapeDtypeStruct((t, d), F32),
        grid=(t // tm,),
        in_specs=[
            pl.BlockSpec((tm, k), lambda i: (i, 0)),
            pl.BlockSpec((k, d), lambda i: (0, 0)),
            pl.BlockSpec((tm, d), lambda i: (i, 0)),
            pl.BlockSpec((None, 1, d), lambda i: (i // per_seq, 0, 0)),
        ],
        out_specs=pl.BlockSpec((tm, d), lambda i: (i, 0)),
        compiler_params=_params("parallel"),
        name="proj_residual",
    )(a, w, x, gate.reshape(b, 1, d))


def _scaled_halves(q_ref):
    q = (q_ref[...].astype(F32) * (HEAD_DIM ** -0.5 * LOG2E)).astype(BF16)
    lane = lax.broadcasted_iota(jnp.int32, q.shape, 1)
    zero = jnp.zeros_like(q)
    return jnp.concatenate([jnp.where(lane < HEAD_DIM, q, zero),
                            jnp.where(lane >= HEAD_DIM, q, zero)], axis=0)


def _dot_nt(a, b):
    return lax.dot_general(a, b, (((1,), (1,)), ((), ())), preferred_element_type=F32)


def _tile_row_col(tq, tk):
    row = lax.broadcasted_iota(jnp.int32, (2 * tq, tk), 0)
    row = jnp.where(row >= tq, row - tq, row)
    col = lax.broadcasted_iota(jnp.int32, (2 * tq, tk), 1)
    return row, col


def _lane_chunks(n):
    return [slice(c * LANES, (c + 1) * LANES) for c in range(n // LANES)]


def _diff_attn_kernel(q_ref, k_ref, v_ref, slope_ref, lam_ref, subln_ref, o_ref,
                      m_ref, l_ref, acc_ref, sa_ref, sb_ref, kmax_ref, *, tile, lam_init):
    i = pl.program_id(2)
    qq = _scaled_halves(q_ref)
    slope = slope_ref[...]
    col_i = lax.broadcasted_iota(jnp.int32, (1, tile), 1)
    ones = jnp.ones((tile, LANES), BF16)
    m_ref[...] = jnp.full(m_ref.shape, MASK_VALUE, F32)
    l_ref[...] = jnp.zeros(l_ref.shape, F32)
    acc_ref[...] = jnp.zeros(acc_ref.shape, F32)

    def scores(s_ref, kb):
        k = k_ref[pl.ds(pl.multiple_of(kb * tile, tile), tile), :]
        s_ref[...] = _dot_nt(qq, k) + slope * (col_i + (kb - i) * tile).astype(F32)

    def accumulate(s_ref, kb, diagonal):
        v = v_ref[pl.ds(pl.multiple_of(kb * tile, tile), tile), :]
        if diagonal:
            row, col = _tile_row_col(tile, tile)
            s_ref[...] = jnp.where(col <= row, s_ref[...], MASK_VALUE)
        m_prev = m_ref[...]
        m_new = jnp.maximum(m_prev, jnp.max(s_ref[...], axis=-1, keepdims=True))
        alpha = jnp.exp2(m_prev - m_new)
        p = jnp.concatenate([jnp.exp2(s_ref[:, sl] - m_new).astype(BF16) for sl in _lane_chunks(tile)],
                            axis=1)
        pv = jnp.dot(p, jnp.concatenate([v, ones], axis=1),
                     preferred_element_type=F32)
        acc_ref[...] = alpha * acc_ref[...] + pv[:, :LANES]
        l_ref[...] = alpha * l_ref[...] + pv[:, LANES:]
        m_ref[...] = m_new

    lane = lax.broadcasted_iota(jnp.int32, (1, LANES), 1)
    half_ones = (lax.broadcasted_iota(jnp.int32, (LANES, LANES), 0) < HEAD_DIM) == (
        lax.broadcasted_iota(jnp.int32, (LANES, LANES), 1) < HEAD_DIM)
    half_ones = jnp.where(half_ones, 1.0, 0.0).astype(BF16)

    @pl.when(i == 0)
    def _():
        def chunk_max(c, best):
            k = k_ref[pl.ds(pl.multiple_of(c * tile, tile), tile), :].astype(F32)
            n2 = jnp.dot((k * k).astype(BF16), half_ones, preferred_element_type=F32)
            return jnp.maximum(best, jnp.max(n2, axis=0, keepdims=True))

        best = lax.fori_loop(0, k_ref.shape[0] // tile, chunk_max, jnp.zeros((1, LANES), F32))
        kmax_ref[...] = jnp.broadcast_to(best * NORM_SLACK, kmax_ref.shape)

    qf = qq.astype(F32)
    qn2 = jnp.dot((qf * qf).astype(BF16), jnp.ones((LANES, LANES), BF16), preferred_element_type=F32)
    q_half = jnp.where(lane < HEAD_DIM, jnp.max(qn2[:tile], axis=0, keepdims=True),
                       jnp.max(qn2[tile:], axis=0, keepdims=True))
    raw_bound = jnp.sqrt(jnp.max(q_half * NORM_SLACK * kmax_ref[0:1, :], axis=1, keepdims=True))

    def live(kb):
        far = jnp.zeros((1, LANES), jnp.int32) + ((kb + 1 - i) * tile - 1)
        best_left = raw_bound + slope[:, :LANES] * far.astype(F32)
        lowest_max = jnp.min(m_ref[...], axis=0, keepdims=True)
        return jnp.max(best_left - lowest_max) > -UNDERFLOW_BITS

    scores(sa_ref, i)

    @pl.when(i == 0)
    def _():
        accumulate(sa_ref, 0, True)

    @pl.when(i == 1)
    def _():
        scores(sb_ref, 0)
        accumulate(sa_ref, 1, True)
        accumulate(sb_ref, 0, False)

    @pl.when(i >= 2)
    def _():
        scores(sb_ref, i - 1)
        accumulate(sa_ref, i, True)
        scores(sa_ref, i - 2)
        accumulate(sb_ref, i - 1, False)
        n_pairs = (i - 2) // 2

        def cond(carry):
            j, go = carry
            return jnp.logical_and(j < n_pairs, go)

        def body(carry):
            j, _ = carry
            kb = i - 2 - 2 * j
            scores(sb_ref, kb - 1)
            accumulate(sa_ref, kb, False)
            scores(sa_ref, kb - 2)
            accumulate(sb_ref, kb - 1, False)
            return j + 1, live(kb - 2)

        _, go = lax.while_loop(cond, body, (jnp.int32(0), live(i - 2)))

        @pl.when(jnp.logical_and(go, i % 2 == 0))
        def _():
            accumulate(sa_ref, 0, False)

        @pl.when(jnp.logical_and(go, i % 2 == 1))
        def _():
            scores(sb_ref, 0)
            accumulate(sa_ref, 1, False)
            accumulate(sb_ref, 0, False)

    o_all = acc_ref[...] / l_ref[...]
    lv = lam_ref[...]
    lam = (jnp.exp(jnp.sum(lv[0:1] * lv[1:2], axis=1, keepdims=True))
           - jnp.exp(jnp.sum(lv[2:3] * lv[3:4], axis=1, keepdims=True)) + lam_init)
    o = o_all[:tile] - lam * o_all[tile:]
    ms = jnp.mean(o * o, axis=-1, keepdims=True)
    o = o * lax.rsqrt(ms + EPS) * subln_ref[...] * (1.0 - lam_init)
    o_ref[...] = o.astype(o_ref.dtype)


def _diff_attention(qkv, lam_vecs, subln, batch, seq, layer_idx):
    t = qkv.shape[0]
    d = qkv.shape[1] // 3
    heads = d // LANES
    tile = min(ATTN_TILE, seq)
    nq = seq // tile
    lam_init = 0.8 - 0.6 * math.exp(-0.3 * layer_idx)
    slopes = np.array([2.0 ** (-8.0 * (h + 1) / heads) * LOG2E for h in range(heads)], np.float32)
    slopes = jnp.asarray(np.broadcast_to(slopes[:, None, None], (heads, 1, tile)))
    kernel = functools.partial(_diff_attn_kernel, tile=tile, lam_init=lam_init)
    return pl.pallas_call(
        kernel,
        out_shape=jax.ShapeDtypeStruct((t, d), BF16),
        grid=(batch, heads, nq),
        in_specs=[
            pl.BlockSpec((tile, LANES), lambda b, h, i: (b * nq + i, h)),
            pl.BlockSpec((seq, LANES), lambda b, h, i: (b, heads + h)),
            pl.BlockSpec((seq, LANES), lambda b, h, i: (b, 2 * heads + h)),
            pl.BlockSpec((None, 1, tile), lambda b, h, i: (h, 0, 0)),
            pl.BlockSpec(lam_vecs.shape, lambda b, h, i: (0, 0)),
            pl.BlockSpec((1, LANES), lambda b, h, i: (0, 0)),
        ],
        out_specs=pl.BlockSpec((tile, LANES), lambda b, h, i: (b * nq + i, h)),
        scratch_shapes=[
            pltpu.VMEM((2 * tile, LANES), F32),
            pltpu.VMEM((2 * tile, LANES), F32),
            pltpu.VMEM((2 * tile, LANES), F32),
            pltpu.VMEM((2 * tile, tile), F32),
            pltpu.VMEM((2 * tile, tile), F32),
            pltpu.VMEM((SUBLANES, LANES), F32),
        ],
        compiler_params=_params("parallel", "parallel", "arbitrary"),
        name="diff_attention",
    )(qkv, qkv, qkv, slopes, lam_vecs, subln.reshape(1, LANES))


CHUNK = 256


def _suffix_sum_matrix():
    j = np.arange(CHUNK)
    return jnp.asarray((j[:, None] >= j[None, :]).astype(np.float32), dtype=BF16)


def _stick_attn_kernel(q_ref, k_ref, v_ref, mt_ref, o_ref, rem_ref, acc_ref, za_ref, zb_ref, *, tile):
    i = pl.program_id(2)
    qq = _scaled_halves(q_ref)
    mt = mt_ref[...]
    rem_ref[...] = jnp.zeros(rem_ref.shape, F32)
    acc_ref[...] = jnp.zeros(acc_ref.shape, F32)

    def scores(z_ref, kb):
        z_ref[...] = _dot_nt(qq, k_ref[pl.ds(pl.multiple_of(kb * tile, tile), tile), :])

    def accumulate(z_ref, kb, diagonal):
        v = v_ref[pl.ds(pl.multiple_of(kb * tile, tile), tile), :]
        rem = rem_ref[...]
        chunks = []
        for c in reversed(range(tile // CHUNK)):
            z = z_ref[:, c * CHUNK:(c + 1) * CHUNK]
            softplus = jnp.maximum(z, 0.0) + jnp.log2(1.0 + jnp.exp2(-jnp.abs(z)))
            if diagonal:
                row = lax.broadcasted_iota(jnp.int32, z.shape, 0)
                row = jnp.where(row >= tile, row - tile, row)
                strict = lax.broadcasted_iota(jnp.int32, z.shape, 1) + c * CHUNK < row
                softplus = jnp.where(strict, softplus, 0.0)
            sums = jnp.dot(softplus.astype(BF16), mt, preferred_element_type=F32)
            for h in reversed(range(CHUNK // LANES)):
                sl = slice(h * LANES, (h + 1) * LANES)
                a = jnp.exp2(z_ref[:, c * CHUNK + h * LANES:c * CHUNK + (h + 1) * LANES] - sums[:, sl] - rem)
                if diagonal:
                    a = jnp.where(strict[:, sl], a, 0.0)
                chunks.append(a.astype(BF16))
            rem = rem + jnp.broadcast_to(sums[:, 0:1], rem.shape)
        rem_ref[...] = rem
        acc_ref[...] += jnp.dot(jnp.concatenate(chunks[::-1], axis=1), v, preferred_element_type=F32)

    def live():
        return jnp.min(rem_ref[...]) < UNDERFLOW_BITS

    scores(za_ref, i)

    @pl.when(i == 0)
    def _():
        accumulate(za_ref, 0, True)

    @pl.when(i == 1)
    def _():
        scores(zb_ref, 0)
        accumulate(za_ref, 1, True)
        accumulate(zb_ref, 0, False)

    @pl.when(i >= 2)
    def _():
        scores(zb_ref, i - 1)
        accumulate(za_ref, i, True)
        scores(za_ref, i - 2)
        accumulate(zb_ref, i - 1, False)
        n_pairs = (i - 2) // 2

        def cond(carry):
            j, go = carry
            return jnp.logical_and(j < n_pairs, go)

        def body(carry):
            j, _ = carry
            kb = i - 2 - 2 * j
            scores(zb_ref, kb - 1)
            accumulate(za_ref, kb, False)
            scores(za_ref, kb - 2)
            accumulate(zb_ref, kb - 1, False)
            return j + 1, live()

        _, go = lax.while_loop(cond, body, (jnp.int32(0), live()))

        @pl.when(jnp.logical_and(go, i % 2 == 0))
        def _():
            accumulate(za_ref, 0, False)

        @pl.when(jnp.logical_and(go, i % 2 == 1))
        def _():
            scores(zb_ref, 0)
            accumulate(za_ref, 1, False)
            accumulate(zb_ref, 0, False)

    acc = acc_ref[...]
    lane = lax.broadcasted_iota(jnp.int32, (tile, LANES), 1)
    o_ref[...] = jnp.where(lane < HEAD_DIM, acc[:tile], acc[tile:]).astype(o_ref.dtype)


def _stick_attention(q, kv, batch, seq):
    t, d = q.shape
    pairs = d // LANES
    tile = min(ATTN_TILE, seq)
    nq = seq // tile
    kernel = functools.partial(_stick_attn_kernel, tile=tile)
    return pl.pallas_call(
        kernel,
        out_shape=jax.ShapeDtypeStruct((t, d), BF16),
        grid=(batch, pairs, nq),
        in_specs=[
            pl.BlockSpec((tile, LANES), lambda b, j, i: (b * nq + i, j)),
            pl.BlockSpec((seq, LANES), lambda b, j, i: (b, j)),
            pl.BlockSpec((seq, LANES), lambda b, j, i: (b, pairs + j)),
            pl.BlockSpec((CHUNK, CHUNK), lambda b, j, i: (0, 0)),
        ],
        out_specs=pl.BlockSpec((tile, LANES), lambda b, j, i: (b * nq + i, j)),
        scratch_shapes=[
            pltpu.VMEM((2 * tile, LANES), F32),
            pltpu.VMEM((2 * tile, LANES), F32),
            pltpu.VMEM((2 * tile, tile), F32),
            pltpu.VMEM((2 * tile, tile), F32),
        ],
        compiler_params=_params("parallel", "parallel", "arbitrary"),
        name="stick_attention",
    )(q, kv, kv, _suffix_sum_matrix())


SC_WINDOW = 128


def _sc_gather_rows(tab0, tab1, idx):
    m = idx.shape[0]
    w = tab0.shape[1]
    mesh = plsc.VectorSubcoreMesh(core_axis_name="core", subcore_axis_name="subcore")
    assert m % (SC_WINDOW * mesh.num_cores * mesh.num_subcores) == 0, m
    out = jax.ShapeDtypeStruct((m, w), tab0.dtype)

    @pl.kernel(out_type=(out, out), mesh=mesh)
    def gather(t0_hbm, t1_hbm, i_hbm, o0_hbm, o1_hbm):
        for t_hbm, o_hbm in ((t0_hbm, o0_hbm), (t1_hbm, o1_hbm)):
            def body(i_vmem, o_vmem, t_hbm=t_hbm):
                pltpu.sync_copy(t_hbm.at[i_vmem.at[0]], o_vmem)

            pltpu.emit_pipeline(
                body,
                grid=(m // SC_WINDOW,),
                in_specs=[pl.BlockSpec((1, SC_WINDOW), lambda i: (0, i))],
                out_specs=[pl.BlockSpec((SC_WINDOW, w), lambda i: (i, 0))],
                core_axis_name=("core", "subcore"),
                dimension_semantics=(pltpu.PARALLEL,),
            )(i_hbm, o_hbm)

    return gather(tab0, tab1, idx.reshape(1, m))


def _sc_scatter_rows(tab0, tab1, idx, n_rows):
    m = idx.shape[0]
    t, w = tab0.shape
    mesh = plsc.VectorSubcoreMesh(core_axis_name="core", subcore_axis_name="subcore")
    assert m % (SC_WINDOW * mesh.num_cores * mesh.num_subcores) == 0 and t % SC_WINDOW == 0, (m, t)
    windows_per_pass = t // SC_WINDOW
    out = jax.ShapeDtypeStruct((n_rows, w), tab0.dtype)

    @pl.kernel(out_type=(out, out), mesh=mesh)
    def scatter(t0_hbm, t1_hbm, i_hbm, o0_hbm, o1_hbm):
        for t_hbm, o_hbm in ((t0_hbm, o0_hbm), (t1_hbm, o1_hbm)):
            def body(x_vmem, i_vmem, o_hbm=o_hbm):
                pltpu.sync_copy(x_vmem, o_hbm.at[i_vmem.at[0]])

            pltpu.emit_pipeline(
                body,
                grid=(m // SC_WINDOW,),
                in_specs=[pl.BlockSpec((SC_WINDOW, w), lambda i: (i % windows_per_pass, 0)),
                          pl.BlockSpec((1, SC_WINDOW), lambda i: (0, i))],
                out_specs=[],
                core_axis_name=("core", "subcore"),
                dimension_semantics=(pltpu.PARALLEL,),
            )(t_hbm, i_hbm)

    return scatter(tab0, tab1, idx.reshape(1, m))


def _moe_route_kernel(x_ref, g_ref, sh_ref, sc_ref, rw_ref, rb_ref,
                      h0_ref, h1_ref, idx_ref, rank_ref, gate_ref, count_ref, run_ref):
    @pl.when(pl.program_id(0) == 0)
    def _():
        run_ref[...] = jnp.zeros(run_ref.shape, F32)

    h = _adaln(x_ref[...], g_ref[...], sh_ref[...], sc_ref[...])
    h0_ref[...], h1_ref[...] = _pack_rows(h)
    logits = jnp.dot(h, rw_ref[...], precision=lax.Precision.HIGHEST,
                     preferred_element_type=F32) + rb_ref[...]
    lane = lax.broadcasted_iota(jnp.int32, logits.shape, 1)
    lane_f = lane.astype(F32)
    vals, ids = [], []
    for _ in range(TOP_K):
        m = jnp.max(logits, axis=-1, keepdims=True)
        first = jnp.min(jnp.where(logits == m, lane_f, float(LANES)), axis=-1, keepdims=True)
        vals.append(m)
        ids.append(first)
        logits = jnp.where(lane_f == first, -jnp.inf, logits)
    exps = [jnp.exp(v - vals[0]) for v in vals]
    denom = exps[0]
    for e in exps[1:]:
        denom = denom + e
    tm = logits.shape[0]
    earlier = (lax.broadcasted_iota(jnp.int32, (tm, tm), 1) < lax.broadcasted_iota(jnp.int32, (tm, tm), 0))
    earlier = jnp.where(earlier, 1.0, 0.0).astype(BF16)
    seen = run_ref[...]
    gate = jnp.zeros(logits.shape, F32)
    idx = jnp.zeros(logits.shape, F32)
    rank = jnp.zeros(logits.shape, F32)
    for k in range(TOP_K):
        onehot = jnp.where(lane_f == ids[k], 1.0, 0.0)
        before = jnp.dot(earlier, onehot.astype(BF16), preferred_element_type=F32) + seen
        rank = jnp.where(lane == k, jnp.sum(onehot * before, axis=-1, keepdims=True), rank)
        seen = seen + jnp.sum(onehot, axis=0, keepdims=True)
        gate = jnp.where(lane == k, exps[k] / denom, gate)
        idx = jnp.where(lane == k, ids[k], idx)
    run_ref[...] = seen
    count_ref[...] = jnp.broadcast_to(seen, count_ref.shape).astype(jnp.int32)
    gate_ref[...] = gate
    idx_ref[...] = idx.T[:SUBLANES].astype(jnp.int32)
    rank_ref[...] = rank.T[:SUBLANES].astype(jnp.int32)


def _moe_route(x, gain, shift, scale, rw_pad, rb_pad, seq):
    t, d = x.shape
    tm = min(TOKEN_TILE, seq)
    per_seq = seq // tm
    b = shift.shape[0]
    return pl.pallas_call(
        _moe_route_kernel,
        out_shape=(jax.ShapeDtypeStruct((t, d // 4), jnp.uint32),
                   jax.ShapeDtypeStruct((t, d // 4), jnp.uint32),
                   jax.ShapeDtypeStruct((SUBLANES, t), jnp.int32),
                   jax.ShapeDtypeStruct((SUBLANES, t), jnp.int32),
                   jax.ShapeDtypeStruct((t, LANES), F32),
                   jax.ShapeDtypeStruct((SUBLANES, LANES), jnp.int32)),
        grid=(t // tm,),
        in_specs=[
            pl.BlockSpec((tm, d), lambda i: (i, 0)),
            pl.BlockSpec((1, d), lambda i: (0, 0)),
            pl.BlockSpec((None, 1, d), lambda i: (i // per_seq, 0, 0)),
            pl.BlockSpec((None, 1, d), lambda i: (i // per_seq, 0, 0)),
            pl.BlockSpec((d, LANES), lambda i: (0, 0)),
            pl.BlockSpec((1, LANES), lambda i: (0, 0)),
        ],
        out_specs=(pl.BlockSpec((tm, d // 4), lambda i: (i, 0)),
                   pl.BlockSpec((tm, d // 4), lambda i: (i, 0)),
                   pl.BlockSpec((SUBLANES, tm), lambda i: (0, i)),
                   pl.BlockSpec((SUBLANES, tm), lambda i: (0, i)),
                   pl.BlockSpec((tm, LANES), lambda i: (i, 0)),
                   pl.BlockSpec((SUBLANES, LANES), lambda i: (0, 0))),
        scratch_shapes=[pltpu.VMEM((1, LANES), F32)],
        compiler_params=_params("arbitrary"),
        name="moe_route",
    )(x, gain.reshape(1, d), shift.reshape(b, 1, d), scale.reshape(b, 1, d), rw_pad, rb_pad)


def _expert_kernel(block_e_ref, n_used_ref, n_valid_ref, h0_ref, h1_ref, wgu_ref, bgu_ref, wd_ref, bd_ref,
                   o0_ref, o1_ref):
    i = pl.program_id(0)

    @pl.when(i < n_used_ref[0])
    def _():
        d_e = wd_ref.shape[0]
        written = lax.broadcasted_iota(jnp.int32, h0_ref.shape, 0) < n_valid_ref[i]
        zero = jnp.zeros(h0_ref.shape, h0_ref.dtype)
        quarters = _unpack_rows(jnp.where(written, h0_ref[...], zero), jnp.where(written, h1_ref[...], zero))
        h = jnp.concatenate([q.astype(BF16) for q in quarters], axis=1)
        gu = jnp.dot(h, wgu_ref[...].astype(BF16), preferred_element_type=F32) + bgu_ref[...]
        g = jnp.minimum(gu[:, :d_e], SWIGLU_LIMIT)
        u = jnp.clip(gu[:, d_e:], -SWIGLU_LIMIT, SWIGLU_LIMIT)
        act = g / (1.0 + jnp.exp(-SWIGLU_ALPHA * g)) * (u + 1.0)
        y = jnp.dot(act.astype(BF16), wd_ref[...].astype(BF16), preferred_element_type=F32) + bd_ref[...]
        o0_ref[...], o1_ref[...] = _pack_rows(y)

    @pl.when(i >= n_used_ref[0])
    def _():
        o0_ref[...] = jnp.zeros(o0_ref.shape, o0_ref.dtype)
        o1_ref[...] = jnp.zeros(o1_ref.shape, o1_ref.dtype)


def _expert_mlp(block_e, n_used, n_valid, rows0, rows1, wgu, bgu, wd, bd, layer, tile):
    n_rows, dq = rows0.shape
    d = 4 * dq
    n_l, n_e, _, two_de = wgu.shape
    d_e = two_de // 2
    n_blocks = n_rows // tile
    grid_spec = pltpu.PrefetchScalarGridSpec(
        num_scalar_prefetch=3,
        grid=(n_blocks,),
        in_specs=[
            pl.BlockSpec((tile, dq), lambda i, be, nu, nv: (i, 0)),
            pl.BlockSpec((tile, dq), lambda i, be, nu, nv: (i, 0)),
            pl.BlockSpec((None, None, d, two_de), lambda i, be, nu, nv: (layer, be[i], 0, 0)),
            pl.BlockSpec((None, None, 1, two_de), lambda i, be, nu, nv: (layer, be[i], 0, 0)),
            pl.BlockSpec((None, None, d_e, d), lambda i, be, nu, nv: (layer, be[i], 0, 0)),
            pl.BlockSpec((None, None, 1, d), lambda i, be, nu, nv: (layer, be[i], 0, 0)),
        ],
        out_specs=(pl.BlockSpec((tile, dq), lambda i, be, nu, nv: (i, 0)),
                   pl.BlockSpec((tile, dq), lambda i, be, nu, nv: (i, 0))),
    )
    packed = jax.ShapeDtypeStruct((n_rows, dq), jnp.uint32)
    return pl.pallas_call(
        _expert_kernel,
        out_shape=(packed, packed),
        grid_spec=grid_spec,
        compiler_params=_params("arbitrary"),
        name="expert_mlp",
    )(block_e, n_used, n_valid, rows0, rows1, wgu, bgu.reshape(n_l, n_e, 1, two_de), wd,
      bd.reshape(n_l, n_e, 1, d))


def _combine_kernel(y0_ref, y1_ref, w_ref, x_ref, g_ref, o_ref):
    w = w_ref[...]
    acc = None
    for k in range(y0_ref.shape[0]):
        quarters = _unpack_rows(y0_ref[k], y1_ref[k])
        scaled = [w[:, k:k + 1] * q for q in quarters]
        acc = scaled if acc is None else [a + s for a, s in zip(acc, scaled)]
    o_ref[...] = x_ref[...] + g_ref[...] * jnp.concatenate(acc, axis=1)


def _combine(y0, y1, weights, x, gate, seq):
    t, d = x.shape
    tm = min(TOKEN_TILE, seq)
    per_seq = seq // tm
    b = gate.shape[0]
    return pl.pallas_call(
        _combine_kernel,
        out_shape=jax.ShapeDtypeStruct((t, d), F32),
        grid=(t // tm,),
        in_specs=[
            pl.BlockSpec((y0.shape[0], tm, d // 4), lambda i: (0, i, 0)),
            pl.BlockSpec((y0.shape[0], tm, d // 4), lambda i: (0, i, 0)),
            pl.BlockSpec((tm, LANES), lambda i: (i, 0)),
            pl.BlockSpec((tm, d), lambda i: (i, 0)),
            pl.BlockSpec((None, 1, d), lambda i: (i // per_seq, 0, 0)),
        ],
        out_specs=pl.BlockSpec((tm, d), lambda i: (i, 0)),
        compiler_params=_params("parallel"),
        name="moe_combine",
    )(y0, y1, weights, x, gate.reshape(b, 1, d))


ROW_BLOCK_MULTIPLE = 8


def _routing_tables(idx_t, rank_t, counts, tile):
    top_k, t = idx_t.shape
    n_experts = counts.shape[0]
    n_assign = top_k * t
    padded = (counts + tile - 1) // tile * tile
    pend = jnp.cumsum(padded)
    pstart = pend - padded
    e_flat = idx_t.reshape(n_assign)
    onehot = e_flat[:, None] == jnp.arange(n_experts, dtype=jnp.int32)[None, :]
    slot_row = rank_t.reshape(n_assign) + jnp.sum(jnp.where(onehot, pstart[None, :], 0), axis=1,
                                                  dtype=jnp.int32)
    n_blocks = (n_assign + n_experts * (tile - 1) + tile - 1) // tile
    n_blocks = -(-n_blocks // ROW_BLOCK_MULTIPLE) * ROW_BLOCK_MULTIPLE
    first_row = jnp.arange(n_blocks, dtype=jnp.int32) * tile
    block_e = jnp.minimum(jnp.sum(pend[None, :] <= first_row[:, None], axis=1), n_experts - 1)
    block_e = block_e.astype(jnp.int32)
    n_valid = jnp.clip(counts[block_e] - (first_row - pstart[block_e]), 0, tile).astype(jnp.int32)
    n_used = (pend[-1] // tile).astype(jnp.int32).reshape(1)
    return slot_row, block_e, n_valid, n_used, n_blocks * tile


def _moe(x, gain, shift, scale, gate, rw, rb, wgu, bgu, wd, bd, layer, seq):
    t, d = x.shape
    n_e = rw.shape[1]
    rw_pad = jnp.pad(rw, ((0, 0), (0, LANES - n_e)))
    rb_pad = jnp.pad(rb, (0, LANES - n_e), constant_values=MASK_VALUE).reshape(1, LANES)
    h0, h1, idx_t, rank_t, weights, counts = _moe_route(x, gain, shift, scale, rw_pad, rb_pad, seq)
    tile = min(EXPERT_TILE, t * TOP_K // n_e)
    slot_row, block_e, n_valid, n_used, n_rows = _routing_tables(
        idx_t[:TOP_K], rank_t[:TOP_K], counts[0, :n_e], tile)
    rows0, rows1 = _sc_scatter_rows(h0, h1, slot_row, n_rows)
    ys0, ys1 = _expert_mlp(block_e, n_used, n_valid, rows0, rows1, wgu, bgu, wd, bd, layer, tile)
    y0, y1 = _sc_gather_rows(ys0, ys1, slot_row)
    return _combine(y0.reshape(TOP_K, t, d // 4), y1.reshape(TOP_K, t, d // 4), weights, x, gate, seq)


def _rmsnorm_kernel(x_ref, g_ref, o_ref):
    x = x_ref[...]
    ms = jnp.mean(x * x, axis=-1, keepdims=True)
    o_ref[...] = x * lax.rsqrt(ms + EPS) * g_ref[...]


def _rmsnorm(x, gain):
    t, d = x.shape
    tm = min(TOKEN_TILE, t)
    return pl.pallas_call(
        _rmsnorm_kernel,
        out_shape=jax.ShapeDtypeStruct((t, d), F32),
        grid=(t // tm,),
        in_specs=[pl.BlockSpec((tm, d), lambda i: (i, 0)),
                  pl.BlockSpec((1, d), lambda i: (0, 0))],
        out_specs=pl.BlockSpec((tm, d), lambda i: (i, 0)),
        compiler_params=_params("parallel"),
        name="final_rmsnorm",
    )(x, gain.reshape(1, d))


def kernel(x, c, ada_w, ada_b, norm_mix, norm_moe, a_wqkv, a_wo, a_lambda, a_subln, kv_norm,
           kv_ada_w, kv_ada_b, kv_w, b_wq, b_wo, router_w, router_b, w_gate_up, b_gate_up,
           w_down, b_down, final_norm):
    b, s, d = x.shape
    depth = ada_w.shape[0]
    n_a = a_wqkv.shape[0]
    xt = x.reshape(b * s, d)

    c_pad = jnp.pad(c, ((0, SUBLANES - b), (0, 0)))
    mod = _modulation(c_pad, ada_w, ada_b)[:, :b]
    kv_mod = _modulation(c_pad, kv_ada_w[None], kv_ada_b[None])[0, :b]

    for l in range(depth):
        sh1, sc1, g1, sh2, sc2, g2 = jnp.split(mod[l], 6, axis=-1)
        if l < n_a:
            qkv = _norm_linear(xt, norm_mix[l], sh1, sc1, a_wqkv[l].astype(BF16), s)
            o = _diff_attention(qkv, a_lambda[l], a_subln[l], b, s, l)
            xt = _proj_residual(o, a_wo[l].astype(BF16), xt, g1, s)
        else:
            if l == n_a:
                kv_sh, kv_sc = jnp.split(kv_mod, 2, axis=-1)
                kv = _norm_linear(xt, kv_norm, kv_sh, kv_sc, kv_w.astype(BF16), s)
            j = l - n_a
            q = _norm_linear(xt, norm_mix[l], sh1, sc1, b_wq[j].astype(BF16), s)
            o = _stick_attention(q, kv, b, s)
            xt = _proj_residual(o, b_wo[j].astype(BF16), xt, g1, s)
        xt = _moe(xt, norm_moe[l], sh2, sc2, g2, router_w[l], router_b[l],
                  w_gate_up, b_gate_up, w_down, b_down, l, s)
    return _rmsnorm(xt, final_norm).reshape(b, s, d)
```

```python
import functools
import math

import jax
import jax.numpy as jnp
import numpy as np
from jax import lax
from jax.experimental import pallas as pl
from jax.experimental.pallas import tpu as pltpu
from jax.experimental.pallas import tpu_sc as plsc

F32 = jnp.float32
BF16 = jnp.bfloat16
EPS = 1e-6
MASK_VALUE = -1e30
TOP_K = 4
SWIGLU_ALPHA = 1.702
SWIGLU_LIMIT = 7.0
LOG2E = math.log2(math.e)
UNDERFLOW_BITS = 160.0
NORM_SLACK = 1.03

LANES = 128
SUBLANES = 8
HEAD_DIM = 64
V7X_VMEM_BYTES = 64 * 2 ** 20
VMEM_LIMIT = V7X_VMEM_BYTES * 7 // 8

TOKEN_TILE = 512
ATTN_TILE = 512
STICK_TILE = 256
EXPERT_TILE = 512


def _params(*semantics):
    return pltpu.CompilerParams(dimension_semantics=semantics, vmem_limit_bytes=VMEM_LIMIT)


def _largest_tile(n, cap):
    t = min(n, cap) // LANES * LANES
    while n % t:
        t -= LANES
    return t


def _adaln(x, gain, shift, scale):
    ms = jnp.mean(x * x, axis=-1, keepdims=True)
    return x * lax.rsqrt(ms + EPS) * gain * (1.0 + scale) + shift


def _pack_bf16_pair(lo, hi):
    lo_bits = lax.bitcast_convert_type(lo.astype(BF16).astype(F32), jnp.uint32) >> 16
    hi_bits = lax.bitcast_convert_type(hi.astype(BF16).astype(F32), jnp.uint32) & jnp.uint32(0xFFFF0000)
    return lo_bits | hi_bits


def _unpack_bf16_pair(packed):
    lo = lax.bitcast_convert_type(packed << 16, F32)
    hi = lax.bitcast_convert_type(packed & jnp.uint32(0xFFFF0000), F32)
    return lo, hi


def _pack_rows(x):
    q = x.shape[1] // 4
    return (_pack_bf16_pair(x[:, :q], x[:, 2 * q:3 * q]),
            _pack_bf16_pair(x[:, q:2 * q], x[:, 3 * q:]))


def _unpack_rows(p0, p1):
    a0, a2 = _unpack_bf16_pair(p0)
    a1, a3 = _unpack_bf16_pair(p1)
    return a0, a1, a2, a3


def _mod_kernel(c_ref, w_ref, b_ref, o_ref):
    c = c_ref[...]
    a = c / (1.0 + jnp.exp(-c))
    o_ref[...] = jnp.dot(a, w_ref[...], precision=lax.Precision.HIGHEST,
                         preferred_element_type=F32) + b_ref[...]


def _modulation(c_pad, w, b):
    n_l, d, n = w.shape
    tn = _largest_tile(n, 2048)
    return pl.pallas_call(
        _mod_kernel,
        out_shape=jax.ShapeDtypeStruct((n_l, SUBLANES, n), F32),
        grid=(n_l, n // tn),
        in_specs=[
            pl.BlockSpec((SUBLANES, d), lambda l, j: (0, 0)),
            pl.BlockSpec((None, d, tn), lambda l, j: (l, 0, j)),
            pl.BlockSpec((None, 1, tn), lambda l, j: (l, 0, j)),
        ],
        out_specs=pl.BlockSpec((None, SUBLANES, tn), lambda l, j: (l, 0, j)),
        compiler_params=_params("parallel", "parallel"),
        name="modulation",
    )(c_pad, w, b.reshape(n_l, 1, n))


def _norm_linear_kernel(x_ref, g_ref, sh_ref, sc_ref, w_ref, o_ref):
    h = _adaln(x_ref[...], g_ref[...], sh_ref[...], sc_ref[...])
    o_ref[...] = jnp.dot(h.astype(BF16), w_ref[...], preferred_element_type=F32).astype(o_ref.dtype)


def _norm_linear(x, gain, shift, scale, w, seq):
    t, d = x.shape
    n = w.shape[1]
    tm = min(TOKEN_TILE, seq)
    per_seq = seq // tm
    b = shift.shape[0]
    return pl.pallas_call(
        _norm_linear_kernel,
        out_shape=jax.ShapeDtypeStruct((t, n), BF16),
        grid=(t // tm,),
        in_specs=[
            pl.BlockSpec((tm, d), lambda i: (i, 0)),
            pl.BlockSpec((1, d), lambda i: (0, 0)),
            pl.BlockSpec((None, 1, d), lambda i: (i // per_seq, 0, 0)),
            pl.BlockSpec((None, 1, d), lambda i: (i // per_seq, 0, 0)),
            pl.BlockSpec((d, n), lambda i: (0, 0)),
        ],
        out_specs=pl.BlockSpec((tm, n), lambda i: (i, 0)),
        compiler_params=_params("parallel"),
        name="norm_linear",
    )(x, gain.reshape(1, d), shift.reshape(b, 1, d), scale.reshape(b, 1, d), w)


def _proj_residual_kernel(a_ref, w_ref, x_ref, g_ref, o_ref):
    y = jnp.dot(a_ref[...], w_ref[...], preferred_element_type=F32)
    o_ref[...] = x_ref[...] + g_ref[...] * y


def _proj_residual(a, w, x, gate, seq):
    t, d = x.shape
    k = a.shape[1]
    tm = min(TOKEN_TILE, seq)
    per_seq = seq // tm
    b = gate.shape[0]
    return pl.pallas_call(
        _proj_residual_kernel,
        out_shape=jax.ShapeDtypeStruct((t, d), F32),
        grid=(t // tm,),
        in_specs=[
            pl.BlockSpec((tm, k), lambda i: (i, 0)),
            pl.BlockSpec((k, d), lambda i: (0, 0)),
            pl.BlockSpec((tm, d), lambda i: (i, 0)),
            pl.BlockSpec((None, 1, d), lambda i: (i // per_seq, 0, 0)),
        ],
        out_specs=pl.BlockSpec((tm, d), lambda i: (i, 0)),
        compiler_params=_params("parallel"),
        name="proj_residual",
    )(a, w, x, gate.reshape(b, 1, d))


def _scaled_halves(q_ref):
    q = (q_ref[...].astype(F32) * (HEAD_DIM ** -0.5 * LOG2E)).astype(BF16)
    lane = lax.broadcasted_iota(jnp.int32, q.shape, 1)
    zero = jnp.zeros_like(q)
    return jnp.concatenate([jnp.where(lane < HEAD_DIM, q, zero),
                            jnp.where(lane >= HEAD_DIM, q, zero)], axis=0)


def _dot_nt(a, b):
    return lax.dot_general(a, b, (((1,), (1,)), ((), ())), preferred_element_type=F32)


def _tile_row_col(tq, tk):
    row = lax.broadcasted_iota(jnp.int32, (2 * tq, tk), 0)
    row = jnp.where(row >= tq, row - tq, row)
    col = lax.broadcasted_iota(jnp.int32, (2 * tq, tk), 1)
    return row, col


def _lane_chunks(n):
    return [slice(c * LANES, (c + 1) * LANES) for c in range(n // LANES)]


def _diff_attn_kernel(q_ref, k_ref, v_ref, slope_ref, lam_ref, subln_ref, o_ref,
                      m_ref, l_ref, acc_ref, sa_ref, sb_ref, kmax_ref, *, tile, lam_init):
    i = pl.program_id(2)
    qq = _scaled_halves(q_ref)
    slope = slope_ref[...]
    col_i = lax.broadcasted_iota(jnp.int32, (1, tile), 1)
    ones = jnp.ones((tile, LANES), BF16)
    m_ref[...] = jnp.full(m_ref.shape, MASK_VALUE, F32)
    l_ref[...] = jnp.zeros(l_ref.shape, F32)
    acc_ref[...] = jnp.zeros(acc_ref.shape, F32)

    def scores(s_ref, kb):
        k = k_ref[pl.ds(pl.multiple_of(kb * tile, tile), tile), :]
        s_ref[...] = _dot_nt(qq, k) + slope * (col_i + (kb - i) * tile).astype(F32)

    def accumulate(s_ref, kb, diagonal):
        v = v_ref[pl.ds(pl.multiple_of(kb * tile, tile), tile), :]
        if diagonal:
            row, col = _tile_row_col(tile, tile)
            s_ref[...] = jnp.where(col <= row, s_ref[...], MASK_VALUE)
        m_prev = m_ref[...]
        m_new = jnp.maximum(m_prev, jnp.max(s_ref[...], axis=-1, keepdims=True))
        alpha = jnp.exp2(m_prev - m_new)
        p = jnp.concatenate([jnp.exp2(s_ref[:, sl] - m_new).astype(BF16) for sl in _lane_chunks(tile)],
                            axis=1)
        pv = jnp.dot(p, jnp.concatenate([v, ones], axis=1),
                     preferred_element_type=F32)
        acc_ref[...] = alpha * acc_ref[...] + pv[:, :LANES]
        l_ref[...] = alpha * l_ref[...] + pv[:, LANES:]
        m_ref[...] = m_new

    lane = lax.broadcasted_iota(jnp.int32, (1, LANES), 1)
    half_ones = (lax.broadcasted_iota(jnp.int32, (LANES, LANES), 0) < HEAD_DIM) == (
        lax.broadcasted_iota(jnp.int32, (LANES, LANES), 1) < HEAD_DIM)
    half_ones = jnp.where(half_ones, 1.0, 0.0).astype(BF16)

    @pl.when(i == 0)
    def _():
        def chunk_max(c, best):
            k = k_ref[pl.ds(pl.multiple_of(c * tile, tile), tile), :].astype(F32)
            n2 = jnp.dot((k * k).astype(BF16), half_ones, preferred_element_type=F32)
            return jnp.maximum(best, jnp.max(n2, axis=0, keepdims=True))

        best = lax.fori_loop(0, k_ref.shape[0] // tile, chunk_max, jnp.zeros((1, LANES), F32))
        kmax_ref[...] = jnp.broadcast_to(best * NORM_SLACK, kmax_ref.shape)

    qf = qq.astype(F32)
    qn2 = jnp.dot((qf * qf).astype(BF16), jnp.ones((LANES, LANES), BF16), preferred_element_type=F32)
    q_half = jnp.where(lane < HEAD_DIM, jnp.max(qn2[:tile], axis=0, keepdims=True),
                       jnp.max(qn2[tile:], axis=0, keepdims=True))
    raw_bound = jnp.sqrt(jnp.max(q_half * NORM_SLACK * kmax_ref[0:1, :], axis=1, keepdims=True))

    def live(kb):
        far = jnp.zeros((1, LANES), jnp.int32) + ((kb + 1 - i) * tile - 1)
        best_left = raw_bound + slope[:, :LANES] * far.astype(F32)
        lowest_max = jnp.min(m_ref[...], axis=0, keepdims=True)
        return jnp.max(best_left - lowest_max) > -UNDERFLOW_BITS

    scores(sa_ref, i)

    @pl.when(i == 0)
    def _():
        accumulate(sa_ref, 0, True)

    @pl.when(i == 1)
    def _():
        scores(sb_ref, 0)
        accumulate(sa_ref, 1, True)
        accumulate(sb_ref, 0, False)

    @pl.when(i >= 2)
    def _():
        scores(sb_ref, i - 1)
        accumulate(sa_ref, i, True)
        scores(sa_ref, i - 2)
        accumulate(sb_ref, i - 1, False)
        n_pairs = (i - 2) // 2

        def cond(carry):
            j, go = carry
            return jnp.logical_and(j < n_pairs, go)

        def body(carry):
            j, _ = carry
            kb = i - 2 - 2 * j
            scores(sb_ref, kb - 1)
            accumulate(sa_ref, kb, False)
            scores(sa_ref, kb - 2)
            accumulate(sb_ref, kb - 1, False)
            return j + 1, live(kb - 2)

        _, go = lax.while_loop(cond, body, (jnp.int32(0), live(i - 2)))

        @pl.when(jnp.logical_and(go, i % 2 == 0))
        def _():
            accumulate(sa_ref, 0, False)

        @pl.when(jnp.logical_and(go, i % 2 == 1))
        def _():
            scores(sb_ref, 0)
            accumulate(sa_ref, 1, False)
            accumulate(sb_ref, 0, False)

    o_all = acc_ref[...] / l_ref[...]
    lv = lam_ref[...]
    lam = (jnp.exp(jnp.sum(lv[0:1] * lv[1:2], axis=1, keepdims=True))
           - jnp.exp(jnp.sum(lv[2:3] * lv[3:4], axis=1, keepdims=True)) + lam_init)
    o = o_all[:tile] - lam * o_all[tile:]
    ms = jnp.mean(o * o, axis=-1, keepdims=True)
    o = o * lax.rsqrt(ms + EPS) * subln_ref[...] * (1.0 - lam_init)
    o_ref[...] = o.astype(o_ref.dtype)


def _diff_attention(qkv, lam_vecs, subln, batch, seq, layer_idx):
    t = qkv.shape[0]
    d = qkv.shape[1] // 3
    heads = d // LANES
    tile = min(ATTN_TILE, seq)
    nq = seq // tile
    lam_init = 0.8 - 0.6 * math.exp(-0.3 * layer_idx)
    slopes = np.array([2.0 ** (-8.0 * (h + 1) / heads) * LOG2E for h in range(heads)], np.float32)
    slopes = jnp.asarray(np.broadcast_to(slopes[:, None, None], (heads, 1, tile)))
    kernel = functools.partial(_diff_attn_kernel, tile=tile, lam_init=lam_init)
    return pl.pallas_call(
        kernel,
        out_shape=jax.ShapeDtypeStruct((t, d), BF16),
        grid=(batch, heads, nq),
        in_specs=[
            pl.BlockSpec((tile, LANES), lambda b, h, i: (b * nq + i, h)),
            pl.BlockSpec((seq, LANES), lambda b, h, i: (b, heads + h)),
            pl.BlockSpec((seq, LANES), lambda b, h, i: (b, 2 * heads + h)),
            pl.BlockSpec((None, 1, tile), lambda b, h, i: (h, 0, 0)),
            pl.BlockSpec(lam_vecs.shape, lambda b, h, i: (0, 0)),
            pl.BlockSpec((1, LANES), lambda b, h, i: (0, 0)),
        ],
        out_specs=pl.BlockSpec((tile, LANES), lambda b, h, i: (b * nq + i, h)),
        scratch_shapes=[
            pltpu.VMEM((2 * tile, LANES), F32),
            pltpu.VMEM((2 * tile, LANES), F32),
            pltpu.VMEM((2 * tile, LANES), F32),
            pltpu.VMEM((2 * tile, tile), F32),
            pltpu.VMEM((2 * tile, tile), F32),
            pltpu.VMEM((SUBLANES, LANES), F32),
        ],
        compiler_params=_params("parallel", "parallel", "arbitrary"),
        name="diff_attention",
    )(qkv, qkv, qkv, slopes, lam_vecs, subln.reshape(1, LANES))


CHUNK = 256


def _suffix_sum_matrix():
    j = np.arange(CHUNK)
    return jnp.asarray((j[:, None] >= j[None, :]).astype(np.float32), dtype=BF16)


def _stick_attn_kernel(q_ref, k_ref, v_ref, mt_ref, o_ref, rem_ref, acc_ref, za_ref, zb_ref, *, tile):
    i = pl.program_id(2)
    qq = _scaled_halves(q_ref)
    mt = mt_ref[...]
    rem_ref[...] = jnp.zeros(rem_ref.shape, F32)
    acc_ref[...] = jnp.zeros(acc_ref.shape, F32)

    def scores(z_ref, kb):
        z_ref[...] = _dot_nt(qq, k_ref[pl.ds(pl.multiple_of(kb * tile, tile), tile), :])

    def accumulate(z_ref, kb, diagonal):
        v = v_ref[pl.ds(pl.multiple_of(kb * tile, tile), tile), :]
        rem = rem_ref[...]
        chunks = []
        for c in reversed(range(tile // CHUNK)):
            z = z_ref[:, c * CHUNK:(c + 1) * CHUNK]
            softplus = jnp.maximum(z, 0.0) + jnp.log2(1.0 + jnp.exp2(-jnp.abs(z)))
            if diagonal:
                row = lax.broadcasted_iota(jnp.int32, z.shape, 0)
                row = jnp.where(row >= tile, row - tile, row)
                strict = lax.broadcasted_iota(jnp.int32, z.shape, 1) + c * CHUNK < row
                softplus = jnp.where(strict, softplus, 0.0)
            sums = jnp.dot(softplus.astype(BF16), mt, preferred_element_type=F32)
            for h in reversed(range(CHUNK // LANES)):
                sl = slice(h * LANES, (h + 1) * LANES)
                a = jnp.exp2(z_ref[:, c * CHUNK + h * LANES:c * CHUNK + (h + 1) * LANES] - sums[:, sl] - rem)
                if diagonal:
                    a = jnp.where(strict[:, sl], a, 0.0)
                chunks.append(a.astype(BF16))
            rem = rem + jnp.broadcast_to(sums[:, 0:1], rem.shape)
        rem_ref[...] = rem
        acc_ref[...] += jnp.dot(jnp.concatenate(chunks[::-1], axis=1), v, preferred_element_type=F32)

    def live():
        return jnp.min(rem_ref[...]) < UNDERFLOW_BITS

    scores(za_ref, i)

    @pl.when(i == 0)
    def _():
        accumulate(za_ref, 0, True)

    @pl.when(i == 1)
    def _():
        scores(zb_ref, 0)
        accumulate(za_ref, 1, True)
        accumulate(zb_ref, 0, False)

    @pl.when(i >= 2)
    def _():
        scores(zb_ref, i - 1)
        accumulate(za_ref, i, True)
        scores(za_ref, i - 2)
        accumulate(zb_ref, i - 1, False)
        n_pairs = (i - 2) // 2

        def cond(carry):
            j, go = carry
            return jnp.logical_and(j < n_pairs, go)

        def body(carry):
            j, _ = carry
            kb = i - 2 - 2 * j
            scores(zb_ref, kb - 1)
            accumulate(za_ref, kb, False)
            scores(za_ref, kb - 2)
            accumulate(zb_ref, kb - 1, False)
            return j + 1, live()

        _, go = lax.while_loop(cond, body, (jnp.int32(0), live()))

        @pl.when(jnp.logical_and(go, i % 2 == 0))
        def _():
            accumulate(za_ref, 0, False)

        @pl.when(jnp.logical_and(go, i % 2 == 1))
        def _():
            scores(zb_ref, 0)
            accumulate(za_ref, 1, False)
            accumulate(zb_ref, 0, False)

    acc = acc_ref[...]
    lane = lax.broadcasted_iota(jnp.int32, (tile, LANES), 1)
    o_ref[...] = jnp.where(lane < HEAD_DIM, acc[:tile], acc[tile:]).astype(o_ref.dtype)


def _stick_attention(q, kv, batch, seq):
    t, d = q.shape
    pairs = d // LANES
    tile = min(STICK_TILE, seq)
    nq = seq // tile
    kernel = functools.partial(_stick_attn_kernel, tile=tile)
    return pl.pallas_call(
        kernel,
        out_shape=jax.ShapeDtypeStruct((t, d), BF16),
        grid=(batch, pairs, nq),
        in_specs=[
            pl.BlockSpec((tile, LANES), lambda b, j, i: (b * nq + i, j)),
            pl.BlockSpec((seq, LANES), lambda b, j, i: (b, j)),
            pl.BlockSpec((seq, LANES), lambda b, j, i: (b, pairs + j)),
            pl.BlockSpec((CHUNK, CHUNK), lambda b, j, i: (0, 0)),
        ],
        out_specs=pl.BlockSpec((tile, LANES), lambda b, j, i: (b * nq + i, j)),
        scratch_shapes=[
            pltpu.VMEM((2 * tile, LANES), F32),
            pltpu.VMEM((2 * tile, LANES), F32),
            pltpu.VMEM((2 * tile, tile), F32),
            pltpu.VMEM((2 * tile, tile), F32),
        ],
        compiler_params=_params("parallel", "parallel", "arbitrary"),
        name="stick_attention",
    )(q, kv, kv, _suffix_sum_matrix())


SC_WINDOW = 128


def _sc_gather_rows(tab0, tab1, idx):
    m = idx.shape[0]
    w = tab0.shape[1]
    mesh = plsc.VectorSubcoreMesh(core_axis_name="core", subcore_axis_name="subcore")
    assert m % (SC_WINDOW * mesh.num_cores * mesh.num_subcores) == 0, m
    out = jax.ShapeDtypeStruct((m, w), tab0.dtype)

    @pl.kernel(out_type=(out, out), mesh=mesh)
    def gather(t0_hbm, t1_hbm, i_hbm, o0_hbm, o1_hbm):
        for t_hbm, o_hbm in ((t0_hbm, o0_hbm), (t1_hbm, o1_hbm)):
            def body(i_vmem, o_vmem, t_hbm=t_hbm):
                pltpu.sync_copy(t_hbm.at[i_vmem.at[0]], o_vmem)

            pltpu.emit_pipeline(
                body,
                grid=(m // SC_WINDOW,),
                in_specs=[pl.BlockSpec((1, SC_WINDOW), lambda i: (0, i))],
                out_specs=[pl.BlockSpec((SC_WINDOW, w), lambda i: (i, 0))],
                core_axis_name=("core", "subcore"),
                dimension_semantics=(pltpu.PARALLEL,),
            )(i_hbm, o_hbm)

    return gather(tab0, tab1, idx.reshape(1, m))


def _sc_scatter_rows(tab0, tab1, idx, n_rows):
    m = idx.shape[0]
    t, w = tab0.shape
    mesh = plsc.VectorSubcoreMesh(core_axis_name="core", subcore_axis_name="subcore")
    assert m % (SC_WINDOW * mesh.num_cores * mesh.num_subcores) == 0 and t % SC_WINDOW == 0, (m, t)
    windows_per_pass = t // SC_WINDOW
    out = jax.ShapeDtypeStruct((n_rows, w), tab0.dtype)

    @pl.kernel(out_type=(out, out), mesh=mesh)
    def scatter(t0_hbm, t1_hbm, i_hbm, o0_hbm, o1_hbm):
        for t_hbm, o_hbm in ((t0_hbm, o0_hbm), (t1_hbm, o1_hbm)):
            def body(x_vmem, i_vmem, o_hbm=o_hbm):
                pltpu.sync_copy(x_vmem, o_hbm.at[i_vmem.at[0]])

            pltpu.emit_pipeline(
                body,
                grid=(m // SC_WINDOW,),
                in_specs=[pl.BlockSpec((SC_WINDOW, w), lambda i: (i % windows_per_pass, 0)),
                          pl.BlockSpec((1, SC_WINDOW), lambda i: (0, i))],
                out_specs=[],
                core_axis_name=("core", "subcore"),
                dimension_semantics=(pltpu.PARALLEL,),
            )(t_hbm, i_hbm)

    return scatter(tab0, tab1, idx.reshape(1, m))


def _moe_route_kernel(x_ref, g_ref, sh_ref, sc_ref, rw_ref, rb_ref,
                      h0_ref, h1_ref, idx_ref, rank_ref, gate_ref, count_ref, run_ref):
    @pl.when(pl.program_id(0) == 0)
    def _():
        run_ref[...] = jnp.zeros(run_ref.shape, F32)

    h = _adaln(x_ref[...], g_ref[...], sh_ref[...], sc_ref[...])
    h0_ref[...], h1_ref[...] = _pack_rows(h)
    h_hi = h.astype(BF16)
    h_lo = (h - h_hi.astype(F32)).astype(BF16)
    logits = jnp.dot(jnp.concatenate([h_hi, h_lo, h_hi], axis=1), rw_ref[...],
                     preferred_element_type=F32) + rb_ref[...]
    lane = lax.broadcasted_iota(jnp.int32, logits.shape, 1)
    lane_f = lane.astype(F32)
    vals, ids = [], []
    for _ in range(TOP_K):
        m = jnp.max(logits, axis=-1, keepdims=True)
        first = jnp.min(jnp.where(logits == m, lane_f, float(LANES)), axis=-1, keepdims=True)
        vals.append(m)
        ids.append(first)
        logits = jnp.where(lane_f == first, -jnp.inf, logits)
    exps = [jnp.exp(v - vals[0]) for v in vals]
    denom = exps[0]
    for e in exps[1:]:
        denom = denom + e
    tm = logits.shape[0]
    earlier = (lax.broadcasted_iota(jnp.int32, (tm, tm), 1) < lax.broadcasted_iota(jnp.int32, (tm, tm), 0))
    earlier = jnp.where(earlier, 1.0, 0.0).astype(BF16)
    seen = run_ref[...]
    gate = jnp.zeros(logits.shape, F32)
    idx = jnp.zeros(logits.shape, F32)
    rank = jnp.zeros(logits.shape, F32)
    for k in range(TOP_K):
        onehot = jnp.where(lane_f == ids[k], 1.0, 0.0)
        before = jnp.dot(earlier, onehot.astype(BF16), preferred_element_type=F32) + seen
        rank = jnp.where(lane == k, jnp.sum(onehot * before, axis=-1, keepdims=True), rank)
        seen = seen + jnp.sum(onehot, axis=0, keepdims=True)
        gate = jnp.where(lane == k, exps[k] / denom, gate)
        idx = jnp.where(lane == k, ids[k], idx)
    run_ref[...] = seen
    count_ref[...] = jnp.broadcast_to(seen, count_ref.shape).astype(jnp.int32)
    gate_ref[...] = gate
    idx_ref[...] = idx.T[:SUBLANES].astype(jnp.int32)
    rank_ref[...] = rank.T[:SUBLANES].astype(jnp.int32)


def _moe_route(x, gain, shift, scale, rw_pad, rb_pad, seq):
    t, d = x.shape
    tm = min(TOKEN_TILE, seq)
    per_seq = seq // tm
    b = shift.shape[0]
    return pl.pallas_call(
        _moe_route_kernel,
        out_shape=(jax.ShapeDtypeStruct((t, d // 4), jnp.uint32),
                   jax.ShapeDtypeStruct((t, d // 4), jnp.uint32),
                   jax.ShapeDtypeStruct((SUBLANES, t), jnp.int32),
                   jax.ShapeDtypeStruct((SUBLANES, t), jnp.int32),
                   jax.ShapeDtypeStruct((t, LANES), F32),
                   jax.ShapeDtypeStruct((SUBLANES, LANES), jnp.int32)),
        grid=(t // tm,),
        in_specs=[
            pl.BlockSpec((tm, d), lambda i: (i, 0)),
            pl.BlockSpec((1, d), lambda i: (0, 0)),
            pl.BlockSpec((None, 1, d), lambda i: (i // per_seq, 0, 0)),
            pl.BlockSpec((None, 1, d), lambda i: (i // per_seq, 0, 0)),
            pl.BlockSpec((3 * d, LANES), lambda i: (0, 0)),
            pl.BlockSpec((1, LANES), lambda i: (0, 0)),
        ],
        out_specs=(pl.BlockSpec((tm, d // 4), lambda i: (i, 0)),
                   pl.BlockSpec((tm, d // 4), lambda i: (i, 0)),
                   pl.BlockSpec((SUBLANES, tm), lambda i: (0, i)),
                   pl.BlockSpec((SUBLANES, tm), lambda i: (0, i)),
                   pl.BlockSpec((tm, LANES), lambda i: (i, 0)),
                   pl.BlockSpec((SUBLANES, LANES), lambda i: (0, 0))),
        scratch_shapes=[pltpu.VMEM((1, LANES), F32)],
        compiler_params=_params("arbitrary"),
        name="moe_route",
    )(x, gain.reshape(1, d), shift.reshape(b, 1, d), scale.reshape(b, 1, d), rw_pad, rb_pad)


def _expert_kernel(block_e_ref, n_used_ref, n_valid_ref, h0_ref, h1_ref, wgu_ref, bgu_ref, wd_ref, bd_ref,
                   o0_ref, o1_ref):
    i = pl.program_id(0)

    @pl.when(i < n_used_ref[0])
    def _():
        d_e = wd_ref.shape[0]
        written = lax.broadcasted_iota(jnp.int32, h0_ref.shape, 0) < n_valid_ref[i]
        zero = jnp.zeros(h0_ref.shape, h0_ref.dtype)
        quarters = _unpack_rows(jnp.where(written, h0_ref[...], zero), jnp.where(written, h1_ref[...], zero))
        h = jnp.concatenate([q.astype(BF16) for q in quarters], axis=1)
        gu = jnp.dot(h, wgu_ref[...].astype(BF16), preferred_element_type=F32) + bgu_ref[...]
        g = jnp.minimum(gu[:, :d_e], SWIGLU_LIMIT)
        u = jnp.clip(gu[:, d_e:], -SWIGLU_LIMIT, SWIGLU_LIMIT)
        act = g / (1.0 + jnp.exp(-SWIGLU_ALPHA * g)) * (u + 1.0)
        y = jnp.dot(act.astype(BF16), wd_ref[...].astype(BF16), preferred_element_type=F32) + bd_ref[...]
        o0_ref[...], o1_ref[...] = _pack_rows(y)

    @pl.when(i >= n_used_ref[0])
    def _():
        o0_ref[...] = jnp.zeros(o0_ref.shape, o0_ref.dtype)
        o1_ref[...] = jnp.zeros(o1_ref.shape, o1_ref.dtype)


def _expert_mlp(block_e, n_used, n_valid, rows0, rows1, wgu, bgu, wd, bd, layer, tile):
    n_rows, dq = rows0.shape
    d = 4 * dq
    n_l, n_e, _, two_de = wgu.shape
    d_e = two_de // 2
    n_blocks = n_rows // tile
    grid_spec = pltpu.PrefetchScalarGridSpec(
        num_scalar_prefetch=3,
        grid=(n_blocks,),
        in_specs=[
            pl.BlockSpec((tile, dq), lambda i, be, nu, nv: (i, 0)),
            pl.BlockSpec((tile, dq), lambda i, be, nu, nv: (i, 0)),
            pl.BlockSpec((None, None, d, two_de), lambda i, be, nu, nv: (layer, be[i], 0, 0)),
            pl.BlockSpec((None, None, 1, two_de), lambda i, be, nu, nv: (layer, be[i], 0, 0)),
            pl.BlockSpec((None, None, d_e, d), lambda i, be, nu, nv: (layer, be[i], 0, 0)),
            pl.BlockSpec((None, None, 1, d), lambda i, be, nu, nv: (layer, be[i], 0, 0)),
        ],
        out_specs=(pl.BlockSpec((tile, dq), lambda i, be, nu, nv: (i, 0)),
                   pl.BlockSpec((tile, dq), lambda i, be, nu, nv: (i, 0))),
    )
    packed = jax.ShapeDtypeStruct((n_rows, dq), jnp.uint32)
    return pl.pallas_call(
        _expert_kernel,
        out_shape=(packed, packed),
        grid_spec=grid_spec,
        compiler_params=_params("arbitrary"),
        name="expert_mlp",
    )(block_e, n_used, n_valid, rows0, rows1, wgu, bgu.reshape(n_l, n_e, 1, two_de), wd,
      bd.reshape(n_l, n_e, 1, d))


def _combine_kernel(y0_ref, y1_ref, w_ref, x_ref, g_ref, o_ref):
    w = w_ref[...]
    acc = None
    for k in range(y0_ref.shape[0]):
        quarters = _unpack_rows(y0_ref[k], y1_ref[k])
        scaled = [w[:, k:k + 1] * q for q in quarters]
        acc = scaled if acc is None else [a + s for a, s in zip(acc, scaled)]
    o_ref[...] = x_ref[...] + g_ref[...] * jnp.concatenate(acc, axis=1)


def _combine(y0, y1, weights, x, gate, seq):
    t, d = x.shape
    tm = min(TOKEN_TILE, seq)
    per_seq = seq // tm
    b = gate.shape[0]
    return pl.pallas_call(
        _combine_kernel,
        out_shape=jax.ShapeDtypeStruct((t, d), F32),
        grid=(t // tm,),
        in_specs=[
            pl.BlockSpec((y0.shape[0], tm, d // 4), lambda i: (0, i, 0)),
            pl.BlockSpec((y0.shape[0], tm, d // 4), lambda i: (0, i, 0)),
            pl.BlockSpec((tm, LANES), lambda i: (i, 0)),
            pl.BlockSpec((tm, d), lambda i: (i, 0)),
            pl.BlockSpec((None, 1, d), lambda i: (i // per_seq, 0, 0)),
        ],
        out_specs=pl.BlockSpec((tm, d), lambda i: (i, 0)),
        compiler_params=_params("parallel"),
        name="moe_combine",
    )(y0, y1, weights, x, gate.reshape(b, 1, d))


ROW_BLOCK_MULTIPLE = 8


def _routing_tables(idx_t, rank_t, counts, tile):
    top_k, t = idx_t.shape
    n_experts = counts.shape[0]
    n_assign = top_k * t
    padded = (counts + tile - 1) // tile * tile
    pend = jnp.cumsum(padded)
    pstart = pend - padded
    e_flat = idx_t.reshape(n_assign)
    onehot = e_flat[:, None] == jnp.arange(n_experts, dtype=jnp.int32)[None, :]
    slot_row = rank_t.reshape(n_assign) + jnp.sum(jnp.where(onehot, pstart[None, :], 0), axis=1,
                                                  dtype=jnp.int32)
    n_blocks = (n_assign + n_experts * (tile - 1) + tile - 1) // tile
    n_blocks = -(-n_blocks // ROW_BLOCK_MULTIPLE) * ROW_BLOCK_MULTIPLE
    first_row = jnp.arange(n_blocks, dtype=jnp.int32) * tile
    block_e = jnp.minimum(jnp.sum(pend[None, :] <= first_row[:, None], axis=1), n_experts - 1)
    block_e = block_e.astype(jnp.int32)
    n_valid = jnp.clip(counts[block_e] - (first_row - pstart[block_e]), 0, tile).astype(jnp.int32)
    n_used = (pend[-1] // tile).astype(jnp.int32).reshape(1)
    return slot_row, block_e, n_valid, n_used, n_blocks * tile


def _moe(x, gain, shift, scale, gate, rw, rb, wgu, bgu, wd, bd, layer, seq):
    t, d = x.shape
    n_e = rw.shape[1]
    rw_pad = jnp.pad(rw, ((0, 0), (0, LANES - n_e)))
    rw_hi = rw_pad.astype(BF16)
    rw_lo = (rw_pad - rw_hi.astype(F32)).astype(BF16)
    rw_pad = jnp.concatenate([rw_hi, rw_hi, rw_lo], axis=0)
    rb_pad = jnp.pad(rb, (0, LANES - n_e), constant_values=MASK_VALUE).reshape(1, LANES)
    h0, h1, idx_t, rank_t, weights, counts = _moe_route(x, gain, shift, scale, rw_pad, rb_pad, seq)
    tile = min(EXPERT_TILE, t * TOP_K // n_e)
    slot_row, block_e, n_valid, n_used, n_rows = _routing_tables(
        idx_t[:TOP_K], rank_t[:TOP_K], counts[0, :n_e], tile)
    rows0, rows1 = _sc_scatter_rows(h0, h1, slot_row, n_rows)
    ys0, ys1 = _expert_mlp(block_e, n_used, n_valid, rows0, rows1, wgu, bgu, wd, bd, layer, tile)
    y0, y1 = _sc_gather_rows(ys0, ys1, slot_row)
    return _combine(y0.reshape(TOP_K, t, d // 4), y1.reshape(TOP_K, t, d // 4), weights, x, gate, seq)


def _rmsnorm_kernel(x_ref, g_ref, o_ref):
    x = x_ref[...]
    ms = jnp.mean(x * x, axis=-1, keepdims=True)
    o_ref[...] = x * lax.rsqrt(ms + EPS) * g_ref[...]


def _rmsnorm(x, gain):
    t, d = x.shape
    tm = min(TOKEN_TILE, t)
    return pl.pallas_call(
        _rmsnorm_kernel,
        out_shape=jax.ShapeDtypeStruct((t, d), F32),
        grid=(t // tm,),
        in_specs=[pl.BlockSpec((tm, d), lambda i: (i, 0)),
                  pl.BlockSpec((1, d), lambda i: (0, 0))],
        out_specs=pl.BlockSpec((tm, d), lambda i: (i, 0)),
        compiler_params=_params("parallel"),
        name="final_rmsnorm",
    )(x, gain.reshape(1, d))


def kernel(x, c, ada_w, ada_b, norm_mix, norm_moe, a_wqkv, a_wo, a_lambda, a_subln, kv_norm,
           kv_ada_w, kv_ada_b, kv_w, b_wq, b_wo, router_w, router_b, w_gate_up, b_gate_up,
           w_down, b_down, final_norm):
    b, s, d = x.shape
    depth = ada_w.shape[0]
    n_a = a_wqkv.shape[0]
    xt = x.reshape(b * s, d)

    c_pad = jnp.pad(c, ((0, SUBLANES - b), (0, 0)))
    mod = _modulation(c_pad, ada_w, ada_b)[:, :b]
    kv_mod = _modulation(c_pad, kv_ada_w[None], kv_ada_b[None])[0, :b]

    for l in range(depth):
        sh1, sc1, g1, sh2, sc2, g2 = jnp.split(mod[l], 6, axis=-1)
        if l < n_a:
            qkv = _norm_linear(xt, norm_mix[l], sh1, sc1, a_wqkv[l].astype(BF16), s)
            o = _diff_attention(qkv, a_lambda[l], a_subln[l], b, s, l)
            xt = _proj_residual(o, a_wo[l].astype(BF16), xt, g1, s)
        else:
            if l == n_a:
                kv_sh, kv_sc = jnp.split(kv_mod, 2, axis=-1)
                kv = _norm_linear(xt, kv_norm, kv_sh, kv_sc, kv_w.astype(BF16), s)
            j = l - n_a
            q = _norm_linear(xt, norm_mix[l], sh1, sc1, b_wq[j].astype(BF16), s)
            o = _stick_attention(q, kv, b, s)
            xt = _proj_residual(o, b_wo[j].astype(BF16), xt, g1, s)
        xt = _moe(xt, norm_moe[l], sh2, sc2, g2, router_w[l], router_b[l],
                  w_gate_up, b_gate_up, w_down, b_down, l, s)
    return _rmsnorm(xt, final_norm).reshape(b, s, d)
```

```python
import functools
import math

import jax
import jax.numpy as jnp
import numpy as np
from jax import lax
from jax.experimental import pallas as pl
from jax.experimental.pallas import tpu as pltpu
from jax.experimental.pallas import tpu_sc as plsc

F32 = jnp.float32
BF16 = jnp.bfloat16
EPS = 1e-6
MASK_VALUE = -1e30
TOP_K = 4
SWIGLU_ALPHA = 1.702
SWIGLU_LIMIT = 7.0
LOG2E = math.log2(math.e)
UNDERFLOW_BITS = 160.0
NORM_SLACK = 1.03

LANES = 128
SUBLANES = 8
HEAD_DIM = 64
V7X_VMEM_BYTES = 64 * 2 ** 20
VMEM_LIMIT = V7X_VMEM_BYTES * 7 // 8

TOKEN_TILE = 512
ATTN_TILE = 512
STICK_TILE = 256
DIFF_SUBTILES = 2
STICK_SUBTILES = 4
EXPERT_TILE = 512


def _params(*semantics):
    return pltpu.CompilerParams(dimension_semantics=semantics, vmem_limit_bytes=VMEM_LIMIT)


def _largest_tile(n, cap):
    t = min(n, cap) // LANES * LANES
    while n % t:
        t -= LANES
    return t


def _adaln(x, gain, shift, scale):
    ms = jnp.mean(x * x, axis=-1, keepdims=True)
    return x * lax.rsqrt(ms + EPS) * gain * (1.0 + scale) + shift


def _pack_bf16_pair(lo, hi):
    lo_bits = lax.bitcast_convert_type(lo.astype(BF16).astype(F32), jnp.uint32) >> 16
    hi_bits = lax.bitcast_convert_type(hi.astype(BF16).astype(F32), jnp.uint32) & jnp.uint32(0xFFFF0000)
    return lo_bits | hi_bits


def _unpack_bf16_pair(packed):
    lo = lax.bitcast_convert_type(packed << 16, F32)
    hi = lax.bitcast_convert_type(packed & jnp.uint32(0xFFFF0000), F32)
    return lo, hi


def _pack_rows(x):
    q = x.shape[1] // 4
    return (_pack_bf16_pair(x[:, :q], x[:, 2 * q:3 * q]),
            _pack_bf16_pair(x[:, q:2 * q], x[:, 3 * q:]))


def _unpack_rows(p0, p1):
    a0, a2 = _unpack_bf16_pair(p0)
    a1, a3 = _unpack_bf16_pair(p1)
    return a0, a1, a2, a3


def _mod_kernel(c_ref, w_ref, b_ref, o_ref):
    c = c_ref[...]
    a = c / (1.0 + jnp.exp(-c))
    o_ref[...] = jnp.dot(a, w_ref[...], precision=lax.Precision.HIGHEST,
                         preferred_element_type=F32) + b_ref[...]


def _modulation(c_pad, w, b):
    n_l, d, n = w.shape
    tn = _largest_tile(n, 2048)
    return pl.pallas_call(
        _mod_kernel,
        out_shape=jax.ShapeDtypeStruct((n_l, SUBLANES, n), F32),
        grid=(n_l, n // tn),
        in_specs=[
            pl.BlockSpec((SUBLANES, d), lambda l, j: (0, 0)),
            pl.BlockSpec((None, d, tn), lambda l, j: (l, 0, j)),
            pl.BlockSpec((None, 1, tn), lambda l, j: (l, 0, j)),
        ],
        out_specs=pl.BlockSpec((None, SUBLANES, tn), lambda l, j: (l, 0, j)),
        compiler_params=_params("parallel", "parallel"),
        name="modulation",
    )(c_pad, w, b.reshape(n_l, 1, n))


def _norm_linear_kernel(x_ref, g_ref, sh_ref, sc_ref, w_ref, o_ref):
    h = _adaln(x_ref[...], g_ref[...], sh_ref[...], sc_ref[...])
    o_ref[...] = jnp.dot(h.astype(BF16), w_ref[...], preferred_element_type=F32).astype(o_ref.dtype)


def _norm_linear(x, gain, shift, scale, w, seq):
    t, d = x.shape
    n = w.shape[1]
    tm = min(TOKEN_TILE, seq)
    per_seq = seq // tm
    b = shift.shape[0]
    return pl.pallas_call(
        _norm_linear_kernel,
        out_shape=jax.ShapeDtypeStruct((t, n), BF16),
        grid=(t // tm,),
        in_specs=[
            pl.BlockSpec((tm, d), lambda i: (i, 0)),
            pl.BlockSpec((1, d), lambda i: (0, 0)),
            pl.BlockSpec((None, 1, d), lambda i: (i // per_seq, 0, 0)),
            pl.BlockSpec((None, 1, d), lambda i: (i // per_seq, 0, 0)),
            pl.BlockSpec((d, n), lambda i: (0, 0)),
        ],
        out_specs=pl.BlockSpec((tm, n), lambda i: (i, 0)),
        compiler_params=_params("parallel"),
        name="norm_linear",
    )(x, gain.reshape(1, d), shift.reshape(b, 1, d), scale.reshape(b, 1, d), w)


def _proj_residual_kernel(a_ref, w_ref, x_ref, g_ref, o_ref):
    y = jnp.dot(a_ref[...], w_ref[...], preferred_element_type=F32)
    o_ref[...] = x_ref[...] + g_ref[...] * y


def _proj_residual(a, w, x, gate, seq):
    t, d = x.shape
    k = a.shape[1]
    tm = min(TOKEN_TILE, seq)
    per_seq = seq // tm
    b = gate.shape[0]
    return pl.pallas_call(
        _proj_residual_kernel,
        out_shape=jax.ShapeDtypeStruct((t, d), F32),
        grid=(t // tm,),
        in_specs=[
            pl.BlockSpec((tm, k), lambda i: (i, 0)),
            pl.BlockSpec((k, d), lambda i: (0, 0)),
            pl.BlockSpec((tm, d), lambda i: (i, 0)),
            pl.BlockSpec((None, 1, d), lambda i: (i // per_seq, 0, 0)),
        ],
        out_specs=pl.BlockSpec((tm, d), lambda i: (i, 0)),
        compiler_params=_params("parallel"),
        name="proj_residual",
    )(a, w, x, gate.reshape(b, 1, d))


def _scaled_halves(q):
    q = (q.astype(F32) * (HEAD_DIM ** -0.5 * LOG2E)).astype(BF16)
    lane = lax.broadcasted_iota(jnp.int32, q.shape, 1)
    zero = jnp.zeros_like(q)
    return jnp.concatenate([jnp.where(lane < HEAD_DIM, q, zero),
                            jnp.where(lane >= HEAD_DIM, q, zero)], axis=0)


def _dot_nt(a, b):
    return lax.dot_general(a, b, (((1,), (1,)), ((), ())), preferred_element_type=F32)


def _tile_row_col(tq, tk):
    row = lax.broadcasted_iota(jnp.int32, (2 * tq, tk), 0)
    row = jnp.where(row >= tq, row - tq, row)
    col = lax.broadcasted_iota(jnp.int32, (2 * tq, tk), 1)
    return row, col


def _lane_chunks(n):
    return [slice(c * LANES, (c + 1) * LANES) for c in range(n // LANES)]


def _diff_attn_kernel(q_ref, k_ref, v_ref, slope_ref, lam_ref, subln_ref, o_ref,
                      m_ref, l_ref, acc_ref, sa_ref, sb_ref, kmax_ref, *, tile, subtiles, lam_init):
    step = pl.program_id(2)

    def one_tile(sub, carry):
        i = step * subtiles + sub
        rows = pl.ds(pl.multiple_of(sub * tile, tile), tile)
        qq = _scaled_halves(q_ref[rows, :])
        slope = slope_ref[...]
        col_i = lax.broadcasted_iota(jnp.int32, (1, tile), 1)
        ones = jnp.ones((tile, LANES), BF16)
        m_ref[...] = jnp.full(m_ref.shape, MASK_VALUE, F32)
        l_ref[...] = jnp.zeros(l_ref.shape, F32)
        acc_ref[...] = jnp.zeros(acc_ref.shape, F32)

        def scores(s_ref, kb):
            k = k_ref[pl.ds(pl.multiple_of(kb * tile, tile), tile), :]
            s_ref[...] = _dot_nt(qq, k) + slope * (col_i + (kb - i) * tile).astype(F32)

        def accumulate(s_ref, kb, diagonal):
            v = v_ref[pl.ds(pl.multiple_of(kb * tile, tile), tile), :]
            if diagonal:
                row, col = _tile_row_col(tile, tile)
                s_ref[...] = jnp.where(col <= row, s_ref[...], MASK_VALUE)
            m_prev = m_ref[...]
            m_new = jnp.maximum(m_prev, jnp.max(s_ref[...], axis=-1, keepdims=True))
            alpha = jnp.exp2(m_prev - m_new)
            p = jnp.concatenate([jnp.exp2(s_ref[:, sl] - m_new).astype(BF16) for sl in _lane_chunks(tile)],
                                axis=1)
            pv = jnp.dot(p, jnp.concatenate([v, ones], axis=1),
                         preferred_element_type=F32)
            acc_ref[...] = alpha * acc_ref[...] + pv[:, :LANES]
            l_ref[...] = alpha * l_ref[...] + pv[:, LANES:]
            m_ref[...] = m_new

        lane = lax.broadcasted_iota(jnp.int32, (1, LANES), 1)
        half_ones = (lax.broadcasted_iota(jnp.int32, (LANES, LANES), 0) < HEAD_DIM) == (
            lax.broadcasted_iota(jnp.int32, (LANES, LANES), 1) < HEAD_DIM)
        half_ones = jnp.where(half_ones, 1.0, 0.0).astype(BF16)

        @pl.when(i == 0)
        def _():
            def chunk_max(c, best):
                k = k_ref[pl.ds(pl.multiple_of(c * tile, tile), tile), :].astype(F32)
                n2 = jnp.dot((k * k).astype(BF16), half_ones, preferred_element_type=F32)
                return jnp.maximum(best, jnp.max(n2, axis=0, keepdims=True))

            best = lax.fori_loop(0, k_ref.shape[0] // tile, chunk_max, jnp.zeros((1, LANES), F32))
            kmax_ref[...] = jnp.broadcast_to(best * NORM_SLACK, kmax_ref.shape)

        qf = qq.astype(F32)
        qn2 = jnp.dot((qf * qf).astype(BF16), jnp.ones((LANES, LANES), BF16), preferred_element_type=F32)
        q_half = jnp.where(lane < HEAD_DIM, jnp.max(qn2[:tile], axis=0, keepdims=True),
                           jnp.max(qn2[tile:], axis=0, keepdims=True))
        raw_bound = jnp.sqrt(jnp.max(q_half * NORM_SLACK * kmax_ref[0:1, :], axis=1, keepdims=True))

        def live(kb):
            far = jnp.zeros((1, LANES), jnp.int32) + ((kb + 1 - i) * tile - 1)
            best_left = raw_bound + slope[:, :LANES] * far.astype(F32)
            lowest_max = jnp.min(m_ref[...], axis=0, keepdims=True)
            return jnp.max(best_left - lowest_max) > -UNDERFLOW_BITS

        scores(sa_ref, i)

        @pl.when(i == 0)
        def _():
            accumulate(sa_ref, 0, True)

        @pl.when(i == 1)
        def _():
            scores(sb_ref, 0)
            accumulate(sa_ref, 1, True)
            accumulate(sb_ref, 0, False)

        @pl.when(i >= 2)
        def _():
            scores(sb_ref, i - 1)
            accumulate(sa_ref, i, True)
            scores(sa_ref, i - 2)
            accumulate(sb_ref, i - 1, False)
            n_pairs = (i - 2) // 2

            def cond(carry):
                j, go = carry
                return jnp.logical_and(j < n_pairs, go)

            def body(carry):
                j, _ = carry
                kb = i - 2 - 2 * j
                scores(sb_ref, kb - 1)
                accumulate(sa_ref, kb, False)
                scores(sa_ref, kb - 2)
                accumulate(sb_ref, kb - 1, False)
                return j + 1, live(kb - 2)

            _, go = lax.while_loop(cond, body, (jnp.int32(0), live(i - 2)))

            @pl.when(jnp.logical_and(go, i % 2 == 0))
            def _():
                accumulate(sa_ref, 0, False)

            @pl.when(jnp.logical_and(go, i % 2 == 1))
            def _():
                scores(sb_ref, 0)
                accumulate(sa_ref, 1, False)
                accumulate(sb_ref, 0, False)

        o_all = acc_ref[...] / l_ref[...]
        lv = lam_ref[...]
        lam = (jnp.exp(jnp.sum(lv[0:1] * lv[1:2], axis=1, keepdims=True))
               - jnp.exp(jnp.sum(lv[2:3] * lv[3:4], axis=1, keepdims=True)) + lam_init)
        o = o_all[:tile] - lam * o_all[tile:]
        ms = jnp.mean(o * o, axis=-1, keepdims=True)
        o = o * lax.rsqrt(ms + EPS) * subln_ref[...] * (1.0 - lam_init)
        o_ref[rows, :] = o.astype(o_ref.dtype)
        return carry

    lax.fori_loop(0, subtiles, one_tile, 0)


def _diff_attention(qkv, lam_vecs, subln, batch, seq, layer_idx):
    t = qkv.shape[0]
    d = qkv.shape[1] // 3
    heads = d // LANES
    tile = min(ATTN_TILE, seq)
    subtiles = min(DIFF_SUBTILES, seq // tile)
    nq = seq // (tile * subtiles)
    lam_init = 0.8 - 0.6 * math.exp(-0.3 * layer_idx)
    slopes = np.array([2.0 ** (-8.0 * (h + 1) / heads) * LOG2E for h in range(heads)], np.float32)
    slopes = jnp.asarray(np.broadcast_to(slopes[:, None, None], (heads, 1, tile)))
    kernel = functools.partial(_diff_attn_kernel, tile=tile, subtiles=subtiles, lam_init=lam_init)
    return pl.pallas_call(
        kernel,
        out_shape=jax.ShapeDtypeStruct((t, d), BF16),
        grid=(batch, heads, nq),
        in_specs=[
            pl.BlockSpec((tile * subtiles, LANES), lambda b, h, i: (b * nq + i, h)),
            pl.BlockSpec((seq, LANES), lambda b, h, i: (b, heads + h)),
            pl.BlockSpec((seq, LANES), lambda b, h, i: (b, 2 * heads + h)),
            pl.BlockSpec((None, 1, tile), lambda b, h, i: (h, 0, 0)),
            pl.BlockSpec(lam_vecs.shape, lambda b, h, i: (0, 0)),
            pl.BlockSpec((1, LANES), lambda b, h, i: (0, 0)),
        ],
        out_specs=pl.BlockSpec((tile * subtiles, LANES), lambda b, h, i: (b * nq + i, h)),
        scratch_shapes=[
            pltpu.VMEM((2 * tile, LANES), F32),
            pltpu.VMEM((2 * tile, LANES), F32),
            pltpu.VMEM((2 * tile, LANES), F32),
            pltpu.VMEM((2 * tile, tile), F32),
            pltpu.VMEM((2 * tile, tile), F32),
            pltpu.VMEM((SUBLANES, LANES), F32),
        ],
        compiler_params=_params("parallel", "parallel", "arbitrary"),
        name="diff_attention",
    )(qkv, qkv, qkv, slopes, lam_vecs, subln.reshape(1, LANES))


CHUNK = 256


def _suffix_sum_matrix():
    j = np.arange(CHUNK)
    return jnp.asarray((j[:, None] >= j[None, :]).astype(np.float32), dtype=BF16)


def _stick_attn_kernel(q_ref, k_ref, v_ref, mt_ref, o_ref, rem_ref, acc_ref, za_ref, zb_ref, *, tile, subtiles):
    step = pl.program_id(2)

    def one_tile(sub, carry):
        i = step * subtiles + sub
        rows = pl.ds(pl.multiple_of(sub * tile, tile), tile)
        qq = _scaled_halves(q_ref[rows, :])
        mt = mt_ref[...]
        rem_ref[...] = jnp.zeros(rem_ref.shape, F32)
        acc_ref[...] = jnp.zeros(acc_ref.shape, F32)

        def scores(z_ref, kb):
            z_ref[...] = _dot_nt(qq, k_ref[pl.ds(pl.multiple_of(kb * tile, tile), tile), :])

        def accumulate(z_ref, kb, diagonal):
            v = v_ref[pl.ds(pl.multiple_of(kb * tile, tile), tile), :]
            rem = rem_ref[...]
            chunks = []
            for c in reversed(range(tile // CHUNK)):
                z = z_ref[:, c * CHUNK:(c + 1) * CHUNK]
                softplus = jnp.maximum(z, 0.0) + jnp.log2(1.0 + jnp.exp2(-jnp.abs(z)))
                if diagonal:
                    row = lax.broadcasted_iota(jnp.int32, z.shape, 0)
                    row = jnp.where(row >= tile, row - tile, row)
                    strict = lax.broadcasted_iota(jnp.int32, z.shape, 1) + c * CHUNK < row
                    softplus = jnp.where(strict, softplus, 0.0)
                sums = jnp.dot(softplus.astype(BF16), mt, preferred_element_type=F32)
                for h in reversed(range(CHUNK // LANES)):
                    sl = slice(h * LANES, (h + 1) * LANES)
                    a = jnp.exp2(z_ref[:, c * CHUNK + h * LANES:c * CHUNK + (h + 1) * LANES] - sums[:, sl] - rem)
                    if diagonal:
                        a = jnp.where(strict[:, sl], a, 0.0)
                    chunks.append(a.astype(BF16))
                rem = rem + jnp.broadcast_to(sums[:, 0:1], rem.shape)
            rem_ref[...] = rem
            acc_ref[...] += jnp.dot(jnp.concatenate(chunks[::-1], axis=1), v, preferred_element_type=F32)

        def live():
            return jnp.min(rem_ref[...]) < UNDERFLOW_BITS

        scores(za_ref, i)

        @pl.when(i == 0)
        def _():
            accumulate(za_ref, 0, True)

        @pl.when(i == 1)
        def _():
            scores(zb_ref, 0)
            accumulate(za_ref, 1, True)
            accumulate(zb_ref, 0, False)

        @pl.when(i >= 2)
        def _():
            scores(zb_ref, i - 1)
            accumulate(za_ref, i, True)
            scores(za_ref, i - 2)
            accumulate(zb_ref, i - 1, False)
            n_pairs = (i - 2) // 2

            def cond(carry):
                j, go = carry
                return jnp.logical_and(j < n_pairs, go)

            def body(carry):
                j, _ = carry
                kb = i - 2 - 2 * j
                scores(zb_ref, kb - 1)
                accumulate(za_ref, kb, False)
                scores(za_ref, kb - 2)
                accumulate(zb_ref, kb - 1, False)
                return j + 1, live()

            _, go = lax.while_loop(cond, body, (jnp.int32(0), live()))

            @pl.when(jnp.logical_and(go, i % 2 == 0))
            def _():
                accumulate(za_ref, 0, False)

            @pl.when(jnp.logical_and(go, i % 2 == 1))
            def _():
                scores(zb_ref, 0)
                accumulate(za_ref, 1, False)
                accumulate(zb_ref, 0, False)

        acc = acc_ref[...]
        lane = lax.broadcasted_iota(jnp.int32, (tile, LANES), 1)
        o_ref[rows, :] = jnp.where(lane < HEAD_DIM, acc[:tile], acc[tile:]).astype(o_ref.dtype)
        return carry

    lax.fori_loop(0, subtiles, one_tile, 0)


def _stick_attention(q, kv, batch, seq):
    t, d = q.shape
    pairs = d // LANES
    tile = min(STICK_TILE, seq)
    subtiles = min(STICK_SUBTILES, seq // tile)
    nq = seq // (tile * subtiles)
    kernel = functools.partial(_stick_attn_kernel, tile=tile, subtiles=subtiles)
    return pl.pallas_call(
        kernel,
        out_shape=jax.ShapeDtypeStruct((t, d), BF16),
        grid=(batch, pairs, nq),
        in_specs=[
            pl.BlockSpec((tile * subtiles, LANES), lambda b, j, i: (b * nq + i, j)),
            pl.BlockSpec((seq, LANES), lambda b, j, i: (b, j)),
            pl.BlockSpec((seq, LANES), lambda b, j, i: (b, pairs + j)),
            pl.BlockSpec((CHUNK, CHUNK), lambda b, j, i: (0, 0)),
        ],
        out_specs=pl.BlockSpec((tile * subtiles, LANES), lambda b, j, i: (b * nq + i, j)),
        scratch_shapes=[
            pltpu.VMEM((2 * tile, LANES), F32),
            pltpu.VMEM((2 * tile, LANES), F32),
            pltpu.VMEM((2 * tile, tile), F32),
            pltpu.VMEM((2 * tile, tile), F32),
        ],
        compiler_params=_params("parallel", "parallel", "arbitrary"),
        name="stick_attention",
    )(q, kv, kv, _suffix_sum_matrix())


SC_WINDOW = 128


def _sc_gather_rows(tab0, tab1, idx):
    m = idx.shape[0]
    w = tab0.shape[1]
    mesh = plsc.VectorSubcoreMesh(core_axis_name="core", subcore_axis_name="subcore")
    assert m % (SC_WINDOW * mesh.num_cores * mesh.num_subcores) == 0, m
    out = jax.ShapeDtypeStruct((m, w), tab0.dtype)

    @pl.kernel(out_type=(out, out), mesh=mesh)
    def gather(t0_hbm, t1_hbm, i_hbm, o0_hbm, o1_hbm):
        for t_hbm, o_hbm in ((t0_hbm, o0_hbm), (t1_hbm, o1_hbm)):
            def body(i_vmem, o_vmem, t_hbm=t_hbm):
                pltpu.sync_copy(t_hbm.at[i_vmem.at[0]], o_vmem)

            pltpu.emit_pipeline(
                body,
                grid=(m // SC_WINDOW,),
                in_specs=[pl.BlockSpec((1, SC_WINDOW), lambda i: (0, i))],
                out_specs=[pl.BlockSpec((SC_WINDOW, w), lambda i: (i, 0))],
                core_axis_name=("core", "subcore"),
                dimension_semantics=(pltpu.PARALLEL,),
            )(i_hbm, o_hbm)

    return gather(tab0, tab1, idx.reshape(1, m))


def _sc_scatter_rows(tab0, tab1, idx, n_rows):
    m = idx.shape[0]
    t, w = tab0.shape
    mesh = plsc.VectorSubcoreMesh(core_axis_name="core", subcore_axis_name="subcore")
    assert m % (SC_WINDOW * mesh.num_cores * mesh.num_subcores) == 0 and t % SC_WINDOW == 0, (m, t)
    windows_per_pass = t // SC_WINDOW
    out = jax.ShapeDtypeStruct((n_rows, w), tab0.dtype)

    @pl.kernel(out_type=(out, out), mesh=mesh)
    def scatter(t0_hbm, t1_hbm, i_hbm, o0_hbm, o1_hbm):
        for t_hbm, o_hbm in ((t0_hbm, o0_hbm), (t1_hbm, o1_hbm)):
            def body(x_vmem, i_vmem, o_hbm=o_hbm):
                pltpu.sync_copy(x_vmem, o_hbm.at[i_vmem.at[0]])

            pltpu.emit_pipeline(
                body,
                grid=(m // SC_WINDOW,),
                in_specs=[pl.BlockSpec((SC_WINDOW, w), lambda i: (i % windows_per_pass, 0)),
                          pl.BlockSpec((1, SC_WINDOW), lambda i: (0, i))],
                out_specs=[],
                core_axis_name=("core", "subcore"),
                dimension_semantics=(pltpu.PARALLEL,),
            )(t_hbm, i_hbm)

    return scatter(tab0, tab1, idx.reshape(1, m))


def _moe_route_kernel(x_ref, g_ref, sh_ref, sc_ref, rw_ref, rb_ref,
                      h0_ref, h1_ref, idx_ref, rank_ref, gate_ref, count_ref, run_ref):
    @pl.when(pl.program_id(0) == 0)
    def _():
        run_ref[...] = jnp.zeros(run_ref.shape, F32)

    h = _adaln(x_ref[...], g_ref[...], sh_ref[...], sc_ref[...])
    h0_ref[...], h1_ref[...] = _pack_rows(h)
    h_hi = h.astype(BF16)
    h_lo = (h - h_hi.astype(F32)).astype(BF16)
    logits = jnp.dot(jnp.concatenate([h_hi, h_lo, h_hi], axis=1), rw_ref[...],
                     preferred_element_type=F32) + rb_ref[...]
    lane = lax.broadcasted_iota(jnp.int32, logits.shape, 1)
    lane_f = lane.astype(F32)
    vals, ids = [], []
    for _ in range(TOP_K):
        m = jnp.max(logits, axis=-1, keepdims=True)
        first = jnp.min(jnp.where(logits == m, lane_f, float(LANES)), axis=-1, keepdims=True)
        vals.append(m)
        ids.append(first)
        logits = jnp.where(lane_f == first, -jnp.inf, logits)
    exps = [jnp.exp(v - vals[0]) for v in vals]
    denom = exps[0]
    for e in exps[1:]:
        denom = denom + e
    tm = logits.shape[0]
    earlier = (lax.broadcasted_iota(jnp.int32, (tm, tm), 1) < lax.broadcasted_iota(jnp.int32, (tm, tm), 0))
    earlier = jnp.where(earlier, 1.0, 0.0).astype(BF16)
    seen = run_ref[...]
    gate = jnp.zeros(logits.shape, F32)
    idx = jnp.zeros(logits.shape, F32)
    rank = jnp.zeros(logits.shape, F32)
    for k in range(TOP_K):
        onehot = jnp.where(lane_f == ids[k], 1.0, 0.0)
        before = jnp.dot(earlier, onehot.astype(BF16), preferred_element_type=F32) + seen
        rank = jnp.where(lane == k, jnp.sum(onehot * before, axis=-1, keepdims=True), rank)
        seen = seen + jnp.sum(onehot, axis=0, keepdims=True)
        gate = jnp.where(lane == k, exps[k] / denom, gate)
        idx = jnp.where(lane == k, ids[k], idx)
    run_ref[...] = seen
    count_ref[...] = jnp.broadcast_to(seen, count_ref.shape).astype(jnp.int32)
    gate_ref[...] = gate
    idx_ref[...] = idx.T[:SUBLANES].astype(jnp.int32)
    rank_ref[...] = rank.T[:SUBLANES].astype(jnp.int32)


def _moe_route(x, gain, shift, scale, rw_pad, rb_pad, seq):
    t, d = x.shape
    tm = min(TOKEN_TILE, seq)
    per_seq = seq // tm
    b = shift.shape[0]
    return pl.pallas_call(
        _moe_route_kernel,
        out_shape=(jax.ShapeDtypeStruct((t, d // 4), jnp.uint32),
                   jax.ShapeDtypeStruct((t, d // 4), jnp.uint32),
                   jax.ShapeDtypeStruct((SUBLANES, t), jnp.int32),
                   jax.ShapeDtypeStruct((SUBLANES, t), jnp.int32),
                   jax.ShapeDtypeStruct((t, LANES), F32),
                   jax.ShapeDtypeStruct((SUBLANES, LANES), jnp.int32)),
        grid=(t // tm,),
        in_specs=[
            pl.BlockSpec((tm, d), lambda i: (i, 0)),
            pl.BlockSpec((1, d), lambda i: (0, 0)),
            pl.BlockSpec((None, 1, d), lambda i: (i // per_seq, 0, 0)),
            pl.BlockSpec((None, 1, d), lambda i: (i // per_seq, 0, 0)),
            pl.BlockSpec((3 * d, LANES), lambda i: (0, 0)),
            pl.BlockSpec((1, LANES), lambda i: (0, 0)),
        ],
        out_specs=(pl.BlockSpec((tm, d // 4), lambda i: (i, 0)),
                   pl.BlockSpec((tm, d // 4), lambda i: (i, 0)),
                   pl.BlockSpec((SUBLANES, tm), lambda i: (0, i)),
                   pl.BlockSpec((SUBLANES, tm), lambda i: (0, i)),
                   pl.BlockSpec((tm, LANES), lambda i: (i, 0)),
                   pl.BlockSpec((SUBLANES, LANES), lambda i: (0, 0))),
        scratch_shapes=[pltpu.VMEM((1, LANES), F32)],
        compiler_params=_params("arbitrary"),
        name="moe_route",
    )(x, gain.reshape(1, d), shift.reshape(b, 1, d), scale.reshape(b, 1, d), rw_pad, rb_pad)


def _expert_kernel(block_e_ref, n_used_ref, n_valid_ref, h0_ref, h1_ref, wgu_ref, bgu_ref, wd_ref, bd_ref,
                   o0_ref, o1_ref):
    i = pl.program_id(0)

    @pl.when(i < n_used_ref[0])
    def _():
        d_e = wd_ref.shape[0]
        written = lax.broadcasted_iota(jnp.int32, h0_ref.shape, 0) < n_valid_ref[i]
        zero = jnp.zeros(h0_ref.shape, h0_ref.dtype)
        quarters = _unpack_rows(jnp.where(written, h0_ref[...], zero), jnp.where(written, h1_ref[...], zero))
        h = jnp.concatenate([q.astype(BF16) for q in quarters], axis=1)
        gu = jnp.dot(h, wgu_ref[...].astype(BF16), preferred_element_type=F32) + bgu_ref[...]
        g = jnp.minimum(gu[:, :d_e], SWIGLU_LIMIT)
        u = jnp.clip(gu[:, d_e:], -SWIGLU_LIMIT, SWIGLU_LIMIT)
        act = g / (1.0 + jnp.exp(-SWIGLU_ALPHA * g)) * (u + 1.0)
        y = jnp.dot(act.astype(BF16), wd_ref[...].astype(BF16), preferred_element_type=F32) + bd_ref[...]
        o0_ref[...], o1_ref[...] = _pack_rows(y)

    @pl.when(i >= n_used_ref[0])
    def _():
        o0_ref[...] = jnp.zeros(o0_ref.shape, o0_ref.dtype)
        o1_ref[...] = jnp.zeros(o1_ref.shape, o1_ref.dtype)


def _expert_mlp(block_e, n_used, n_valid, rows0, rows1, wgu, bgu, wd, bd, layer, tile):
    n_rows, dq = rows0.shape
    d = 4 * dq
    n_l, n_e, _, two_de = wgu.shape
    d_e = two_de // 2
    n_blocks = n_rows // tile
    grid_spec = pltpu.PrefetchScalarGridSpec(
        num_scalar_prefetch=3,
        grid=(n_blocks,),
        in_specs=[
            pl.BlockSpec((tile, dq), lambda i, be, nu, nv: (i, 0)),
            pl.BlockSpec((tile, dq), lambda i, be, nu, nv: (i, 0)),
            pl.BlockSpec((None, None, d, two_de), lambda i, be, nu, nv: (layer, be[i], 0, 0)),
            pl.BlockSpec((None, None, 1, two_de), lambda i, be, nu, nv: (layer, be[i], 0, 0)),
            pl.BlockSpec((None, None, d_e, d), lambda i, be, nu, nv: (layer, be[i], 0, 0)),
            pl.BlockSpec((None, None, 1, d), lambda i, be, nu, nv: (layer, be[i], 0, 0)),
        ],
        out_specs=(pl.BlockSpec((tile, dq), lambda i, be, nu, nv: (i, 0)),
                   pl.BlockSpec((tile, dq), lambda i, be, nu, nv: (i, 0))),
    )
    packed = jax.ShapeDtypeStruct((n_rows, dq), jnp.uint32)
    return pl.pallas_call(
        _expert_kernel,
        out_shape=(packed, packed),
        grid_spec=grid_spec,
        compiler_params=_params("arbitrary"),
        name="expert_mlp",
    )(block_e, n_used, n_valid, rows0, rows1, wgu, bgu.reshape(n_l, n_e, 1, two_de), wd,
      bd.reshape(n_l, n_e, 1, d))


def _combine_kernel(y0_ref, y1_ref, w_ref, x_ref, g_ref, o_ref):
    w = w_ref[...]
    acc = None
    for k in range(y0_ref.shape[0]):
        quarters = _unpack_rows(y0_ref[k], y1_ref[k])
        scaled = [w[:, k:k + 1] * q for q in quarters]
        acc = scaled if acc is None else [a + s for a, s in zip(acc, scaled)]
    o_ref[...] = x_ref[...] + g_ref[...] * jnp.concatenate(acc, axis=1)


def _combine(y0, y1, weights, x, gate, seq):
    t, d = x.shape
    tm = min(TOKEN_TILE, seq)
    per_seq = seq // tm
    b = gate.shape[0]
    return pl.pallas_call(
        _combine_kernel,
        out_shape=jax.ShapeDtypeStruct((t, d), F32),
        grid=(t // tm,),
        in_specs=[
            pl.BlockSpec((y0.shape[0], tm, d // 4), lambda i: (0, i, 0)),
            pl.BlockSpec((y0.shape[0], tm, d // 4), lambda i: (0, i, 0)),
            pl.BlockSpec((tm, LANES), lambda i: (i, 0)),
            pl.BlockSpec((tm, d), lambda i: (i, 0)),
            pl.BlockSpec((None, 1, d), lambda i: (i // per_seq, 0, 0)),
        ],
        out_specs=pl.BlockSpec((tm, d), lambda i: (i, 0)),
        compiler_params=_params("parallel"),
        name="moe_combine",
    )(y0, y1, weights, x, gate.reshape(b, 1, d))


ROW_BLOCK_MULTIPLE = 8


def _routing_tables(idx_t, rank_t, counts, tile):
    top_k, t = idx_t.shape
    n_experts = counts.shape[0]
    n_assign = top_k * t
    padded = (counts + tile - 1) // tile * tile
    pend = jnp.cumsum(padded)
    pstart = pend - padded
    e_flat = idx_t.reshape(n_assign)
    onehot = e_flat[:, None] == jnp.arange(n_experts, dtype=jnp.int32)[None, :]
    slot_row = rank_t.reshape(n_assign) + jnp.sum(jnp.where(onehot, pstart[None, :], 0), axis=1,
                                                  dtype=jnp.int32)
    n_blocks = (n_assign + n_experts * (tile - 1) + tile - 1) // tile
    n_blocks = -(-n_blocks // ROW_BLOCK_MULTIPLE) * ROW_BLOCK_MULTIPLE
    first_row = jnp.arange(n_blocks, dtype=jnp.int32) * tile
    block_e = jnp.minimum(jnp.sum(pend[None, :] <= first_row[:, None], axis=1), n_experts - 1)
    block_e = block_e.astype(jnp.int32)
    n_valid = jnp.clip(counts[block_e] - (first_row - pstart[block_e]), 0, tile).astype(jnp.int32)
    n_used = (pend[-1] // tile).astype(jnp.int32).reshape(1)
    return slot_row, block_e, n_valid, n_used, n_blocks * tile


def _moe(x, gain, shift, scale, gate, rw, rb, wgu, bgu, wd, bd, layer, seq):
    t, d = x.shape
    n_e = rw.shape[1]
    rw_pad = jnp.pad(rw, ((0, 0), (0, LANES - n_e)))
    rw_hi = rw_pad.astype(BF16)
    rw_lo = (rw_pad - rw_hi.astype(F32)).astype(BF16)
    rw_pad = jnp.concatenate([rw_hi, rw_hi, rw_lo], axis=0)
    rb_pad = jnp.pad(rb, (0, LANES - n_e), constant_values=MASK_VALUE).reshape(1, LANES)
    h0, h1, idx_t, rank_t, weights, counts = _moe_route(x, gain, shift, scale, rw_pad, rb_pad, seq)
    tile = min(EXPERT_TILE, t * TOP_K // n_e)
    slot_row, block_e, n_valid, n_used, n_rows = _routing_tables(
        idx_t[:TOP_K], rank_t[:TOP_K], counts[0, :n_e], tile)
    rows0, rows1 = _sc_scatter_rows(h0, h1, slot_row, n_rows)
    ys0, ys1 = _expert_mlp(block_e, n_used, n_valid, rows0, rows1, wgu, bgu, wd, bd, layer, tile)
    y0, y1 = _sc_gather_rows(ys0, ys1, slot_row)
    return _combine(y0.reshape(TOP_K, t, d // 4), y1.reshape(TOP_K, t, d // 4), weights, x, gate, seq)


def _rmsnorm_kernel(x_ref, g_ref, o_ref):
    x = x_ref[...]
    ms = jnp.mean(x * x, axis=-1, keepdims=True)
    o_ref[...] = x * lax.rsqrt(ms + EPS) * g_ref[...]


def _rmsnorm(x, gain):
    t, d = x.shape
    tm = min(TOKEN_TILE, t)
    return pl.pallas_call(
        _rmsnorm_kernel,
        out_shape=jax.ShapeDtypeStruct((t, d), F32),
        grid=(t // tm,),
        in_specs=[pl.BlockSpec((tm, d), lambda i: (i, 0)),
                  pl.BlockSpec((1, d), lambda i: (0, 0))],
        out_specs=pl.BlockSpec((tm, d), lambda i: (i, 0)),
        compiler_params=_params("parallel"),
        name="final_rmsnorm",
    )(x, gain.reshape(1, d))


def kernel(x, c, ada_w, ada_b, norm_mix, norm_moe, a_wqkv, a_wo, a_lambda, a_subln, kv_norm,
           kv_ada_w, kv_ada_b, kv_w, b_wq, b_wo, router_w, router_b, w_gate_up, b_gate_up,
           w_down, b_down, final_norm):
    b, s, d = x.shape
    depth = ada_w.shape[0]
    n_a = a_wqkv.shape[0]
    xt = x.reshape(b * s, d)

    c_pad = jnp.pad(c, ((0, SUBLANES - b), (0, 0)))
    mod = _modulation(c_pad, ada_w, ada_b)[:, :b]
    kv_mod = _modulation(c_pad, kv_ada_w[None], kv_ada_b[None])[0, :b]

    for l in range(depth):
        sh1, sc1, g1, sh2, sc2, g2 = jnp.split(mod[l], 6, axis=-1)
        if l < n_a:
            qkv = _norm_linear(xt, norm_mix[l], sh1, sc1, a_wqkv[l].astype(BF16), s)
            o = _diff_attention(qkv, a_lambda[l], a_subln[l], b, s, l)
            xt = _proj_residual(o, a_wo[l].astype(BF16), xt, g1, s)
        else:
            if l == n_a:
                kv_sh, kv_sc = jnp.split(kv_mod, 2, axis=-1)
                kv = _norm_linear(xt, kv_norm, kv_sh, kv_sc, kv_w.astype(BF16), s)
            j = l - n_a
            q = _norm_linear(xt, norm_mix[l], sh1, sc1, b_wq[j].astype(BF16), s)
            o = _stick_attention(q, kv, b, s)
            xt = _proj_residual(o, b_wo[j].astype(BF16), xt, g1, s)
        xt = _moe(xt, norm_moe[l], sh2, sc2, g2, router_w[l], router_b[l],
                  w_gate_up, b_gate_up, w_down, b_down, l, s)
    return _rmsnorm(xt, final_norm).reshape(b, s, d)
```

```python
import functools
import math

import jax
import jax.numpy as jnp
import numpy as np
from jax import lax
from jax.experimental import pallas as pl
from jax.experimental.pallas import tpu as pltpu
from jax.experimental.pallas import tpu_sc as plsc

F32 = jnp.float32
BF16 = jnp.bfloat16
EPS = 1e-6
MASK_VALUE = -1e30
TOP_K = 4
SWIGLU_ALPHA = 1.702
SWIGLU_LIMIT = 7.0
LOG2E = math.log2(math.e)
UNDERFLOW_BITS = 160.0
NORM_SLACK = 1.03

LANES = 128
SUBLANES = 8
HEAD_DIM = 64
V7X_VMEM_BYTES = 64 * 2 ** 20
VMEM_LIMIT = V7X_VMEM_BYTES * 7 // 8

TOKEN_TILE = 512
ATTN_TILE = 512
STICK_TILE = 256
DIFF_SUBTILES = 2
STICK_SUBTILES = 4
EXPERT_TILE = 512


def _params(*semantics):
    return pltpu.CompilerParams(dimension_semantics=semantics, vmem_limit_bytes=VMEM_LIMIT)


def _largest_tile(n, cap):
    t = min(n, cap) // LANES * LANES
    while n % t:
        t -= LANES
    return t


def _adaln(x, gain, shift, scale):
    ms = jnp.mean(x * x, axis=-1, keepdims=True)
    return x * lax.rsqrt(ms + EPS) * gain * (1.0 + scale) + shift


def _pack_bf16_pair(lo, hi):
    lo_bits = lax.bitcast_convert_type(lo.astype(BF16).astype(F32), jnp.uint32) >> 16
    hi_bits = lax.bitcast_convert_type(hi.astype(BF16).astype(F32), jnp.uint32) & jnp.uint32(0xFFFF0000)
    return lo_bits | hi_bits


def _unpack_bf16_pair(packed):
    lo = lax.bitcast_convert_type(packed << 16, F32)
    hi = lax.bitcast_convert_type(packed & jnp.uint32(0xFFFF0000), F32)
    return lo, hi


def _pack_rows(x):
    q = x.shape[1] // 4
    return (_pack_bf16_pair(x[:, :q], x[:, 2 * q:3 * q]),
            _pack_bf16_pair(x[:, q:2 * q], x[:, 3 * q:]))


def _unpack_rows(p0, p1):
    a0, a2 = _unpack_bf16_pair(p0)
    a1, a3 = _unpack_bf16_pair(p1)
    return a0, a1, a2, a3


def _mod_kernel(c_ref, w_ref, b_ref, o_ref):
    c = c_ref[...]
    a = c / (1.0 + jnp.exp(-c))
    o_ref[...] = jnp.dot(a, w_ref[...], precision=lax.Precision.HIGHEST,
                         preferred_element_type=F32) + b_ref[...]


def _modulation(c_pad, w, b):
    n_l, d, n = w.shape
    tn = _largest_tile(n, 2048)
    return pl.pallas_call(
        _mod_kernel,
        out_shape=jax.ShapeDtypeStruct((n_l, SUBLANES, n), F32),
        grid=(n_l, n // tn),
        in_specs=[
            pl.BlockSpec((SUBLANES, d), lambda l, j: (0, 0)),
            pl.BlockSpec((None, d, tn), lambda l, j: (l, 0, j)),
            pl.BlockSpec((None, 1, tn), lambda l, j: (l, 0, j)),
        ],
        out_specs=pl.BlockSpec((None, SUBLANES, tn), lambda l, j: (l, 0, j)),
        compiler_params=_params("parallel", "parallel"),
        name="modulation",
    )(c_pad, w, b.reshape(n_l, 1, n))


def _norm_linear_kernel(x_ref, g_ref, sh_ref, sc_ref, w_ref, o_ref):
    h = _adaln(x_ref[...], g_ref[...], sh_ref[...], sc_ref[...])
    o_ref[...] = jnp.dot(h.astype(BF16), w_ref[...], preferred_element_type=F32).astype(o_ref.dtype)


def _norm_linear(x, gain, shift, scale, w, seq):
    t, d = x.shape
    n = w.shape[1]
    tm = min(TOKEN_TILE, seq)
    per_seq = seq // tm
    b = shift.shape[0]
    return pl.pallas_call(
        _norm_linear_kernel,
        out_shape=jax.ShapeDtypeStruct((t, n), BF16),
        grid=(t // tm,),
        in_specs=[
            pl.BlockSpec((tm, d), lambda i: (i, 0)),
            pl.BlockSpec((1, d), lambda i: (0, 0)),
            pl.BlockSpec((None, 1, d), lambda i: (i // per_seq, 0, 0)),
            pl.BlockSpec((None, 1, d), lambda i: (i // per_seq, 0, 0)),
            pl.BlockSpec((d, n), lambda i: (0, 0)),
        ],
        out_specs=pl.BlockSpec((tm, n), lambda i: (i, 0)),
        compiler_params=_params("parallel"),
        name="norm_linear",
    )(x, gain.reshape(1, d), shift.reshape(b, 1, d), scale.reshape(b, 1, d), w)


def _proj_residual_kernel(a_ref, w_ref, x_ref, g_ref, o_ref):
    y = jnp.dot(a_ref[...], w_ref[...], preferred_element_type=F32)
    o_ref[...] = x_ref[...] + g_ref[...] * y


def _proj_residual(a, w, x, gate, seq):
    t, d = x.shape
    k = a.shape[1]
    tm = min(TOKEN_TILE, seq)
    per_seq = seq // tm
    b = gate.shape[0]
    return pl.pallas_call(
        _proj_residual_kernel,
        out_shape=jax.ShapeDtypeStruct((t, d), F32),
        grid=(t // tm,),
        in_specs=[
            pl.BlockSpec((tm, k), lambda i: (i, 0)),
            pl.BlockSpec((k, d), lambda i: (0, 0)),
            pl.BlockSpec((tm, d), lambda i: (i, 0)),
            pl.BlockSpec((None, 1, d), lambda i: (i // per_seq, 0, 0)),
        ],
        out_specs=pl.BlockSpec((tm, d), lambda i: (i, 0)),
        compiler_params=_params("parallel"),
        name="proj_residual",
    )(a, w, x, gate.reshape(b, 1, d))


def _scaled_halves(q):
    q = (q.astype(F32) * (HEAD_DIM ** -0.5 * LOG2E)).astype(BF16)
    lane = lax.broadcasted_iota(jnp.int32, q.shape, 1)
    zero = jnp.zeros_like(q)
    return jnp.concatenate([jnp.where(lane < HEAD_DIM, q, zero),
                            jnp.where(lane >= HEAD_DIM, q, zero)], axis=0)


def _dot_nt(a, b):
    return lax.dot_general(a, b, (((1,), (1,)), ((), ())), preferred_element_type=F32)


def _tile_row_col(tq, tk):
    row = lax.broadcasted_iota(jnp.int32, (2 * tq, tk), 0)
    row = jnp.where(row >= tq, row - tq, row)
    col = lax.broadcasted_iota(jnp.int32, (2 * tq, tk), 1)
    return row, col


def _lane_chunks(n):
    return [slice(c * LANES, (c + 1) * LANES) for c in range(n // LANES)]


def _diff_attn_kernel(q_ref, k_ref, v_ref, slope_ref, lam_ref, subln_ref, o_ref,
                      m_ref, l_ref, acc_ref, sa_ref, sb_ref, kmax_ref, *, tile, subtiles, lam_init):
    step = pl.program_id(2)

    def one_tile(sub, carry):
        i = step * subtiles + sub
        rows = pl.ds(pl.multiple_of(sub * tile, tile), tile)
        qq = _scaled_halves(q_ref[rows, :])
        slope = slope_ref[...]
        col_i = lax.broadcasted_iota(jnp.int32, (1, tile), 1)
        ones = jnp.ones((tile, LANES), BF16)
        m_ref[...] = jnp.full(m_ref.shape, MASK_VALUE, F32)
        l_ref[...] = jnp.zeros(l_ref.shape, F32)
        acc_ref[...] = jnp.zeros(acc_ref.shape, F32)

        def scores(s_ref, kb):
            k = k_ref[pl.ds(pl.multiple_of(kb * tile, tile), tile), :]
            s_ref[...] = _dot_nt(qq, k) + slope * (col_i + (kb - i) * tile).astype(F32)

        def accumulate(s_ref, kb, diagonal):
            v = v_ref[pl.ds(pl.multiple_of(kb * tile, tile), tile), :]
            if diagonal:
                row, col = _tile_row_col(tile, tile)
                s_ref[...] = jnp.where(col <= row, s_ref[...], MASK_VALUE)
            m_prev = m_ref[...]
            m_new = jnp.maximum(m_prev, jnp.max(s_ref[...], axis=-1, keepdims=True))
            alpha = jnp.exp2(m_prev - m_new)
            p = jnp.concatenate([jnp.exp2(s_ref[:, sl] - m_new).astype(BF16) for sl in _lane_chunks(tile)],
                                axis=1)
            pv = jnp.dot(p, jnp.concatenate([v, ones], axis=1),
                         preferred_element_type=F32)
            acc_ref[...] = alpha * acc_ref[...] + pv[:, :LANES]
            l_ref[...] = alpha * l_ref[...] + pv[:, LANES:]
            m_ref[...] = m_new

        lane = lax.broadcasted_iota(jnp.int32, (1, LANES), 1)
        half_ones = (lax.broadcasted_iota(jnp.int32, (LANES, LANES), 0) < HEAD_DIM) == (
            lax.broadcasted_iota(jnp.int32, (LANES, LANES), 1) < HEAD_DIM)
        half_ones = jnp.where(half_ones, 1.0, 0.0).astype(BF16)

        @pl.when(i == 0)
        def _():
            def chunk_max(c, best):
                k = k_ref[pl.ds(pl.multiple_of(c * tile, tile), tile), :].astype(F32)
                n2 = jnp.dot((k * k).astype(BF16), half_ones, preferred_element_type=F32)
                return jnp.maximum(best, jnp.max(n2, axis=0, keepdims=True))

            best = lax.fori_loop(0, k_ref.shape[0] // tile, chunk_max, jnp.zeros((1, LANES), F32))
            kmax_ref[...] = jnp.broadcast_to(best * NORM_SLACK, kmax_ref.shape)

        qf = qq.astype(F32)
        qn2 = jnp.dot((qf * qf).astype(BF16), jnp.ones((LANES, LANES), BF16), preferred_element_type=F32)
        q_half = jnp.where(lane < HEAD_DIM, jnp.max(qn2[:tile], axis=0, keepdims=True),
                           jnp.max(qn2[tile:], axis=0, keepdims=True))
        raw_bound = jnp.sqrt(jnp.max(q_half * NORM_SLACK * kmax_ref[0:1, :], axis=1, keepdims=True))

        def live(kb):
            far = jnp.zeros((1, LANES), jnp.int32) + ((kb + 1 - i) * tile - 1)
            best_left = raw_bound + slope[:, :LANES] * far.astype(F32)
            lowest_max = jnp.min(m_ref[...], axis=0, keepdims=True)
            return jnp.max(best_left - lowest_max) > -UNDERFLOW_BITS

        scores(sa_ref, i)

        @pl.when(i == 0)
        def _():
            accumulate(sa_ref, 0, True)

        @pl.when(i == 1)
        def _():
            scores(sb_ref, 0)
            accumulate(sa_ref, 1, True)
            accumulate(sb_ref, 0, False)

        @pl.when(i >= 2)
        def _():
            scores(sb_ref, i - 1)
            accumulate(sa_ref, i, True)
            scores(sa_ref, i - 2)
            accumulate(sb_ref, i - 1, False)
            n_pairs = (i - 2) // 2

            def cond(carry):
                j, go = carry
                return jnp.logical_and(j < n_pairs, go)

            def body(carry):
                j, _ = carry
                kb = i - 2 - 2 * j
                scores(sb_ref, kb - 1)
                accumulate(sa_ref, kb, False)
                scores(sa_ref, kb - 2)
                accumulate(sb_ref, kb - 1, False)
                return j + 1, live(kb - 2)

            _, go = lax.while_loop(cond, body, (jnp.int32(0), live(i - 2)))

            @pl.when(jnp.logical_and(go, i % 2 == 0))
            def _():
                accumulate(sa_ref, 0, False)

            @pl.when(jnp.logical_and(go, i % 2 == 1))
            def _():
                scores(sb_ref, 0)
                accumulate(sa_ref, 1, False)
                accumulate(sb_ref, 0, False)

        o_all = acc_ref[...] / l_ref[...]
        lv = lam_ref[...]
        lam = (jnp.exp(jnp.sum(lv[0:1] * lv[1:2], axis=1, keepdims=True))
               - jnp.exp(jnp.sum(lv[2:3] * lv[3:4], axis=1, keepdims=True)) + lam_init)
        o = o_all[:tile] - lam * o_all[tile:]
        ms = jnp.mean(o * o, axis=-1, keepdims=True)
        o = o * lax.rsqrt(ms + EPS) * subln_ref[...] * (1.0 - lam_init)
        o_ref[rows, :] = o.astype(o_ref.dtype)
        return carry

    lax.fori_loop(0, subtiles, one_tile, 0)


def _diff_attention(qkv, lam_vecs, subln, batch, seq, layer_idx):
    t = qkv.shape[0]
    d = qkv.shape[1] // 3
    heads = d // LANES
    tile = min(ATTN_TILE, seq)
    subtiles = min(DIFF_SUBTILES, seq // tile)
    nq = seq // (tile * subtiles)
    lam_init = 0.8 - 0.6 * math.exp(-0.3 * layer_idx)
    slopes = np.array([2.0 ** (-8.0 * (h + 1) / heads) * LOG2E for h in range(heads)], np.float32)
    slopes = jnp.asarray(np.broadcast_to(slopes[:, None, None], (heads, 1, tile)))
    kernel = functools.partial(_diff_attn_kernel, tile=tile, subtiles=subtiles, lam_init=lam_init)
    return pl.pallas_call(
        kernel,
        out_shape=jax.ShapeDtypeStruct((t, d), BF16),
        grid=(batch, heads, nq),
        in_specs=[
            pl.BlockSpec((tile * subtiles, LANES), lambda b, h, i: (b * nq + i, h)),
            pl.BlockSpec((seq, LANES), lambda b, h, i: (b, heads + h)),
            pl.BlockSpec((seq, LANES), lambda b, h, i: (b, 2 * heads + h)),
            pl.BlockSpec((None, 1, tile), lambda b, h, i: (h, 0, 0)),
            pl.BlockSpec(lam_vecs.shape, lambda b, h, i: (0, 0)),
            pl.BlockSpec((1, LANES), lambda b, h, i: (0, 0)),
        ],
        out_specs=pl.BlockSpec((tile * subtiles, LANES), lambda b, h, i: (b * nq + i, h)),
        scratch_shapes=[
            pltpu.VMEM((2 * tile, LANES), F32),
            pltpu.VMEM((2 * tile, LANES), F32),
            pltpu.VMEM((2 * tile, LANES), F32),
            pltpu.VMEM((2 * tile, tile), F32),
            pltpu.VMEM((2 * tile, tile), F32),
            pltpu.VMEM((SUBLANES, LANES), F32),
        ],
        compiler_params=_params("parallel", "parallel", "arbitrary"),
        name="diff_attention",
    )(qkv, qkv, qkv, slopes, lam_vecs, subln.reshape(1, LANES))


CHUNK = 256


def _suffix_sum_matrix():
    j = np.arange(CHUNK)
    return jnp.asarray((j[:, None] >= j[None, :]).astype(np.float32), dtype=BF16)


def _stick_attn_kernel(q_ref, k_ref, v_ref, mt_ref, o_ref, rem_ref, acc_ref, za_ref, zb_ref, *, tile, subtiles):
    step = pl.program_id(2)

    def one_tile(sub, carry):
        i = step * subtiles + sub
        rows = pl.ds(pl.multiple_of(sub * tile, tile), tile)
        qq = _scaled_halves(q_ref[rows, :])
        mt = mt_ref[...]
        rem_ref[...] = jnp.zeros(rem_ref.shape, F32)
        acc_ref[...] = jnp.zeros(acc_ref.shape, F32)

        def scores(z_ref, kb):
            z_ref[...] = _dot_nt(qq, k_ref[pl.ds(pl.multiple_of(kb * tile, tile), tile), :])

        def accumulate(z_ref, kb, diagonal):
            v = v_ref[pl.ds(pl.multiple_of(kb * tile, tile), tile), :]
            rem = rem_ref[...]
            chunks = []
            for c in reversed(range(tile // CHUNK)):
                z = z_ref[:, c * CHUNK:(c + 1) * CHUNK]
                softplus = jnp.maximum(z, 0.0) + jnp.log2(1.0 + jnp.exp2(-jnp.abs(z)))
                if diagonal:
                    row = lax.broadcasted_iota(jnp.int32, z.shape, 0)
                    row = jnp.where(row >= tile, row - tile, row)
                    strict = lax.broadcasted_iota(jnp.int32, z.shape, 1) + c * CHUNK < row
                    softplus = jnp.where(strict, softplus, 0.0)
                sums = jnp.dot(softplus.astype(BF16), mt, preferred_element_type=F32)
                for h in reversed(range(CHUNK // LANES)):
                    sl = slice(h * LANES, (h + 1) * LANES)
                    a = jnp.exp2(z_ref[:, c * CHUNK + h * LANES:c * CHUNK + (h + 1) * LANES] - sums[:, sl] - rem)
                    if diagonal:
                        a = jnp.where(strict[:, sl], a, 0.0)
                    chunks.append(a.astype(BF16))
                rem = rem + jnp.broadcast_to(sums[:, 0:1], rem.shape)
            rem_ref[...] = rem
            acc_ref[...] += jnp.dot(jnp.concatenate(chunks[::-1], axis=1), v, preferred_element_type=F32)

        def live():
            return jnp.min(rem_ref[...]) < UNDERFLOW_BITS

        scores(za_ref, i)

        @pl.when(i == 0)
        def _():
            accumulate(za_ref, 0, True)

        @pl.when(i == 1)
        def _():
            scores(zb_ref, 0)
            accumulate(za_ref, 1, True)
            accumulate(zb_ref, 0, False)

        @pl.when(i >= 2)
        def _():
            scores(zb_ref, i - 1)
            accumulate(za_ref, i, True)
            scores(za_ref, i - 2)
            accumulate(zb_ref, i - 1, False)
            n_pairs = (i - 2) // 2

            def cond(carry):
                j, go = carry
                return jnp.logical_and(j < n_pairs, go)

            def body(carry):
                j, _ = carry
                kb = i - 2 - 2 * j
                scores(zb_ref, kb - 1)
                accumulate(za_ref, kb, False)
                scores(za_ref, kb - 2)
                accumulate(zb_ref, kb - 1, False)
                return j + 1, live()

            _, go = lax.while_loop(cond, body, (jnp.int32(0), live()))

            @pl.when(jnp.logical_and(go, i % 2 == 0))
            def _():
                accumulate(za_ref, 0, False)

            @pl.when(jnp.logical_and(go, i % 2 == 1))
            def _():
                scores(zb_ref, 0)
                accumulate(za_ref, 1, False)
                accumulate(zb_ref, 0, False)

        acc = acc_ref[...]
        lane = lax.broadcasted_iota(jnp.int32, (tile, LANES), 1)
        o_ref[rows, :] = jnp.where(lane < HEAD_DIM, acc[:tile], acc[tile:]).astype(o_ref.dtype)
        return carry

    lax.fori_loop(0, subtiles, one_tile, 0)


def _stick_attention(q, kv, batch, seq):
    t, d = q.shape
    pairs = d // LANES
    tile = min(STICK_TILE, seq)
    subtiles = min(STICK_SUBTILES, seq // tile)
    nq = seq // (tile * subtiles)
    kernel = functools.partial(_stick_attn_kernel, tile=tile, subtiles=subtiles)
    return pl.pallas_call(
        kernel,
        out_shape=jax.ShapeDtypeStruct((t, d), BF16),
        grid=(batch, pairs, nq),
        in_specs=[
            pl.BlockSpec((tile * subtiles, LANES), lambda b, j, i: (b * nq + i, j)),
            pl.BlockSpec((seq, LANES), lambda b, j, i: (b, j)),
            pl.BlockSpec((seq, LANES), lambda b, j, i: (b, pairs + j)),
            pl.BlockSpec((CHUNK, CHUNK), lambda b, j, i: (0, 0)),
        ],
        out_specs=pl.BlockSpec((tile * subtiles, LANES), lambda b, j, i: (b * nq + i, j)),
        scratch_shapes=[
            pltpu.VMEM((2 * tile, LANES), F32),
            pltpu.VMEM((2 * tile, LANES), F32),
            pltpu.VMEM((2 * tile, tile), F32),
            pltpu.VMEM((2 * tile, tile), F32),
        ],
        compiler_params=_params("parallel", "parallel", "arbitrary"),
        name="stick_attention",
    )(q, kv, kv, _suffix_sum_matrix())


SC_WINDOW = 128


def _sc_gather_rows(tab0, tab1, idx):
    m = idx.shape[0]
    w = tab0.shape[1]
    mesh = plsc.VectorSubcoreMesh(core_axis_name="core", subcore_axis_name="subcore")
    assert m % (SC_WINDOW * mesh.num_cores * mesh.num_subcores) == 0, m
    out = jax.ShapeDtypeStruct((m, w), tab0.dtype)

    @pl.kernel(out_type=(out, out), mesh=mesh)
    def gather(t0_hbm, t1_hbm, i_hbm, o0_hbm, o1_hbm):
        for t_hbm, o_hbm in ((t0_hbm, o0_hbm), (t1_hbm, o1_hbm)):
            def body(i_vmem, o_vmem, t_hbm=t_hbm):
                pltpu.sync_copy(t_hbm.at[i_vmem.at[0]], o_vmem)

            pltpu.emit_pipeline(
                body,
                grid=(m // SC_WINDOW,),
                in_specs=[pl.BlockSpec((1, SC_WINDOW), lambda i: (0, i))],
                out_specs=[pl.BlockSpec((SC_WINDOW, w), lambda i: (i, 0))],
                core_axis_name=("core", "subcore"),
                dimension_semantics=(pltpu.PARALLEL,),
            )(i_hbm, o_hbm)

    return gather(tab0, tab1, idx.reshape(1, m))


def _sc_scatter_rows(tab0, tab1, idx, n_rows):
    m = idx.shape[0]
    t, w = tab0.shape
    mesh = plsc.VectorSubcoreMesh(core_axis_name="core", subcore_axis_name="subcore")
    assert m % (SC_WINDOW * mesh.num_cores * mesh.num_subcores) == 0 and t % SC_WINDOW == 0, (m, t)
    windows_per_pass = t // SC_WINDOW
    out = jax.ShapeDtypeStruct((n_rows, w), tab0.dtype)

    @pl.kernel(out_type=(out, out), mesh=mesh)
    def scatter(t0_hbm, t1_hbm, i_hbm, o0_hbm, o1_hbm):
        for t_hbm, o_hbm in ((t0_hbm, o0_hbm), (t1_hbm, o1_hbm)):
            def body(x_vmem, i_vmem, o_hbm=o_hbm):
                pltpu.sync_copy(x_vmem, o_hbm.at[i_vmem.at[0]])

            pltpu.emit_pipeline(
                body,
                grid=(m // SC_WINDOW,),
                in_specs=[pl.BlockSpec((SC_WINDOW, w), lambda i: (i % windows_per_pass, 0)),
                          pl.BlockSpec((1, SC_WINDOW), lambda i: (0, i))],
                out_specs=[],
                core_axis_name=("core", "subcore"),
                dimension_semantics=(pltpu.PARALLEL,),
            )(t_hbm, i_hbm)

    return scatter(tab0, tab1, idx.reshape(1, m))


def _moe_route_kernel(x_ref, g_ref, sh_ref, sc_ref, rw_ref, rb_ref,
                      h0_ref, h1_ref, idx_ref, rank_ref, gate_ref, count_ref, run_ref):
    @pl.when(pl.program_id(0) == 0)
    def _():
        run_ref[...] = jnp.zeros(run_ref.shape, F32)

    h = _adaln(x_ref[...], g_ref[...], sh_ref[...], sc_ref[...])
    h0_ref[...], h1_ref[...] = _pack_rows(h)
    h_hi = h.astype(BF16)
    h_lo = (h - h_hi.astype(F32)).astype(BF16)
    logits = jnp.dot(jnp.concatenate([h_hi, h_lo, h_hi], axis=1), rw_ref[...],
                     preferred_element_type=F32) + rb_ref[...]
    lane = lax.broadcasted_iota(jnp.int32, logits.shape, 1)
    lane_f = lane.astype(F32)
    vals, ids = [], []
    for _ in range(TOP_K):
        m = jnp.max(logits, axis=-1, keepdims=True)
        first = jnp.min(jnp.where(logits == m, lane_f, float(LANES)), axis=-1, keepdims=True)
        vals.append(m)
        ids.append(first)
        logits = jnp.where(lane_f == first, -jnp.inf, logits)
    exps = [jnp.exp(v - vals[0]) for v in vals]
    denom = exps[0]
    for e in exps[1:]:
        denom = denom + e
    tm = logits.shape[0]
    earlier = (lax.broadcasted_iota(jnp.int32, (tm, tm), 1) < lax.broadcasted_iota(jnp.int32, (tm, tm), 0))
    earlier = jnp.where(earlier, 1.0, 0.0).astype(BF16)
    seen = run_ref[...]
    gate = jnp.zeros(logits.shape, F32)
    idx = jnp.zeros(logits.shape, F32)
    rank = jnp.zeros(logits.shape, F32)
    for k in range(TOP_K):
        onehot = jnp.where(lane_f == ids[k], 1.0, 0.0)
        before = jnp.dot(earlier, onehot.astype(BF16), preferred_element_type=F32) + seen
        rank = jnp.where(lane == k, jnp.sum(onehot * before, axis=-1, keepdims=True), rank)
        seen = seen + jnp.sum(onehot, axis=0, keepdims=True)
        gate = jnp.where(lane == k, exps[k] / denom, gate)
        idx = jnp.where(lane == k, ids[k], idx)
    run_ref[...] = seen
    count_ref[...] = jnp.broadcast_to(seen, count_ref.shape).astype(jnp.int32)
    gate_ref[...] = gate
    idx_ref[...] = idx.T[:SUBLANES].astype(jnp.int32)
    rank_ref[...] = rank.T[:SUBLANES].astype(jnp.int32)


def _moe_route(x, gain, shift, scale, rw_pad, rb_pad, seq):
    t, d = x.shape
    tm = min(TOKEN_TILE, seq)
    per_seq = seq // tm
    b = shift.shape[0]
    return pl.pallas_call(
        _moe_route_kernel,
        out_shape=(jax.ShapeDtypeStruct((t, d // 4), jnp.uint32),
                   jax.ShapeDtypeStruct((t, d // 4), jnp.uint32),
                   jax.ShapeDtypeStruct((SUBLANES, t), jnp.int32),
                   jax.ShapeDtypeStruct((SUBLANES, t), jnp.int32),
                   jax.ShapeDtypeStruct((t, LANES), F32),
                   jax.ShapeDtypeStruct((SUBLANES, LANES), jnp.int32)),
        grid=(t // tm,),
        in_specs=[
            pl.BlockSpec((tm, d), lambda i: (i, 0)),
            pl.BlockSpec((1, d), lambda i: (0, 0)),
            pl.BlockSpec((None, 1, d), lambda i: (i // per_seq, 0, 0)),
            pl.BlockSpec((None, 1, d), lambda i: (i // per_seq, 0, 0)),
            pl.BlockSpec((3 * d, LANES), lambda i: (0, 0)),
            pl.BlockSpec((1, LANES), lambda i: (0, 0)),
        ],
        out_specs=(pl.BlockSpec((tm, d // 4), lambda i: (i, 0)),
                   pl.BlockSpec((tm, d // 4), lambda i: (i, 0)),
                   pl.BlockSpec((SUBLANES, tm), lambda i: (0, i)),
                   pl.BlockSpec((SUBLANES, tm), lambda i: (0, i)),
                   pl.BlockSpec((tm, LANES), lambda i: (i, 0)),
                   pl.BlockSpec((SUBLANES, LANES), lambda i: (0, 0))),
        scratch_shapes=[pltpu.VMEM((1, LANES), F32)],
        compiler_params=_params("arbitrary"),
        name="moe_route",
    )(x, gain.reshape(1, d), shift.reshape(b, 1, d), scale.reshape(b, 1, d), rw_pad, rb_pad)


def _expert_kernel(block_e_ref, n_used_ref, n_valid_ref, h0_ref, h1_ref, wgu_ref, bgu_ref, wd_ref, bd_ref,
                   o0_ref, o1_ref, wgu_bf_ref, wd_bf_ref):
    i = pl.program_id(0)
    new_expert = jnp.logical_or(i == 0, block_e_ref[i] != block_e_ref[jnp.maximum(i - 1, 0)])

    @pl.when(jnp.logical_and(new_expert, i < n_used_ref[0]))
    def _():
        wgu_bf_ref[...] = wgu_ref[...].astype(BF16)
        wd_bf_ref[...] = wd_ref[...].astype(BF16)

    @pl.when(i < n_used_ref[0])
    def _():
        d_e = wd_ref.shape[0]
        written = lax.broadcasted_iota(jnp.int32, h0_ref.shape, 0) < n_valid_ref[i]
        zero = jnp.zeros(h0_ref.shape, h0_ref.dtype)
        quarters = _unpack_rows(jnp.where(written, h0_ref[...], zero), jnp.where(written, h1_ref[...], zero))
        h = jnp.concatenate([q.astype(BF16) for q in quarters], axis=1)
        gu = jnp.dot(h, wgu_bf_ref[...], preferred_element_type=F32) + bgu_ref[...]
        g = jnp.minimum(gu[:, :d_e], SWIGLU_LIMIT)
        u = jnp.clip(gu[:, d_e:], -SWIGLU_LIMIT, SWIGLU_LIMIT)
        act = g / (1.0 + jnp.exp(-SWIGLU_ALPHA * g)) * (u + 1.0)
        y = jnp.dot(act.astype(BF16), wd_bf_ref[...], preferred_element_type=F32) + bd_ref[...]
        o0_ref[...], o1_ref[...] = _pack_rows(y)

    @pl.when(i >= n_used_ref[0])
    def _():
        o0_ref[...] = jnp.zeros(o0_ref.shape, o0_ref.dtype)
        o1_ref[...] = jnp.zeros(o1_ref.shape, o1_ref.dtype)


def _expert_mlp(block_e, n_used, n_valid, rows0, rows1, wgu, bgu, wd, bd, layer, tile):
    n_rows, dq = rows0.shape
    d = 4 * dq
    n_l, n_e, _, two_de = wgu.shape
    d_e = two_de // 2
    n_blocks = n_rows // tile
    grid_spec = pltpu.PrefetchScalarGridSpec(
        num_scalar_prefetch=3,
        grid=(n_blocks,),
        in_specs=[
            pl.BlockSpec((tile, dq), lambda i, be, nu, nv: (i, 0)),
            pl.BlockSpec((tile, dq), lambda i, be, nu, nv: (i, 0)),
            pl.BlockSpec((None, None, d, two_de), lambda i, be, nu, nv: (layer, be[i], 0, 0)),
            pl.BlockSpec((None, None, 1, two_de), lambda i, be, nu, nv: (layer, be[i], 0, 0)),
            pl.BlockSpec((None, None, d_e, d), lambda i, be, nu, nv: (layer, be[i], 0, 0)),
            pl.BlockSpec((None, None, 1, d), lambda i, be, nu, nv: (layer, be[i], 0, 0)),
        ],
        out_specs=(pl.BlockSpec((tile, dq), lambda i, be, nu, nv: (i, 0)),
                   pl.BlockSpec((tile, dq), lambda i, be, nu, nv: (i, 0))),
        scratch_shapes=[pltpu.VMEM((d, two_de), BF16), pltpu.VMEM((d_e, d), BF16)],
    )
    packed = jax.ShapeDtypeStruct((n_rows, dq), jnp.uint32)
    return pl.pallas_call(
        _expert_kernel,
        out_shape=(packed, packed),
        grid_spec=grid_spec,
        compiler_params=_params("arbitrary"),
        name="expert_mlp",
    )(block_e, n_used, n_valid, rows0, rows1, wgu, bgu.reshape(n_l, n_e, 1, two_de), wd,
      bd.reshape(n_l, n_e, 1, d))


def _combine_kernel(y0_ref, y1_ref, w_ref, x_ref, g_ref, n_ref, o_ref, *, normalize):
    w = w_ref[...]
    acc = None
    for k in range(y0_ref.shape[0]):
        quarters = _unpack_rows(y0_ref[k], y1_ref[k])
        scaled = [w[:, k:k + 1] * q for q in quarters]
        acc = scaled if acc is None else [a + s for a, s in zip(acc, scaled)]
    x = x_ref[...] + g_ref[...] * jnp.concatenate(acc, axis=1)
    if normalize:
        x = x * lax.rsqrt(jnp.mean(x * x, axis=-1, keepdims=True) + EPS) * n_ref[...]
    o_ref[...] = x


def _combine(y0, y1, weights, x, gate, seq, final_gain=None):
    t, d = x.shape
    tm = min(TOKEN_TILE, seq)
    per_seq = seq // tm
    b = gate.shape[0]
    norm_gain = jnp.ones((1, d), F32) if final_gain is None else final_gain.reshape(1, d)
    return pl.pallas_call(
        functools.partial(_combine_kernel, normalize=final_gain is not None),
        out_shape=jax.ShapeDtypeStruct((t, d), F32),
        grid=(t // tm,),
        in_specs=[
            pl.BlockSpec((y0.shape[0], tm, d // 4), lambda i: (0, i, 0)),
            pl.BlockSpec((y0.shape[0], tm, d // 4), lambda i: (0, i, 0)),
            pl.BlockSpec((tm, LANES), lambda i: (i, 0)),
            pl.BlockSpec((tm, d), lambda i: (i, 0)),
            pl.BlockSpec((None, 1, d), lambda i: (i // per_seq, 0, 0)),
            pl.BlockSpec((1, d), lambda i: (0, 0)),
        ],
        out_specs=pl.BlockSpec((tm, d), lambda i: (i, 0)),
        compiler_params=_params("parallel"),
        name="moe_combine",
    )(y0, y1, weights, x, gate.reshape(b, 1, d), norm_gain)


ROW_BLOCK_MULTIPLE = 8


def _routing_tables(idx_t, rank_t, counts, tile):
    top_k, t = idx_t.shape
    n_experts = counts.shape[0]
    n_assign = top_k * t
    padded = (counts + tile - 1) // tile * tile
    pend = jnp.cumsum(padded)
    pstart = pend - padded
    e_flat = idx_t.reshape(n_assign)
    onehot = e_flat[:, None] == jnp.arange(n_experts, dtype=jnp.int32)[None, :]
    slot_row = rank_t.reshape(n_assign) + jnp.sum(jnp.where(onehot, pstart[None, :], 0), axis=1,
                                                  dtype=jnp.int32)
    n_blocks = (n_assign + n_experts * (tile - 1) + tile - 1) // tile
    n_blocks = -(-n_blocks // ROW_BLOCK_MULTIPLE) * ROW_BLOCK_MULTIPLE
    first_row = jnp.arange(n_blocks, dtype=jnp.int32) * tile
    block_e = jnp.minimum(jnp.sum(pend[None, :] <= first_row[:, None], axis=1), n_experts - 1)
    block_e = block_e.astype(jnp.int32)
    n_valid = jnp.clip(counts[block_e] - (first_row - pstart[block_e]), 0, tile).astype(jnp.int32)
    n_used = (pend[-1] // tile).astype(jnp.int32).reshape(1)
    return slot_row, block_e, n_valid, n_used, n_blocks * tile


def _moe(x, gain, shift, scale, gate, rw, rb, wgu, bgu, wd, bd, layer, seq, final_gain=None):
    t, d = x.shape
    n_e = rw.shape[1]
    rw_pad = jnp.pad(rw, ((0, 0), (0, LANES - n_e)))
    rw_hi = rw_pad.astype(BF16)
    rw_lo = (rw_pad - rw_hi.astype(F32)).astype(BF16)
    rw_pad = jnp.concatenate([rw_hi, rw_hi, rw_lo], axis=0)
    rb_pad = jnp.pad(rb, (0, LANES - n_e), constant_values=MASK_VALUE).reshape(1, LANES)
    h0, h1, idx_t, rank_t, weights, counts = _moe_route(x, gain, shift, scale, rw_pad, rb_pad, seq)
    tile = min(EXPERT_TILE, t * TOP_K // n_e)
    slot_row, block_e, n_valid, n_used, n_rows = _routing_tables(
        idx_t[:TOP_K], rank_t[:TOP_K], counts[0, :n_e], tile)
    rows0, rows1 = _sc_scatter_rows(h0, h1, slot_row, n_rows)
    ys0, ys1 = _expert_mlp(block_e, n_used, n_valid, rows0, rows1, wgu, bgu, wd, bd, layer, tile)
    y0, y1 = _sc_gather_rows(ys0, ys1, slot_row)
    return _combine(y0.reshape(TOP_K, t, d // 4), y1.reshape(TOP_K, t, d // 4), weights, x, gate, seq,
                    final_gain)


def kernel(x, c, ada_w, ada_b, norm_mix, norm_moe, a_wqkv, a_wo, a_lambda, a_subln, kv_norm,
           kv_ada_w, kv_ada_b, kv_w, b_wq, b_wo, router_w, router_b, w_gate_up, b_gate_up,
           w_down, b_down, final_norm):
    b, s, d = x.shape
    depth = ada_w.shape[0]
    n_a = a_wqkv.shape[0]
    xt = x.reshape(b * s, d)

    c_pad = jnp.pad(c, ((0, SUBLANES - b), (0, 0)))
    mod = _modulation(c_pad, ada_w, ada_b)[:, :b]
    kv_mod = _modulation(c_pad, kv_ada_w[None], kv_ada_b[None])[0, :b]

    for l in range(depth):
        sh1, sc1, g1, sh2, sc2, g2 = jnp.split(mod[l], 6, axis=-1)
        if l < n_a:
            qkv = _norm_linear(xt, norm_mix[l], sh1, sc1, a_wqkv[l].astype(BF16), s)
            o = _diff_attention(qkv, a_lambda[l], a_subln[l], b, s, l)
            xt = _proj_residual(o, a_wo[l].astype(BF16), xt, g1, s)
        else:
            if l == n_a:
                kv_sh, kv_sc = jnp.split(kv_mod, 2, axis=-1)
                kv = _norm_linear(xt, kv_norm, kv_sh, kv_sc, kv_w.astype(BF16), s)
            j = l - n_a
            q = _norm_linear(xt, norm_mix[l], sh1, sc1, b_wq[j].astype(BF16), s)
            o = _stick_attention(q, kv, b, s)
            xt = _proj_residual(o, b_wo[j].astype(BF16), xt, g1, s)
        xt = _moe(xt, norm_moe[l], sh2, sc2, g2, router_w[l], router_b[l],
                  w_gate_up, b_gate_up, w_down, b_down, l, s,
                  final_gain=final_norm if l == depth - 1 else None)
    return xt.reshape(b, s, d)
```

```python
import functools
import math

import jax
import jax.numpy as jnp
import numpy as np
from jax import lax
from jax.experimental import pallas as pl
from jax.experimental.pallas import tpu as pltpu
from jax.experimental.pallas import tpu_sc as plsc

F32 = jnp.float32
BF16 = jnp.bfloat16
EPS = 1e-6
MASK_VALUE = -1e30
TOP_K = 4
SWIGLU_ALPHA = 1.702
SWIGLU_LIMIT = 7.0
LOG2E = math.log2(math.e)
UNDERFLOW_BITS = 160.0
NORM_SLACK = 1.03

LANES = 128
SUBLANES = 8
HEAD_DIM = 64
V7X_VMEM_BYTES = 64 * 2 ** 20
VMEM_LIMIT = V7X_VMEM_BYTES * 7 // 8

TOKEN_TILE = 512
ATTN_TILE = 512
STICK_TILE = 256
DIFF_SUBTILES = 2
STICK_SUBTILES = 4
EXPERT_TILE = 512


def _params(*semantics):
    return pltpu.CompilerParams(dimension_semantics=semantics, vmem_limit_bytes=VMEM_LIMIT)


def _largest_tile(n, cap):
    t = min(n, cap) // LANES * LANES
    while n % t:
        t -= LANES
    return t


def _adaln(x, gain, shift, scale):
    ms = jnp.mean(x * x, axis=-1, keepdims=True)
    return x * lax.rsqrt(ms + EPS) * gain * (1.0 + scale) + shift


def _pack_bf16_pair(lo, hi):
    lo_bits = lax.bitcast_convert_type(lo.astype(BF16).astype(F32), jnp.uint32) >> 16
    hi_bits = lax.bitcast_convert_type(hi.astype(BF16).astype(F32), jnp.uint32) & jnp.uint32(0xFFFF0000)
    return lo_bits | hi_bits


def _unpack_bf16_pair(packed):
    lo = lax.bitcast_convert_type(packed << 16, F32)
    hi = lax.bitcast_convert_type(packed & jnp.uint32(0xFFFF0000), F32)
    return lo, hi


def _pack_rows(x):
    q = x.shape[1] // 4
    return (_pack_bf16_pair(x[:, :q], x[:, 2 * q:3 * q]),
            _pack_bf16_pair(x[:, q:2 * q], x[:, 3 * q:]))


def _unpack_rows(p0, p1):
    a0, a2 = _unpack_bf16_pair(p0)
    a1, a3 = _unpack_bf16_pair(p1)
    return a0, a1, a2, a3


def _mod_kernel(c_ref, w_ref, b_ref, o_ref):
    c = c_ref[...]
    a = c / (1.0 + jnp.exp(-c))
    o_ref[...] = jnp.dot(a, w_ref[...], precision=lax.Precision.HIGHEST,
                         preferred_element_type=F32) + b_ref[...]


def _modulation(c_pad, w, b):
    n_l, d, n = w.shape
    tn = _largest_tile(n, 2048)
    return pl.pallas_call(
        _mod_kernel,
        out_shape=jax.ShapeDtypeStruct((n_l, SUBLANES, n), F32),
        grid=(n_l, n // tn),
        in_specs=[
            pl.BlockSpec((SUBLANES, d), lambda l, j: (0, 0)),
            pl.BlockSpec((None, d, tn), lambda l, j: (l, 0, j)),
            pl.BlockSpec((None, 1, tn), lambda l, j: (l, 0, j)),
        ],
        out_specs=pl.BlockSpec((None, SUBLANES, tn), lambda l, j: (l, 0, j)),
        compiler_params=_params("parallel", "parallel"),
        name="modulation",
    )(c_pad, w, b.reshape(n_l, 1, n))


def _norm_linear_kernel(x_ref, g_ref, sh_ref, sc_ref, w_ref, o_ref):
    h = _adaln(x_ref[...], g_ref[...], sh_ref[...], sc_ref[...])
    o_ref[...] = jnp.dot(h.astype(BF16), w_ref[...], preferred_element_type=F32).astype(o_ref.dtype)


def _norm_linear(x, gain, shift, scale, w, seq):
    t, d = x.shape
    n = w.shape[1]
    tm = min(TOKEN_TILE, seq)
    per_seq = seq // tm
    b = shift.shape[0]
    return pl.pallas_call(
        _norm_linear_kernel,
        out_shape=jax.ShapeDtypeStruct((t, n), BF16),
        grid=(t // tm,),
        in_specs=[
            pl.BlockSpec((tm, d), lambda i: (i, 0)),
            pl.BlockSpec((1, d), lambda i: (0, 0)),
            pl.BlockSpec((None, 1, d), lambda i: (i // per_seq, 0, 0)),
            pl.BlockSpec((None, 1, d), lambda i: (i // per_seq, 0, 0)),
            pl.BlockSpec((d, n), lambda i: (0, 0)),
        ],
        out_specs=pl.BlockSpec((tm, n), lambda i: (i, 0)),
        compiler_params=_params("parallel"),
        name="norm_linear",
    )(x, gain.reshape(1, d), shift.reshape(b, 1, d), scale.reshape(b, 1, d), w)


def _proj_residual_kernel(a_ref, w_ref, x_ref, g_ref, o_ref):
    y = jnp.dot(a_ref[...], w_ref[...], preferred_element_type=F32)
    o_ref[...] = x_ref[...] + g_ref[...] * y


def _proj_residual(a, w, x, gate, seq):
    t, d = x.shape
    k = a.shape[1]
    tm = min(TOKEN_TILE, seq)
    per_seq = seq // tm
    b = gate.shape[0]
    return pl.pallas_call(
        _proj_residual_kernel,
        out_shape=jax.ShapeDtypeStruct((t, d), F32),
        grid=(t // tm,),
        in_specs=[
            pl.BlockSpec((tm, k), lambda i: (i, 0)),
            pl.BlockSpec((k, d), lambda i: (0, 0)),
            pl.BlockSpec((tm, d), lambda i: (i, 0)),
            pl.BlockSpec((None, 1, d), lambda i: (i // per_seq, 0, 0)),
        ],
        out_specs=pl.BlockSpec((tm, d), lambda i: (i, 0)),
        compiler_params=_params("parallel"),
        name="proj_residual",
    )(a, w, x, gate.reshape(b, 1, d))


def _scaled_halves(q):
    q = (q.astype(F32) * (HEAD_DIM ** -0.5 * LOG2E)).astype(BF16)
    lane = lax.broadcasted_iota(jnp.int32, q.shape, 1)
    zero = jnp.zeros_like(q)
    return jnp.concatenate([jnp.where(lane < HEAD_DIM, q, zero),
                            jnp.where(lane >= HEAD_DIM, q, zero)], axis=0)


def _dot_nt(a, b):
    return lax.dot_general(a, b, (((1,), (1,)), ((), ())), preferred_element_type=F32)


def _tile_row_col(tq, tk):
    row = lax.broadcasted_iota(jnp.int32, (2 * tq, tk), 0)
    row = jnp.where(row >= tq, row - tq, row)
    col = lax.broadcasted_iota(jnp.int32, (2 * tq, tk), 1)
    return row, col


def _lane_chunks(n):
    return [slice(c * LANES, (c + 1) * LANES) for c in range(n // LANES)]


def _diff_attn_kernel(q_ref, k_ref, v_ref, slope_ref, lam_ref, subln_ref, o_ref,
                      m_ref, l_ref, acc_ref, sa_ref, sb_ref, kmax_ref, *, tile, subtiles, lam_init):
    step = pl.program_id(2)

    def one_tile(sub, carry):
        i = step * subtiles + sub
        rows = pl.ds(pl.multiple_of(sub * tile, tile), tile)
        qq = _scaled_halves(q_ref[rows, :])
        slope = slope_ref[...]
        col_i = lax.broadcasted_iota(jnp.int32, (1, tile), 1)
        ones = jnp.ones((tile, LANES), BF16)
        m_ref[...] = jnp.full(m_ref.shape, MASK_VALUE, F32)
        l_ref[...] = jnp.zeros(l_ref.shape, F32)
        acc_ref[...] = jnp.zeros(acc_ref.shape, F32)

        def scores(s_ref, kb):
            k = k_ref[pl.ds(pl.multiple_of(kb * tile, tile), tile), :]
            s_ref[...] = _dot_nt(qq, k) + slope * (col_i + (kb - i) * tile).astype(F32)

        def accumulate(s_ref, kb, diagonal):
            v = v_ref[pl.ds(pl.multiple_of(kb * tile, tile), tile), :]
            if diagonal:
                row, col = _tile_row_col(tile, tile)
                s_ref[...] = jnp.where(col <= row, s_ref[...], MASK_VALUE)
            m_prev = m_ref[...]
            m_new = jnp.maximum(m_prev, jnp.max(s_ref[...], axis=-1, keepdims=True))
            alpha = jnp.exp2(m_prev - m_new)
            p = jnp.concatenate([jnp.exp2(s_ref[:, sl] - m_new).astype(BF16) for sl in _lane_chunks(tile)],
                                axis=1)
            pv = jnp.dot(p, jnp.concatenate([v, ones], axis=1),
                         preferred_element_type=F32)
            acc_ref[...] = alpha * acc_ref[...] + pv[:, :LANES]
            l_ref[...] = alpha * l_ref[...] + pv[:, LANES:]
            m_ref[...] = m_new

        lane = lax.broadcasted_iota(jnp.int32, (1, LANES), 1)
        half_ones = (lax.broadcasted_iota(jnp.int32, (LANES, LANES), 0) < HEAD_DIM) == (
            lax.broadcasted_iota(jnp.int32, (LANES, LANES), 1) < HEAD_DIM)
        half_ones = jnp.where(half_ones, 1.0, 0.0).astype(BF16)

        @pl.when(i == 0)
        def _():
            def chunk_max(c, best):
                k = k_ref[pl.ds(pl.multiple_of(c * tile, tile), tile), :].astype(F32)
                n2 = jnp.dot((k * k).astype(BF16), half_ones, preferred_element_type=F32)
                return jnp.maximum(best, jnp.max(n2, axis=0, keepdims=True))

            best = lax.fori_loop(0, k_ref.shape[0] // tile, chunk_max, jnp.zeros((1, LANES), F32))
            kmax_ref[...] = jnp.broadcast_to(best * NORM_SLACK, kmax_ref.shape)

        qf = qq.astype(F32)
        qn2 = jnp.dot((qf * qf).astype(BF16), jnp.ones((LANES, LANES), BF16), preferred_element_type=F32)
        q_half = jnp.where(lane < HEAD_DIM, jnp.max(qn2[:tile], axis=0, keepdims=True),
                           jnp.max(qn2[tile:], axis=0, keepdims=True))
        raw_bound = jnp.sqrt(jnp.max(q_half * NORM_SLACK * kmax_ref[0:1, :], axis=1, keepdims=True))

        def live(kb):
            far = jnp.zeros((1, LANES), jnp.int32) + ((kb + 1 - i) * tile - 1)
            best_left = raw_bound + slope[:, :LANES] * far.astype(F32)
            lowest_max = jnp.min(m_ref[...], axis=0, keepdims=True)
            return jnp.max(best_left - lowest_max) > -UNDERFLOW_BITS

        scores(sa_ref, i)

        @pl.when(i == 0)
        def _():
            accumulate(sa_ref, 0, True)

        @pl.when(i == 1)
        def _():
            scores(sb_ref, 0)
            accumulate(sa_ref, 1, True)
            accumulate(sb_ref, 0, False)

        @pl.when(i >= 2)
        def _():
            scores(sb_ref, i - 1)
            accumulate(sa_ref, i, True)
            scores(sa_ref, i - 2)
            accumulate(sb_ref, i - 1, False)
            n_pairs = (i - 2) // 2

            def cond(carry):
                j, go = carry
                return jnp.logical_and(j < n_pairs, go)

            def body(carry):
                j, _ = carry
                kb = i - 2 - 2 * j
                scores(sb_ref, kb - 1)
                accumulate(sa_ref, kb, False)
                scores(sa_ref, kb - 2)
                accumulate(sb_ref, kb - 1, False)
                return j + 1, live(kb - 2)

            _, go = lax.while_loop(cond, body, (jnp.int32(0), live(i - 2)))

            @pl.when(jnp.logical_and(go, i % 2 == 0))
            def _():
                accumulate(sa_ref, 0, False)

            @pl.when(jnp.logical_and(go, i % 2 == 1))
            def _():
                scores(sb_ref, 0)
                accumulate(sa_ref, 1, False)
                accumulate(sb_ref, 0, False)

        o_all = acc_ref[...] / l_ref[...]
        lv = lam_ref[...]
        lam = (jnp.exp(jnp.sum(lv[0:1] * lv[1:2], axis=1, keepdims=True))
               - jnp.exp(jnp.sum(lv[2:3] * lv[3:4], axis=1, keepdims=True)) + lam_init)
        o = o_all[:tile] - lam * o_all[tile:]
        ms = jnp.mean(o * o, axis=-1, keepdims=True)
        o = o * lax.rsqrt(ms + EPS) * subln_ref[...] * (1.0 - lam_init)
        o_ref[rows, :] = o.astype(o_ref.dtype)
        return carry

    lax.fori_loop(0, subtiles, one_tile, 0)


def _diff_attention(qkv, lam_vecs, subln, batch, seq, layer_idx):
    t = qkv.shape[0]
    d = qkv.shape[1] // 3
    heads = d // LANES
    tile = min(ATTN_TILE, seq)
    subtiles = min(DIFF_SUBTILES, seq // tile)
    nq = seq // (tile * subtiles)
    lam_init = 0.8 - 0.6 * math.exp(-0.3 * layer_idx)
    slopes = np.array([2.0 ** (-8.0 * (h + 1) / heads) * LOG2E for h in range(heads)], np.float32)
    slopes = jnp.asarray(np.broadcast_to(slopes[:, None, None], (heads, 1, tile)))
    kernel = functools.partial(_diff_attn_kernel, tile=tile, subtiles=subtiles, lam_init=lam_init)
    return pl.pallas_call(
        kernel,
        out_shape=jax.ShapeDtypeStruct((t, d), BF16),
        grid=(batch, heads, nq),
        in_specs=[
            pl.BlockSpec((tile * subtiles, LANES), lambda b, h, i: (b * nq + i, h)),
            pl.BlockSpec((seq, LANES), lambda b, h, i: (b, heads + h)),
            pl.BlockSpec((seq, LANES), lambda b, h, i: (b, 2 * heads + h)),
            pl.BlockSpec((None, 1, tile), lambda b, h, i: (h, 0, 0)),
            pl.BlockSpec(lam_vecs.shape, lambda b, h, i: (0, 0)),
            pl.BlockSpec((1, LANES), lambda b, h, i: (0, 0)),
        ],
        out_specs=pl.BlockSpec((tile * subtiles, LANES), lambda b, h, i: (b * nq + i, h)),
        scratch_shapes=[
            pltpu.VMEM((2 * tile, LANES), F32),
            pltpu.VMEM((2 * tile, LANES), F32),
            pltpu.VMEM((2 * tile, LANES), F32),
            pltpu.VMEM((2 * tile, tile), F32),
            pltpu.VMEM((2 * tile, tile), F32),
            pltpu.VMEM((SUBLANES, LANES), F32),
        ],
        compiler_params=_params("parallel", "parallel", "arbitrary"),
        name="diff_attention",
    )(qkv, qkv, qkv, slopes, lam_vecs, subln.reshape(1, LANES))


CHUNK = 256


def _suffix_sum_matrix():
    j = np.arange(CHUNK)
    return jnp.asarray((j[:, None] >= j[None, :]).astype(np.float32), dtype=BF16)


def _stick_attn_kernel(q_ref, k_ref, v_ref, mt_ref, o_ref, rem_ref, acc_ref, za_ref, zb_ref, zw_ref,
                       *, tile, subtiles):
    step = pl.program_id(2)

    def one_tile(sub, carry):
        i = step * subtiles + sub
        rows = pl.ds(pl.multiple_of(sub * tile, tile), tile)
        qq = _scaled_halves(q_ref[rows, :])
        mt = mt_ref[...]
        rem_ref[...] = jnp.zeros(rem_ref.shape, F32)
        acc_ref[...] = jnp.zeros(acc_ref.shape, F32)

        def scores(z_ref, kb, blocks=1):
            keys = pl.ds(pl.multiple_of(kb * tile, tile), blocks * tile)
            z_ref[...] = _dot_nt(qq, k_ref[keys, :])

        def accumulate(z_ref, kb, diagonal, blocks=1):
            v = v_ref[pl.ds(pl.multiple_of(kb * tile, tile), blocks * tile), :]
            rem = rem_ref[...]
            n_chunks = blocks * tile // CHUNK
            diag_from = (blocks - 1) * tile
            chunks = []
            for c in reversed(range(n_chunks)):
                z = z_ref[:, c * CHUNK:(c + 1) * CHUNK]
                softplus = jnp.maximum(z, 0.0) + jnp.log2(1.0 + jnp.exp2(-jnp.abs(z)))
                masked = diagonal and (c + 1) * CHUNK > diag_from
                if masked:
                    row = lax.broadcasted_iota(jnp.int32, z.shape, 0)
                    row = jnp.where(row >= tile, row - tile, row)
                    strict = lax.broadcasted_iota(jnp.int32, z.shape, 1) + (c * CHUNK - diag_from) < row
                    softplus = jnp.where(strict, softplus, 0.0)
                sums = jnp.dot(softplus.astype(BF16), mt, preferred_element_type=F32)
                for h in reversed(range(CHUNK // LANES)):
                    sl = slice(h * LANES, (h + 1) * LANES)
                    a = jnp.exp2(z_ref[:, c * CHUNK + h * LANES:c * CHUNK + (h + 1) * LANES] - sums[:, sl] - rem)
                    if masked:
                        a = jnp.where(strict[:, sl], a, 0.0)
                    chunks.append(a.astype(BF16))
                rem = rem + jnp.broadcast_to(sums[:, 0:1], rem.shape)
            rem_ref[...] = rem
            acc_ref[...] += jnp.dot(jnp.concatenate(chunks[::-1], axis=1), v, preferred_element_type=F32)

        def live():
            return jnp.min(rem_ref[...]) < UNDERFLOW_BITS

        @pl.when(i == 0)
        def _():
            scores(za_ref, 0)
            accumulate(za_ref, 0, True)

        @pl.when(i >= 1)
        def _():
            scores(zw_ref, i - 1, blocks=2)
            accumulate(zw_ref, i - 1, True, blocks=2)

        @pl.when(jnp.logical_and(i >= 2, live()))
        def _():
            scores(za_ref, i - 2)
            n_pairs = (i - 2) // 2

            def cond(carry):
                j, go = carry
                return jnp.logical_and(j < n_pairs, go)

            def body(carry):
                j, _ = carry
                kb = i - 2 - 2 * j
                scores(zb_ref, kb - 1)
                accumulate(za_ref, kb, False)
                scores(za_ref, kb - 2)
                accumulate(zb_ref, kb - 1, False)
                return j + 1, live()

            _, go = lax.while_loop(cond, body, (jnp.int32(0), jnp.bool_(True)))

            @pl.when(jnp.logical_and(go, i % 2 == 0))
            def _():
                accumulate(za_ref, 0, False)

            @pl.when(jnp.logical_and(go, i % 2 == 1))
            def _():
                scores(zb_ref, 0)
                accumulate(za_ref, 1, False)
                accumulate(zb_ref, 0, False)

        acc = acc_ref[...]
        lane = lax.broadcasted_iota(jnp.int32, (tile, LANES), 1)
        o_ref[rows, :] = jnp.where(lane < HEAD_DIM, acc[:tile], acc[tile:]).astype(o_ref.dtype)
        return carry

    lax.fori_loop(0, subtiles, one_tile, 0)


def _stick_attention(q, kv, batch, seq):
    t, d = q.shape
    pairs = d // LANES
    tile = min(STICK_TILE, seq)
    subtiles = min(STICK_SUBTILES, seq // tile)
    nq = seq // (tile * subtiles)
    kernel = functools.partial(_stick_attn_kernel, tile=tile, subtiles=subtiles)
    return pl.pallas_call(
        kernel,
        out_shape=jax.ShapeDtypeStruct((t, d), BF16),
        grid=(batch, pairs, nq),
        in_specs=[
            pl.BlockSpec((tile * subtiles, LANES), lambda b, j, i: (b * nq + i, j)),
            pl.BlockSpec((seq, LANES), lambda b, j, i: (b, j)),
            pl.BlockSpec((seq, LANES), lambda b, j, i: (b, pairs + j)),
            pl.BlockSpec((CHUNK, CHUNK), lambda b, j, i: (0, 0)),
        ],
        out_specs=pl.BlockSpec((tile * subtiles, LANES), lambda b, j, i: (b * nq + i, j)),
        scratch_shapes=[
            pltpu.VMEM((2 * tile, LANES), F32),
            pltpu.VMEM((2 * tile, LANES), F32),
            pltpu.VMEM((2 * tile, tile), F32),
            pltpu.VMEM((2 * tile, tile), F32),
            pltpu.VMEM((2 * tile, 2 * tile), F32),
        ],
        compiler_params=_params("parallel", "parallel", "arbitrary"),
        name="stick_attention",
    )(q, kv, kv, _suffix_sum_matrix())


SC_WINDOW = 128


def _sc_gather_rows(tab0, tab1, idx):
    m = idx.shape[0]
    w = tab0.shape[1]
    mesh = plsc.VectorSubcoreMesh(core_axis_name="core", subcore_axis_name="subcore")
    assert m % (SC_WINDOW * mesh.num_cores * mesh.num_subcores) == 0, m
    out = jax.ShapeDtypeStruct((m, w), tab0.dtype)

    @pl.kernel(out_type=(out, out), mesh=mesh)
    def gather(t0_hbm, t1_hbm, i_hbm, o0_hbm, o1_hbm):
        for t_hbm, o_hbm in ((t0_hbm, o0_hbm), (t1_hbm, o1_hbm)):
            def body(i_vmem, o_vmem, t_hbm=t_hbm):
                pltpu.sync_copy(t_hbm.at[i_vmem.at[0]], o_vmem)

            pltpu.emit_pipeline(
                body,
                grid=(m // SC_WINDOW,),
                in_specs=[pl.BlockSpec((1, SC_WINDOW), lambda i: (0, i))],
                out_specs=[pl.BlockSpec((SC_WINDOW, w), lambda i: (i, 0))],
                core_axis_name=("core", "subcore"),
                dimension_semantics=(pltpu.PARALLEL,),
            )(i_hbm, o_hbm)

    return gather(tab0, tab1, idx.reshape(1, m))


def _sc_scatter_rows(tab0, tab1, idx, n_rows):
    m = idx.shape[0]
    t, w = tab0.shape
    mesh = plsc.VectorSubcoreMesh(core_axis_name="core", subcore_axis_name="subcore")
    assert m % (SC_WINDOW * mesh.num_cores * mesh.num_subcores) == 0 and t % SC_WINDOW == 0, (m, t)
    windows_per_pass = t // SC_WINDOW
    out = jax.ShapeDtypeStruct((n_rows, w), tab0.dtype)

    @pl.kernel(out_type=(out, out), mesh=mesh)
    def scatter(t0_hbm, t1_hbm, i_hbm, o0_hbm, o1_hbm):
        for t_hbm, o_hbm in ((t0_hbm, o0_hbm), (t1_hbm, o1_hbm)):
            def body(x_vmem, i_vmem, o_hbm=o_hbm):
                pltpu.sync_copy(x_vmem, o_hbm.at[i_vmem.at[0]])

            pltpu.emit_pipeline(
                body,
                grid=(m // SC_WINDOW,),
                in_specs=[pl.BlockSpec((SC_WINDOW, w), lambda i: (i % windows_per_pass, 0)),
                          pl.BlockSpec((1, SC_WINDOW), lambda i: (0, i))],
                out_specs=[],
                core_axis_name=("core", "subcore"),
                dimension_semantics=(pltpu.PARALLEL,),
            )(t_hbm, i_hbm)

    return scatter(tab0, tab1, idx.reshape(1, m))


def _moe_route_kernel(x_ref, g_ref, sh_ref, sc_ref, rw_ref, rb_ref,
                      h0_ref, h1_ref, idx_ref, rank_ref, gate_ref, count_ref, run_ref):
    @pl.when(pl.program_id(0) == 0)
    def _():
        run_ref[...] = jnp.zeros(run_ref.shape, F32)

    h = _adaln(x_ref[...], g_ref[...], sh_ref[...], sc_ref[...])
    h0_ref[...], h1_ref[...] = _pack_rows(h)
    h_hi = h.astype(BF16)
    h_lo = (h - h_hi.astype(F32)).astype(BF16)
    logits = jnp.dot(jnp.concatenate([h_hi, h_lo, h_hi], axis=1), rw_ref[...],
                     preferred_element_type=F32) + rb_ref[...]
    lane = lax.broadcasted_iota(jnp.int32, logits.shape, 1)
    lane_f = lane.astype(F32)
    vals, ids = [], []
    for _ in range(TOP_K):
        m = jnp.max(logits, axis=-1, keepdims=True)
        first = jnp.min(jnp.where(logits == m, lane_f, float(LANES)), axis=-1, keepdims=True)
        vals.append(m)
        ids.append(first)
        logits = jnp.where(lane_f == first, -jnp.inf, logits)
    exps = [jnp.exp(v - vals[0]) for v in vals]
    denom = exps[0]
    for e in exps[1:]:
        denom = denom + e
    tm = logits.shape[0]
    earlier = (lax.broadcasted_iota(jnp.int32, (tm, tm), 1) < lax.broadcasted_iota(jnp.int32, (tm, tm), 0))
    earlier = jnp.where(earlier, 1.0, 0.0).astype(BF16)
    seen = run_ref[...]
    gate = jnp.zeros(logits.shape, F32)
    idx = jnp.zeros(logits.shape, F32)
    rank = jnp.zeros(logits.shape, F32)
    for k in range(TOP_K):
        onehot = jnp.where(lane_f == ids[k], 1.0, 0.0)
        before = jnp.dot(earlier, onehot.astype(BF16), preferred_element_type=F32) + seen
        rank = jnp.where(lane == k, jnp.sum(onehot * before, axis=-1, keepdims=True), rank)
        seen = seen + jnp.sum(onehot, axis=0, keepdims=True)
        gate = jnp.where(lane == k, exps[k] / denom, gate)
        idx = jnp.where(lane == k, ids[k], idx)
    run_ref[...] = seen
    count_ref[...] = jnp.broadcast_to(seen, count_ref.shape).astype(jnp.int32)
    gate_ref[...] = gate
    idx_ref[...] = idx.T[:SUBLANES].astype(jnp.int32)
    rank_ref[...] = rank.T[:SUBLANES].astype(jnp.int32)


def _moe_route(x, gain, shift, scale, rw_pad, rb_pad, seq):
    t, d = x.shape
    tm = min(TOKEN_TILE, seq)
    per_seq = seq // tm
    b = shift.shape[0]
    return pl.pallas_call(
        _moe_route_kernel,
        out_shape=(jax.ShapeDtypeStruct((t, d // 4), jnp.uint32),
                   jax.ShapeDtypeStruct((t, d // 4), jnp.uint32),
                   jax.ShapeDtypeStruct((SUBLANES, t), jnp.int32),
                   jax.ShapeDtypeStruct((SUBLANES, t), jnp.int32),
                   jax.ShapeDtypeStruct((t, LANES), F32),
                   jax.ShapeDtypeStruct((SUBLANES, LANES), jnp.int32)),
        grid=(t // tm,),
        in_specs=[
            pl.BlockSpec((tm, d), lambda i: (i, 0)),
            pl.BlockSpec((1, d), lambda i: (0, 0)),
            pl.BlockSpec((None, 1, d), lambda i: (i // per_seq, 0, 0)),
            pl.BlockSpec((None, 1, d), lambda i: (i // per_seq, 0, 0)),
            pl.BlockSpec((3 * d, LANES), lambda i: (0, 0)),
            pl.BlockSpec((1, LANES), lambda i: (0, 0)),
        ],
        out_specs=(pl.BlockSpec((tm, d // 4), lambda i: (i, 0)),
                   pl.BlockSpec((tm, d // 4), lambda i: (i, 0)),
                   pl.BlockSpec((SUBLANES, tm), lambda i: (0, i)),
                   pl.BlockSpec((SUBLANES, tm), lambda i: (0, i)),
                   pl.BlockSpec((tm, LANES), lambda i: (i, 0)),
                   pl.BlockSpec((SUBLANES, LANES), lambda i: (0, 0))),
        scratch_shapes=[pltpu.VMEM((1, LANES), F32)],
        compiler_params=_params("arbitrary"),
        name="moe_route",
    )(x, gain.reshape(1, d), shift.reshape(b, 1, d), scale.reshape(b, 1, d), rw_pad, rb_pad)


def _expert_kernel(block_e_ref, n_used_ref, n_valid_ref, h0_ref, h1_ref, wgu_ref, bgu_ref, wd_ref, bd_ref,
                   o0_ref, o1_ref, wgu_bf_ref, wd_bf_ref):
    i = pl.program_id(0)
    new_expert = jnp.logical_or(i == 0, block_e_ref[i] != block_e_ref[jnp.maximum(i - 1, 0)])

    @pl.when(jnp.logical_and(new_expert, i < n_used_ref[0]))
    def _():
        wgu_bf_ref[...] = wgu_ref[...].astype(BF16)
        wd_bf_ref[...] = wd_ref[...].astype(BF16)

    @pl.when(i < n_used_ref[0])
    def _():
        d_e = wd_ref.shape[0]
        written = lax.broadcasted_iota(jnp.int32, h0_ref.shape, 0) < n_valid_ref[i]
        zero = jnp.zeros(h0_ref.shape, h0_ref.dtype)
        quarters = _unpack_rows(jnp.where(written, h0_ref[...], zero), jnp.where(written, h1_ref[...], zero))
        h = jnp.concatenate([q.astype(BF16) for q in quarters], axis=1)
        gu = jnp.dot(h, wgu_bf_ref[...], preferred_element_type=F32) + bgu_ref[...]
        g = jnp.minimum(gu[:, :d_e], SWIGLU_LIMIT)
        u = jnp.clip(gu[:, d_e:], -SWIGLU_LIMIT, SWIGLU_LIMIT)
        act = g / (1.0 + jnp.exp(-SWIGLU_ALPHA * g)) * (u + 1.0)
        y = jnp.dot(act.astype(BF16), wd_bf_ref[...], preferred_element_type=F32) + bd_ref[...]
        o0_ref[...], o1_ref[...] = _pack_rows(y)

    @pl.when(i >= n_used_ref[0])
    def _():
        o0_ref[...] = jnp.zeros(o0_ref.shape, o0_ref.dtype)
        o1_ref[...] = jnp.zeros(o1_ref.shape, o1_ref.dtype)


def _expert_mlp(block_e, n_used, n_valid, rows0, rows1, wgu, bgu, wd, bd, layer, tile):
    n_rows, dq = rows0.shape
    d = 4 * dq
    n_l, n_e, _, two_de = wgu.shape
    d_e = two_de // 2
    n_blocks = n_rows // tile
    grid_spec = pltpu.PrefetchScalarGridSpec(
        num_scalar_prefetch=3,
        grid=(n_blocks,),
        in_specs=[
            pl.BlockSpec((tile, dq), lambda i, be, nu, nv: (i, 0)),
            pl.BlockSpec((tile, dq), lambda i, be, nu, nv: (i, 0)),
            pl.BlockSpec((None, None, d, two_de), lambda i, be, nu, nv: (layer, be[i], 0, 0)),
            pl.BlockSpec((None, None, 1, two_de), lambda i, be, nu, nv: (layer, be[i], 0, 0)),
            pl.BlockSpec((None, None, d_e, d), lambda i, be, nu, nv: (layer, be[i], 0, 0)),
            pl.BlockSpec((None, None, 1, d), lambda i, be, nu, nv: (layer, be[i], 0, 0)),
        ],
        out_specs=(pl.BlockSpec((tile, dq), lambda i, be, nu, nv: (i, 0)),
                   pl.BlockSpec((tile, dq), lambda i, be, nu, nv: (i, 0))),
        scratch_shapes=[pltpu.VMEM((d, two_de), BF16), pltpu.VMEM((d_e, d), BF16)],
    )
    packed = jax.ShapeDtypeStruct((n_rows, dq), jnp.uint32)
    return pl.pallas_call(
        _expert_kernel,
        out_shape=(packed, packed),
        grid_spec=grid_spec,
        compiler_params=_params("arbitrary"),
        name="expert_mlp",
    )(block_e, n_used, n_valid, rows0, rows1, wgu, bgu.reshape(n_l, n_e, 1, two_de), wd,
      bd.reshape(n_l, n_e, 1, d))


def _combine_kernel(y0_ref, y1_ref, w_ref, x_ref, g_ref, n_ref, o_ref, *, normalize):
    w = w_ref[...]
    acc = None
    for k in range(y0_ref.shape[0]):
        quarters = _unpack_rows(y0_ref[k], y1_ref[k])
        scaled = [w[:, k:k + 1] * q for q in quarters]
        acc = scaled if acc is None else [a + s for a, s in zip(acc, scaled)]
    x = x_ref[...] + g_ref[...] * jnp.concatenate(acc, axis=1)
    if normalize:
        x = x * lax.rsqrt(jnp.mean(x * x, axis=-1, keepdims=True) + EPS) * n_ref[...]
    o_ref[...] = x


def _combine(y0, y1, weights, x, gate, seq, final_gain=None):
    t, d = x.shape
    tm = min(TOKEN_TILE, seq)
    per_seq = seq // tm
    b = gate.shape[0]
    norm_gain = jnp.ones((1, d), F32) if final_gain is None else final_gain.reshape(1, d)
    return pl.pallas_call(
        functools.partial(_combine_kernel, normalize=final_gain is not None),
        out_shape=jax.ShapeDtypeStruct((t, d), F32),
        grid=(t // tm,),
        in_specs=[
            pl.BlockSpec((y0.shape[0], tm, d // 4), lambda i: (0, i, 0)),
            pl.BlockSpec((y0.shape[0], tm, d // 4), lambda i: (0, i, 0)),
            pl.BlockSpec((tm, LANES), lambda i: (i, 0)),
            pl.BlockSpec((tm, d), lambda i: (i, 0)),
            pl.BlockSpec((None, 1, d), lambda i: (i // per_seq, 0, 0)),
            pl.BlockSpec((1, d), lambda i: (0, 0)),
        ],
        out_specs=pl.BlockSpec((tm, d), lambda i: (i, 0)),
        compiler_params=_params("parallel"),
        name="moe_combine",
    )(y0, y1, weights, x, gate.reshape(b, 1, d), norm_gain)


ROW_BLOCK_MULTIPLE = 8


def _routing_tables(idx_t, rank_t, counts, tile):
    top_k, t = idx_t.shape
    n_experts = counts.shape[0]
    n_assign = top_k * t
    padded = (counts + tile - 1) // tile * tile
    pend = jnp.cumsum(padded)
    pstart = pend - padded
    e_flat = idx_t.reshape(n_assign)
    onehot = e_flat[:, None] == jnp.arange(n_experts, dtype=jnp.int32)[None, :]
    slot_row = rank_t.reshape(n_assign) + jnp.sum(jnp.where(onehot, pstart[None, :], 0), axis=1,
                                                  dtype=jnp.int32)
    n_blocks = (n_assign + n_experts * (tile - 1) + tile - 1) // tile
    n_blocks = -(-n_blocks // ROW_BLOCK_MULTIPLE) * ROW_BLOCK_MULTIPLE
    first_row = jnp.arange(n_blocks, dtype=jnp.int32) * tile
    block_e = jnp.minimum(jnp.sum(pend[None, :] <= first_row[:, None], axis=1), n_experts - 1)
    block_e = block_e.astype(jnp.int32)
    n_valid = jnp.clip(counts[block_e] - (first_row - pstart[block_e]), 0, tile).astype(jnp.int32)
    n_used = (pend[-1] // tile).astype(jnp.int32).reshape(1)
    return slot_row, block_e, n_valid, n_used, n_blocks * tile


def _moe(x, gain, shift, scale, gate, rw, rb, wgu, bgu, wd, bd, layer, seq, final_gain=None):
    t, d = x.shape
    n_e = rw.shape[1]
    rw_pad = jnp.pad(rw, ((0, 0), (0, LANES - n_e)))
    rw_hi = rw_pad.astype(BF16)
    rw_lo = (rw_pad - rw_hi.astype(F32)).astype(BF16)
    rw_pad = jnp.concatenate([rw_hi, rw_hi, rw_lo], axis=0)
    rb_pad = jnp.pad(rb, (0, LANES - n_e), constant_values=MASK_VALUE).reshape(1, LANES)
    h0, h1, idx_t, rank_t, weights, counts = _moe_route(x, gain, shift, scale, rw_pad, rb_pad, seq)
    tile = min(EXPERT_TILE, t * TOP_K // n_e)
    slot_row, block_e, n_valid, n_used, n_rows = _routing_tables(
        idx_t[:TOP_K], rank_t[:TOP_K], counts[0, :n_e], tile)
    rows0, rows1 = _sc_scatter_rows(h0, h1, slot_row, n_rows)
    ys0, ys1 = _expert_mlp(block_e, n_used, n_valid, rows0, rows1, wgu, bgu, wd, bd, layer, tile)
    y0, y1 = _sc_gather_rows(ys0, ys1, slot_row)
    return _combine(y0.reshape(TOP_K, t, d // 4), y1.reshape(TOP_K, t, d // 4), weights, x, gate, seq,
                    final_gain)


def kernel(x, c, ada_w, ada_b, norm_mix, norm_moe, a_wqkv, a_wo, a_lambda, a_subln, kv_norm,
           kv_ada_w, kv_ada_b, kv_w, b_wq, b_wo, router_w, router_b, w_gate_up, b_gate_up,
           w_down, b_down, final_norm):
    b, s, d = x.shape
    depth = ada_w.shape[0]
    n_a = a_wqkv.shape[0]
    xt = x.reshape(b * s, d)

    c_pad = jnp.pad(c, ((0, SUBLANES - b), (0, 0)))
    mod = _modulation(c_pad, ada_w, ada_b)[:, :b]
    kv_mod = _modulation(c_pad, kv_ada_w[None], kv_ada_b[None])[0, :b]

    for l in range(depth):
        sh1, sc1, g1, sh2, sc2, g2 = jnp.split(mod[l], 6, axis=-1)
        if l < n_a:
            qkv = _norm_linear(xt, norm_mix[l], sh1, sc1, a_wqkv[l].astype(BF16), s)
            o = _diff_attention(qkv, a_lambda[l], a_subln[l], b, s, l)
            xt = _proj_residual(o, a_wo[l].astype(BF16), xt, g1, s)
        else:
            if l == n_a:
                kv_sh, kv_sc = jnp.split(kv_mod, 2, axis=-1)
                kv = _norm_linear(xt, kv_norm, kv_sh, kv_sc, kv_w.astype(BF16), s)
            j = l - n_a
            q = _norm_linear(xt, norm_mix[l], sh1, sc1, b_wq[j].astype(BF16), s)
            o = _stick_attention(q, kv, b, s)
            xt = _proj_residual(o, b_wo[j].astype(BF16), xt, g1, s)
        xt = _moe(xt, norm_moe[l], sh2, sc2, g2, router_w[l], router_b[l],
                  w_gate_up, b_gate_up, w_down, b_down, l, s,
                  final_gain=final_norm if l == depth - 1 else None)
    return xt.reshape(b, s, d)
```

```python
import functools
import math

import jax
import jax.numpy as jnp
import numpy as np
from jax import lax
from jax.experimental import pallas as pl
from jax.experimental.pallas import tpu as pltpu
from jax.experimental.pallas import tpu_sc as plsc

F32 = jnp.float32
BF16 = jnp.bfloat16
EPS = 1e-6
MASK_VALUE = -1e30
TOP_K = 4
SWIGLU_ALPHA = 1.702
SWIGLU_LIMIT = 7.0
LOG2E = math.log2(math.e)
UNDERFLOW_BITS = 160.0
NORM_SLACK = 1.03

LANES = 128
SUBLANES = 8
HEAD_DIM = 64
V7X_VMEM_BYTES = 64 * 2 ** 20
VMEM_LIMIT = V7X_VMEM_BYTES * 7 // 8

TOKEN_TILE = 512
ATTN_TILE = 512
STICK_TILE = 256
DIFF_SUBTILES = 2
STICK_SUBTILES = 4
EXPERT_TILE = 512


def _params(*semantics):
    return pltpu.CompilerParams(dimension_semantics=semantics, vmem_limit_bytes=VMEM_LIMIT)


def _largest_tile(n, cap):
    t = min(n, cap) // LANES * LANES
    while n % t:
        t -= LANES
    return t


def _adaln(x, gain, shift, scale):
    ms = jnp.mean(x * x, axis=-1, keepdims=True)
    return x * lax.rsqrt(ms + EPS) * gain * (1.0 + scale) + shift


def _pack_bf16_pair(lo, hi):
    lo_bits = lax.bitcast_convert_type(lo.astype(BF16).astype(F32), jnp.uint32) >> 16
    hi_bits = lax.bitcast_convert_type(hi.astype(BF16).astype(F32), jnp.uint32) & jnp.uint32(0xFFFF0000)
    return lo_bits | hi_bits


def _unpack_bf16_pair(packed):
    lo = lax.bitcast_convert_type(packed << 16, F32)
    hi = lax.bitcast_convert_type(packed & jnp.uint32(0xFFFF0000), F32)
    return lo, hi


def _pack_rows(x):
    q = x.shape[1] // 4
    return (_pack_bf16_pair(x[:, :q], x[:, 2 * q:3 * q]),
            _pack_bf16_pair(x[:, q:2 * q], x[:, 3 * q:]))


def _unpack_rows(p0, p1):
    a0, a2 = _unpack_bf16_pair(p0)
    a1, a3 = _unpack_bf16_pair(p1)
    return a0, a1, a2, a3


def _mod_kernel(c_ref, w_ref, b_ref, o_ref):
    c = c_ref[...]
    a = c / (1.0 + jnp.exp(-c))
    o_ref[...] = jnp.dot(a, w_ref[...], precision=lax.Precision.HIGHEST,
                         preferred_element_type=F32) + b_ref[...]


def _modulation(c_pad, w, b):
    n_l, d, n = w.shape
    tn = _largest_tile(n, 2048)
    return pl.pallas_call(
        _mod_kernel,
        out_shape=jax.ShapeDtypeStruct((n_l, SUBLANES, n), F32),
        grid=(n_l, n // tn),
        in_specs=[
            pl.BlockSpec((SUBLANES, d), lambda l, j: (0, 0)),
            pl.BlockSpec((None, d, tn), lambda l, j: (l, 0, j)),
            pl.BlockSpec((None, 1, tn), lambda l, j: (l, 0, j)),
        ],
        out_specs=pl.BlockSpec((None, SUBLANES, tn), lambda l, j: (l, 0, j)),
        compiler_params=_params("parallel", "parallel"),
        name="modulation",
    )(c_pad, w, b.reshape(n_l, 1, n))


def _norm_linear_kernel(x_ref, g_ref, sh_ref, sc_ref, w_ref, o_ref):
    h = _adaln(x_ref[...], g_ref[...], sh_ref[...], sc_ref[...])
    o_ref[...] = jnp.dot(h.astype(BF16), w_ref[...], preferred_element_type=F32).astype(o_ref.dtype)


def _norm_linear(x, gain, shift, scale, w, seq):
    t, d = x.shape
    n = w.shape[1]
    tm = min(TOKEN_TILE, seq)
    per_seq = seq // tm
    b = shift.shape[0]
    return pl.pallas_call(
        _norm_linear_kernel,
        out_shape=jax.ShapeDtypeStruct((t, n), BF16),
        grid=(t // tm,),
        in_specs=[
            pl.BlockSpec((tm, d), lambda i: (i, 0)),
            pl.BlockSpec((1, d), lambda i: (0, 0)),
            pl.BlockSpec((None, 1, d), lambda i: (i // per_seq, 0, 0)),
            pl.BlockSpec((None, 1, d), lambda i: (i // per_seq, 0, 0)),
            pl.BlockSpec((d, n), lambda i: (0, 0)),
        ],
        out_specs=pl.BlockSpec((tm, n), lambda i: (i, 0)),
        compiler_params=_params("parallel"),
        name="norm_linear",
    )(x, gain.reshape(1, d), shift.reshape(b, 1, d), scale.reshape(b, 1, d), w)


def _proj_residual_kernel(a_ref, w_ref, x_ref, g_ref, o_ref):
    y = jnp.dot(a_ref[...], w_ref[...], preferred_element_type=F32)
    o_ref[...] = x_ref[...] + g_ref[...] * y


def _proj_residual(a, w, x, gate, seq):
    t, d = x.shape
    k = a.shape[1]
    tm = min(TOKEN_TILE, seq)
    per_seq = seq // tm
    b = gate.shape[0]
    return pl.pallas_call(
        _proj_residual_kernel,
        out_shape=jax.ShapeDtypeStruct((t, d), F32),
        grid=(t // tm,),
        in_specs=[
            pl.BlockSpec((tm, k), lambda i: (i, 0)),
            pl.BlockSpec((k, d), lambda i: (0, 0)),
            pl.BlockSpec((tm, d), lambda i: (i, 0)),
            pl.BlockSpec((None, 1, d), lambda i: (i // per_seq, 0, 0)),
        ],
        out_specs=pl.BlockSpec((tm, d), lambda i: (i, 0)),
        compiler_params=_params("parallel"),
        name="proj_residual",
    )(a, w, x, gate.reshape(b, 1, d))


def _scaled_halves(q):
    q = (q.astype(F32) * (HEAD_DIM ** -0.5 * LOG2E)).astype(BF16)
    lane = lax.broadcasted_iota(jnp.int32, q.shape, 1)
    zero = jnp.zeros_like(q)
    return jnp.concatenate([jnp.where(lane < HEAD_DIM, q, zero),
                            jnp.where(lane >= HEAD_DIM, q, zero)], axis=0)


def _dot_nt(a, b):
    return lax.dot_general(a, b, (((1,), (1,)), ((), ())), preferred_element_type=F32)


def _tile_row_col(tq, tk):
    row = lax.broadcasted_iota(jnp.int32, (2 * tq, tk), 0)
    row = jnp.where(row >= tq, row - tq, row)
    col = lax.broadcasted_iota(jnp.int32, (2 * tq, tk), 1)
    return row, col


def _lane_chunks(n):
    return [slice(c * LANES, (c + 1) * LANES) for c in range(n // LANES)]


def _diff_attn_kernel(q_ref, k_ref, v_ref, slope_ref, lam_ref, subln_ref, o_ref,
                      m_ref, l_ref, acc_ref, sa_ref, sb_ref, kmax_ref, *, tile, subtiles, lam_init):
    step = pl.program_id(2)

    def one_tile(sub, carry):
        i = step * subtiles + sub
        rows = pl.ds(pl.multiple_of(sub * tile, tile), tile)
        qq = _scaled_halves(q_ref[rows, :])
        slope = slope_ref[...]
        col_i = lax.broadcasted_iota(jnp.int32, (1, tile), 1)
        ones = jnp.ones((tile, LANES), BF16)
        m_ref[...] = jnp.full(m_ref.shape, MASK_VALUE, F32)
        l_ref[...] = jnp.zeros(l_ref.shape, F32)
        acc_ref[...] = jnp.zeros(acc_ref.shape, F32)

        def scores(s_ref, kb):
            k = k_ref[pl.ds(pl.multiple_of(kb * tile, tile), tile), :]
            s_ref[...] = _dot_nt(qq, k) + slope * (col_i + (kb - i) * tile).astype(F32)

        def accumulate(s_ref, kb, diagonal):
            v = v_ref[pl.ds(pl.multiple_of(kb * tile, tile), tile), :]
            if diagonal:
                row, col = _tile_row_col(tile, tile)
                s_ref[...] = jnp.where(col <= row, s_ref[...], MASK_VALUE)
            m_prev = m_ref[...]
            m_new = jnp.maximum(m_prev, jnp.max(s_ref[...], axis=-1, keepdims=True))
            alpha = jnp.exp2(m_prev - m_new)
            p = jnp.concatenate([jnp.exp2(s_ref[:, sl] - m_new).astype(BF16) for sl in _lane_chunks(tile)],
                                axis=1)
            pv = jnp.dot(p, jnp.concatenate([v, ones], axis=1),
                         preferred_element_type=F32)
            acc_ref[...] = alpha * acc_ref[...] + pv[:, :LANES]
            l_ref[...] = alpha * l_ref[...] + pv[:, LANES:]
            m_ref[...] = m_new

        lane = lax.broadcasted_iota(jnp.int32, (1, LANES), 1)
        half_ones = (lax.broadcasted_iota(jnp.int32, (LANES, LANES), 0) < HEAD_DIM) == (
            lax.broadcasted_iota(jnp.int32, (LANES, LANES), 1) < HEAD_DIM)
        half_ones = jnp.where(half_ones, 1.0, 0.0).astype(BF16)

        @pl.when(i == 0)
        def _():
            def chunk_max(c, best):
                k = k_ref[pl.ds(pl.multiple_of(c * tile, tile), tile), :].astype(F32)
                n2 = jnp.dot((k * k).astype(BF16), half_ones, preferred_element_type=F32)
                return jnp.maximum(best, jnp.max(n2, axis=0, keepdims=True))

            best = lax.fori_loop(0, k_ref.shape[0] // tile, chunk_max, jnp.zeros((1, LANES), F32))
            kmax_ref[...] = jnp.broadcast_to(best * NORM_SLACK, kmax_ref.shape)

        qf = qq.astype(F32)
        qn2 = jnp.dot((qf * qf).astype(BF16), jnp.ones((LANES, LANES), BF16), preferred_element_type=F32)
        q_half = jnp.where(lane < HEAD_DIM, jnp.max(qn2[:tile], axis=0, keepdims=True),
                           jnp.max(qn2[tile:], axis=0, keepdims=True))
        raw_bound = jnp.sqrt(jnp.max(q_half * NORM_SLACK * kmax_ref[0:1, :], axis=1, keepdims=True))

        def live(kb):
            far = jnp.zeros((1, LANES), jnp.int32) + ((kb + 1 - i) * tile - 1)
            best_left = raw_bound + slope[:, :LANES] * far.astype(F32)
            lowest_max = jnp.min(m_ref[...], axis=0, keepdims=True)
            return jnp.max(best_left - lowest_max) > -UNDERFLOW_BITS

        scores(sa_ref, i)

        @pl.when(i == 0)
        def _():
            accumulate(sa_ref, 0, True)

        @pl.when(i == 1)
        def _():
            scores(sb_ref, 0)
            accumulate(sa_ref, 1, True)
            accumulate(sb_ref, 0, False)

        @pl.when(i >= 2)
        def _():
            scores(sb_ref, i - 1)
            accumulate(sa_ref, i, True)
            scores(sa_ref, i - 2)
            accumulate(sb_ref, i - 1, False)
            n_pairs = (i - 2) // 2

            def cond(carry):
                j, go = carry
                return jnp.logical_and(j < n_pairs, go)

            def body(carry):
                j, _ = carry
                kb = i - 2 - 2 * j
                scores(sb_ref, kb - 1)
                accumulate(sa_ref, kb, False)
                scores(sa_ref, kb - 2)
                accumulate(sb_ref, kb - 1, False)
                return j + 1, live(kb - 2)

            _, go = lax.while_loop(cond, body, (jnp.int32(0), live(i - 2)))

            @pl.when(jnp.logical_and(go, i % 2 == 0))
            def _():
                accumulate(sa_ref, 0, False)

            @pl.when(jnp.logical_and(go, i % 2 == 1))
            def _():
                scores(sb_ref, 0)
                accumulate(sa_ref, 1, False)
                accumulate(sb_ref, 0, False)

        o_all = acc_ref[...] / l_ref[...]
        lv = lam_ref[...]
        lam = (jnp.exp(jnp.sum(lv[0:1] * lv[1:2], axis=1, keepdims=True))
               - jnp.exp(jnp.sum(lv[2:3] * lv[3:4], axis=1, keepdims=True)) + lam_init)
        o = o_all[:tile] - lam * o_all[tile:]
        ms = jnp.mean(o * o, axis=-1, keepdims=True)
        o = o * lax.rsqrt(ms + EPS) * subln_ref[...] * (1.0 - lam_init)
        o_ref[rows, :] = o.astype(o_ref.dtype)
        return carry

    lax.fori_loop(0, subtiles, one_tile, 0)


def _diff_attention(qkv, lam_vecs, subln, batch, seq, layer_idx):
    t = qkv.shape[0]
    d = qkv.shape[1] // 3
    heads = d // LANES
    tile = min(ATTN_TILE, seq)
    subtiles = min(DIFF_SUBTILES, seq // tile)
    nq = seq // (tile * subtiles)
    lam_init = 0.8 - 0.6 * math.exp(-0.3 * layer_idx)
    slopes = np.array([2.0 ** (-8.0 * (h + 1) / heads) * LOG2E for h in range(heads)], np.float32)
    slopes = jnp.asarray(np.broadcast_to(slopes[:, None, None], (heads, 1, tile)))
    kernel = functools.partial(_diff_attn_kernel, tile=tile, subtiles=subtiles, lam_init=lam_init)
    return pl.pallas_call(
        kernel,
        out_shape=jax.ShapeDtypeStruct((t, d), BF16),
        grid=(batch, heads, nq),
        in_specs=[
            pl.BlockSpec((tile * subtiles, LANES), lambda b, h, i: (b * nq + i, h)),
            pl.BlockSpec((seq, LANES), lambda b, h, i: (b, heads + h)),
            pl.BlockSpec((seq, LANES), lambda b, h, i: (b, 2 * heads + h)),
            pl.BlockSpec((None, 1, tile), lambda b, h, i: (h, 0, 0)),
            pl.BlockSpec(lam_vecs.shape, lambda b, h, i: (0, 0)),
            pl.BlockSpec((1, LANES), lambda b, h, i: (0, 0)),
        ],
        out_specs=pl.BlockSpec((tile * subtiles, LANES), lambda b, h, i: (b * nq + i, h)),
        scratch_shapes=[
            pltpu.VMEM((2 * tile, LANES), F32),
            pltpu.VMEM((2 * tile, LANES), F32),
            pltpu.VMEM((2 * tile, LANES), F32),
            pltpu.VMEM((2 * tile, tile), F32),
            pltpu.VMEM((2 * tile, tile), F32),
            pltpu.VMEM((SUBLANES, LANES), F32),
        ],
        compiler_params=_params("parallel", "parallel", "arbitrary"),
        name="diff_attention",
    )(qkv, qkv, qkv, slopes, lam_vecs, subln.reshape(1, LANES))


CHUNK = 256


def _suffix_sum_matrix():
    j = np.arange(CHUNK)
    return jnp.asarray((j[:, None] >= j[None, :]).astype(np.float32), dtype=BF16)


def _stick_attn_kernel(q_ref, k_ref, v_ref, mt_ref, o_ref, rem_ref, acc_ref, low_ref, za_ref, zb_ref, zw_ref,
                       *, tile, subtiles):
    step = pl.program_id(2)
    mt = mt_ref[...]

    def tile_ops(sub):
        i = step * subtiles + sub
        rows = pl.ds(pl.multiple_of(sub * tile, tile), tile)
        qq = _scaled_halves(q_ref[rows, :])
        rem_t, acc_t = rem_ref.at[sub], acc_ref.at[sub]

        def scores(z_ref, kb, blocks=1):
            keys = pl.ds(pl.multiple_of(kb * tile, tile), blocks * tile)
            z_ref[...] = _dot_nt(qq, k_ref[keys, :])

        def accumulate(z_ref, kb, diagonal, blocks=1):
            v = v_ref[pl.ds(pl.multiple_of(kb * tile, tile), blocks * tile), :]
            rem = rem_t[...]
            n_chunks = blocks * tile // CHUNK
            diag_from = (blocks - 1) * tile
            chunks = []
            for c in reversed(range(n_chunks)):
                z = z_ref[:, c * CHUNK:(c + 1) * CHUNK]
                softplus = jnp.maximum(z, 0.0) + jnp.log2(1.0 + jnp.exp2(-jnp.abs(z)))
                masked = diagonal and (c + 1) * CHUNK > diag_from
                if masked:
                    row = lax.broadcasted_iota(jnp.int32, z.shape, 0)
                    row = jnp.where(row >= tile, row - tile, row)
                    strict = lax.broadcasted_iota(jnp.int32, z.shape, 1) + (c * CHUNK - diag_from) < row
                    softplus = jnp.where(strict, softplus, 0.0)
                sums = jnp.dot(softplus.astype(BF16), mt, preferred_element_type=F32)
                for h in reversed(range(CHUNK // LANES)):
                    sl = slice(h * LANES, (h + 1) * LANES)
                    a = jnp.exp2(z_ref[:, c * CHUNK + h * LANES:c * CHUNK + (h + 1) * LANES] - sums[:, sl] - rem)
                    if masked:
                        a = jnp.where(strict[:, sl], a, 0.0)
                    chunks.append(a.astype(BF16))
                rem = rem + jnp.broadcast_to(sums[:, 0:1], rem.shape)
            rem_t[...] = rem
            acc_t[...] += jnp.dot(jnp.concatenate(chunks[::-1], axis=1), v, preferred_element_type=F32)

        return i, rows, rem_t, acc_t, scores, accumulate

    def window_pass(sub, carry):
        i, _, rem_t, acc_t, scores, accumulate = tile_ops(sub)
        rem_t[...] = jnp.zeros(rem_t.shape, F32)
        acc_t[...] = jnp.zeros(acc_t.shape, F32)

        @pl.when(i == 0)
        def _():
            scores(za_ref, 0)
            accumulate(za_ref, 0, True)

        @pl.when(i >= 1)
        def _():
            scores(zw_ref, i - 1, blocks=2)
            accumulate(zw_ref, i - 1, True, blocks=2)

        lowest = jnp.min(rem_t[...], axis=0, keepdims=True)
        low_ref[sub] = jnp.where(i >= 2, jnp.broadcast_to(lowest, low_ref.shape[1:]), UNDERFLOW_BITS)
        return carry

    lax.fori_loop(0, subtiles, window_pass, 0)

    @pl.when(jnp.min(low_ref[...]) < UNDERFLOW_BITS)
    def _():
        def older_tiles(sub, carry):
            i, _, rem_t, _, scores, accumulate = tile_ops(sub)

            def live():
                return jnp.min(rem_t[...]) < UNDERFLOW_BITS

            @pl.when(jnp.min(low_ref[sub]) < UNDERFLOW_BITS)
            def _():
                scores(za_ref, i - 2)
                n_pairs = (i - 2) // 2

                def cond(carry):
                    j, go = carry
                    return jnp.logical_and(j < n_pairs, go)

                def body(carry):
                    j, _ = carry
                    kb = i - 2 - 2 * j
                    scores(zb_ref, kb - 1)
                    accumulate(za_ref, kb, False)
                    scores(za_ref, kb - 2)
                    accumulate(zb_ref, kb - 1, False)
                    return j + 1, live()

                _, go = lax.while_loop(cond, body, (jnp.int32(0), jnp.bool_(True)))

                @pl.when(jnp.logical_and(go, i % 2 == 0))
                def _():
                    accumulate(za_ref, 0, False)

                @pl.when(jnp.logical_and(go, i % 2 == 1))
                def _():
                    scores(zb_ref, 0)
                    accumulate(za_ref, 1, False)
                    accumulate(zb_ref, 0, False)

            return carry

        lax.fori_loop(0, subtiles, older_tiles, 0)

    lane = lax.broadcasted_iota(jnp.int32, (tile, LANES), 1)
    for sub in range(subtiles):
        acc = acc_ref[sub]
        o_ref[sub * tile:(sub + 1) * tile, :] = jnp.where(
            lane < HEAD_DIM, acc[:tile], acc[tile:]).astype(o_ref.dtype)


def _stick_attention(q, kv, batch, seq):
    t, d = q.shape
    pairs = d // LANES
    tile = min(STICK_TILE, seq)
    subtiles = min(STICK_SUBTILES, seq // tile)
    nq = seq // (tile * subtiles)
    kernel = functools.partial(_stick_attn_kernel, tile=tile, subtiles=subtiles)
    return pl.pallas_call(
        kernel,
        out_shape=jax.ShapeDtypeStruct((t, d), BF16),
        grid=(batch, pairs, nq),
        in_specs=[
            pl.BlockSpec((tile * subtiles, LANES), lambda b, j, i: (b * nq + i, j)),
            pl.BlockSpec((seq, LANES), lambda b, j, i: (b, j)),
            pl.BlockSpec((seq, LANES), lambda b, j, i: (b, pairs + j)),
            pl.BlockSpec((CHUNK, CHUNK), lambda b, j, i: (0, 0)),
        ],
        out_specs=pl.BlockSpec((tile * subtiles, LANES), lambda b, j, i: (b * nq + i, j)),
        scratch_shapes=[
            pltpu.VMEM((subtiles, 2 * tile, LANES), F32),
            pltpu.VMEM((subtiles, 2 * tile, LANES), F32),
            pltpu.VMEM((subtiles, SUBLANES, LANES), F32),
            pltpu.VMEM((2 * tile, tile), F32),
            pltpu.VMEM((2 * tile, tile), F32),
            pltpu.VMEM((2 * tile, 2 * tile), F32),
        ],
        compiler_params=_params("parallel", "parallel", "arbitrary"),
        name="stick_attention",
    )(q, kv, kv, _suffix_sum_matrix())


SC_WINDOW = 128


def _sc_gather_rows(tab0, tab1, idx):
    m = idx.shape[0]
    w = tab0.shape[1]
    mesh = plsc.VectorSubcoreMesh(core_axis_name="core", subcore_axis_name="subcore")
    assert m % (SC_WINDOW * mesh.num_cores * mesh.num_subcores) == 0, m
    out = jax.ShapeDtypeStruct((m, w), tab0.dtype)

    @pl.kernel(out_type=(out, out), mesh=mesh)
    def gather(t0_hbm, t1_hbm, i_hbm, o0_hbm, o1_hbm):
        for t_hbm, o_hbm in ((t0_hbm, o0_hbm), (t1_hbm, o1_hbm)):
            def body(i_vmem, o_vmem, t_hbm=t_hbm):
                pltpu.sync_copy(t_hbm.at[i_vmem.at[0]], o_vmem)

            pltpu.emit_pipeline(
                body,
                grid=(m // SC_WINDOW,),
                in_specs=[pl.BlockSpec((1, SC_WINDOW), lambda i: (0, i))],
                out_specs=[pl.BlockSpec((SC_WINDOW, w), lambda i: (i, 0))],
                core_axis_name=("core", "subcore"),
                dimension_semantics=(pltpu.PARALLEL,),
            )(i_hbm, o_hbm)

    return gather(tab0, tab1, idx.reshape(1, m))


def _sc_scatter_rows(tab0, tab1, idx, n_rows):
    m = idx.shape[0]
    t, w = tab0.shape
    mesh = plsc.VectorSubcoreMesh(core_axis_name="core", subcore_axis_name="subcore")
    assert m % (SC_WINDOW * mesh.num_cores * mesh.num_subcores) == 0 and t % SC_WINDOW == 0, (m, t)
    windows_per_pass = t // SC_WINDOW
    out = jax.ShapeDtypeStruct((n_rows, w), tab0.dtype)

    @pl.kernel(out_type=(out, out), mesh=mesh)
    def scatter(t0_hbm, t1_hbm, i_hbm, o0_hbm, o1_hbm):
        for t_hbm, o_hbm in ((t0_hbm, o0_hbm), (t1_hbm, o1_hbm)):
            def body(x_vmem, i_vmem, o_hbm=o_hbm):
                pltpu.sync_copy(x_vmem, o_hbm.at[i_vmem.at[0]])

            pltpu.emit_pipeline(
                body,
                grid=(m // SC_WINDOW,),
                in_specs=[pl.BlockSpec((SC_WINDOW, w), lambda i: (i % windows_per_pass, 0)),
                          pl.BlockSpec((1, SC_WINDOW), lambda i: (0, i))],
                out_specs=[],
                core_axis_name=("core", "subcore"),
                dimension_semantics=(pltpu.PARALLEL,),
            )(t_hbm, i_hbm)

    return scatter(tab0, tab1, idx.reshape(1, m))


def _moe_route_kernel(x_ref, g_ref, sh_ref, sc_ref, rw_ref, rb_ref,
                      h0_ref, h1_ref, idx_ref, rank_ref, gate_ref, count_ref, run_ref):
    @pl.when(pl.program_id(0) == 0)
    def _():
        run_ref[...] = jnp.zeros(run_ref.shape, F32)

    h = _adaln(x_ref[...], g_ref[...], sh_ref[...], sc_ref[...])
    h0_ref[...], h1_ref[...] = _pack_rows(h)
    h_hi = h.astype(BF16)
    h_lo = (h - h_hi.astype(F32)).astype(BF16)
    logits = jnp.dot(jnp.concatenate([h_hi, h_lo, h_hi], axis=1), rw_ref[...],
                     preferred_element_type=F32) + rb_ref[...]
    lane = lax.broadcasted_iota(jnp.int32, logits.shape, 1)
    lane_f = lane.astype(F32)
    vals, ids = [], []
    for _ in range(TOP_K):
        m = jnp.max(logits, axis=-1, keepdims=True)
        first = jnp.min(jnp.where(logits == m, lane_f, float(LANES)), axis=-1, keepdims=True)
        vals.append(m)
        ids.append(first)
        logits = jnp.where(lane_f == first, -jnp.inf, logits)
    exps = [jnp.exp(v - vals[0]) for v in vals]
    denom = exps[0]
    for e in exps[1:]:
        denom = denom + e
    tm = logits.shape[0]
    earlier = (lax.broadcasted_iota(jnp.int32, (tm, tm), 1) < lax.broadcasted_iota(jnp.int32, (tm, tm), 0))
    earlier = jnp.where(earlier, 1.0, 0.0).astype(BF16)
    seen = run_ref[...]
    gate = jnp.zeros(logits.shape, F32)
    idx = jnp.zeros(logits.shape, F32)
    rank = jnp.zeros(logits.shape, F32)
    for k in range(TOP_K):
        onehot = jnp.where(lane_f == ids[k], 1.0, 0.0)
        before = jnp.dot(earlier, onehot.astype(BF16), preferred_element_type=F32) + seen
        rank = jnp.where(lane == k, jnp.sum(onehot * before, axis=-1, keepdims=True), rank)
        seen = seen + jnp.sum(onehot, axis=0, keepdims=True)
        gate = jnp.where(lane == k, exps[k] / denom, gate)
        idx = jnp.where(lane == k, ids[k], idx)
    run_ref[...] = seen
    count_ref[...] = jnp.broadcast_to(seen, count_ref.shape).astype(jnp.int32)
    gate_ref[...] = gate
    idx_ref[...] = idx.T[:SUBLANES].astype(jnp.int32)
    rank_ref[...] = rank.T[:SUBLANES].astype(jnp.int32)


def _moe_route(x, gain, shift, scale, rw_pad, rb_pad, seq):
    t, d = x.shape
    tm = min(TOKEN_TILE, seq)
    per_seq = seq // tm
    b = shift.shape[0]
    return pl.pallas_call(
        _moe_route_kernel,
        out_shape=(jax.ShapeDtypeStruct((t, d // 4), jnp.uint32),
                   jax.ShapeDtypeStruct((t, d // 4), jnp.uint32),
                   jax.ShapeDtypeStruct((SUBLANES, t), jnp.int32),
                   jax.ShapeDtypeStruct((SUBLANES, t), jnp.int32),
                   jax.ShapeDtypeStruct((t, LANES), F32),
                   jax.ShapeDtypeStruct((SUBLANES, LANES), jnp.int32)),
        grid=(t // tm,),
        in_specs=[
            pl.BlockSpec((tm, d), lambda i: (i, 0)),
            pl.BlockSpec((1, d), lambda i: (0, 0)),
            pl.BlockSpec((None, 1, d), lambda i: (i // per_seq, 0, 0)),
            pl.BlockSpec((None, 1, d), lambda i: (i // per_seq, 0, 0)),
            pl.BlockSpec((3 * d, LANES), lambda i: (0, 0)),
            pl.BlockSpec((1, LANES), lambda i: (0, 0)),
        ],
        out_specs=(pl.BlockSpec((tm, d // 4), lambda i: (i, 0)),
                   pl.BlockSpec((tm, d // 4), lambda i: (i, 0)),
                   pl.BlockSpec((SUBLANES, tm), lambda i: (0, i)),
                   pl.BlockSpec((SUBLANES, tm), lambda i: (0, i)),
                   pl.BlockSpec((tm, LANES), lambda i: (i, 0)),
                   pl.BlockSpec((SUBLANES, LANES), lambda i: (0, 0))),
        scratch_shapes=[pltpu.VMEM((1, LANES), F32)],
        compiler_params=_params("arbitrary"),
        name="moe_route",
    )(x, gain.reshape(1, d), shift.reshape(b, 1, d), scale.reshape(b, 1, d), rw_pad, rb_pad)


def _expert_kernel(block_e_ref, n_used_ref, n_valid_ref, h0_ref, h1_ref, wgu_ref, bgu_ref, wd_ref, bd_ref,
                   o0_ref, o1_ref, wgu_bf_ref, wd_bf_ref):
    i = pl.program_id(0)
    new_expert = jnp.logical_or(i == 0, block_e_ref[i] != block_e_ref[jnp.maximum(i - 1, 0)])

    @pl.when(jnp.logical_and(new_expert, i < n_used_ref[0]))
    def _():
        wgu_bf_ref[...] = wgu_ref[...].astype(BF16)
        wd_bf_ref[...] = wd_ref[...].astype(BF16)

    @pl.when(i < n_used_ref[0])
    def _():
        d_e = wd_ref.shape[0]
        written = lax.broadcasted_iota(jnp.int32, h0_ref.shape, 0) < n_valid_ref[i]
        zero = jnp.zeros(h0_ref.shape, h0_ref.dtype)
        quarters = _unpack_rows(jnp.where(written, h0_ref[...], zero), jnp.where(written, h1_ref[...], zero))
        h = jnp.concatenate([q.astype(BF16) for q in quarters], axis=1)
        gu = jnp.dot(h, wgu_bf_ref[...], preferred_element_type=F32) + bgu_ref[...]
        g = jnp.minimum(gu[:, :d_e], SWIGLU_LIMIT)
        u = jnp.clip(gu[:, d_e:], -SWIGLU_LIMIT, SWIGLU_LIMIT)
        act = g / (1.0 + jnp.exp(-SWIGLU_ALPHA * g)) * (u + 1.0)
        y = jnp.dot(act.astype(BF16), wd_bf_ref[...], preferred_element_type=F32) + bd_ref[...]
        o0_ref[...], o1_ref[...] = _pack_rows(y)

    @pl.when(i >= n_used_ref[0])
    def _():
        o0_ref[...] = jnp.zeros(o0_ref.shape, o0_ref.dtype)
        o1_ref[...] = jnp.zeros(o1_ref.shape, o1_ref.dtype)


def _expert_mlp(block_e, n_used, n_valid, rows0, rows1, wgu, bgu, wd, bd, layer, tile):
    n_rows, dq = rows0.shape
    d = 4 * dq
    n_l, n_e, _, two_de = wgu.shape
    d_e = two_de // 2
    n_blocks = n_rows // tile
    grid_spec = pltpu.PrefetchScalarGridSpec(
        num_scalar_prefetch=3,
        grid=(n_blocks,),
        in_specs=[
            pl.BlockSpec((tile, dq), lambda i, be, nu, nv: (i, 0)),
            pl.BlockSpec((tile, dq), lambda i, be, nu, nv: (i, 0)),
            pl.BlockSpec((None, None, d, two_de), lambda i, be, nu, nv: (layer, be[i], 0, 0)),
            pl.BlockSpec((None, None, 1, two_de), lambda i, be, nu, nv: (layer, be[i], 0, 0)),
            pl.BlockSpec((None, None, d_e, d), lambda i, be, nu, nv: (layer, be[i], 0, 0)),
            pl.BlockSpec((None, None, 1, d), lambda i, be, nu, nv: (layer, be[i], 0, 0)),
        ],
        out_specs=(pl.BlockSpec((tile, dq), lambda i, be, nu, nv: (i, 0)),
                   pl.BlockSpec((tile, dq), lambda i, be, nu, nv: (i, 0))),
        scratch_shapes=[pltpu.VMEM((d, two_de), BF16), pltpu.VMEM((d_e, d), BF16)],
    )
    packed = jax.ShapeDtypeStruct((n_rows, dq), jnp.uint32)
    return pl.pallas_call(
        _expert_kernel,
        out_shape=(packed, packed),
        grid_spec=grid_spec,
        compiler_params=_params("arbitrary"),
        name="expert_mlp",
    )(block_e, n_used, n_valid, rows0, rows1, wgu, bgu.reshape(n_l, n_e, 1, two_de), wd,
      bd.reshape(n_l, n_e, 1, d))


def _combine_kernel(y0_ref, y1_ref, w_ref, x_ref, g_ref, n_ref, o_ref, *, normalize):
    w = w_ref[...]
    acc = None
    for k in range(y0_ref.shape[0]):
        quarters = _unpack_rows(y0_ref[k], y1_ref[k])
        scaled = [w[:, k:k + 1] * q for q in quarters]
        acc = scaled if acc is None else [a + s for a, s in zip(acc, scaled)]
    x = x_ref[...] + g_ref[...] * jnp.concatenate(acc, axis=1)
    if normalize:
        x = x * lax.rsqrt(jnp.mean(x * x, axis=-1, keepdims=True) + EPS) * n_ref[...]
    o_ref[...] = x


def _combine(y0, y1, weights, x, gate, seq, final_gain=None):
    t, d = x.shape
    tm = min(TOKEN_TILE, seq)
    per_seq = seq // tm
    b = gate.shape[0]
    norm_gain = jnp.ones((1, d), F32) if final_gain is None else final_gain.reshape(1, d)
    return pl.pallas_call(
        functools.partial(_combine_kernel, normalize=final_gain is not None),
        out_shape=jax.ShapeDtypeStruct((t, d), F32),
        grid=(t // tm,),
        in_specs=[
            pl.BlockSpec((y0.shape[0], tm, d // 4), lambda i: (0, i, 0)),
            pl.BlockSpec((y0.shape[0], tm, d // 4), lambda i: (0, i, 0)),
            pl.BlockSpec((tm, LANES), lambda i: (i, 0)),
            pl.BlockSpec((tm, d), lambda i: (i, 0)),
            pl.BlockSpec((None, 1, d), lambda i: (i // per_seq, 0, 0)),
            pl.BlockSpec((1, d), lambda i: (0, 0)),
        ],
        out_specs=pl.BlockSpec((tm, d), lambda i: (i, 0)),
        compiler_params=_params("parallel"),
        name="moe_combine",
    )(y0, y1, weights, x, gate.reshape(b, 1, d), norm_gain)


ROW_BLOCK_MULTIPLE = 8


def _routing_tables(idx_t, rank_t, counts, tile):
    top_k, t = idx_t.shape
    n_experts = counts.shape[0]
    n_assign = top_k * t
    padded = (counts + tile - 1) // tile * tile
    pend = jnp.cumsum(padded)
    pstart = pend - padded
    e_flat = idx_t.reshape(n_assign)
    onehot = e_flat[:, None] == jnp.arange(n_experts, dtype=jnp.int32)[None, :]
    slot_row = rank_t.reshape(n_assign) + jnp.sum(jnp.where(onehot, pstart[None, :], 0), axis=1,
                                                  dtype=jnp.int32)
    n_blocks = (n_assign + n_experts * (tile - 1) + tile - 1) // tile
    n_blocks = -(-n_blocks // ROW_BLOCK_MULTIPLE) * ROW_BLOCK_MULTIPLE
    first_row = jnp.arange(n_blocks, dtype=jnp.int32) * tile
    block_e = jnp.minimum(jnp.sum(pend[None, :] <= first_row[:, None], axis=1), n_experts - 1)
    block_e = block_e.astype(jnp.int32)
    n_valid = jnp.clip(counts[block_e] - (first_row - pstart[block_e]), 0, tile).astype(jnp.int32)
    n_used = (pend[-1] // tile).astype(jnp.int32).reshape(1)
    return slot_row, block_e, n_valid, n_used, n_blocks * tile


def _moe(x, gain, shift, scale, gate, rw, rb, wgu, bgu, wd, bd, layer, seq, final_gain=None):
    t, d = x.shape
    n_e = rw.shape[1]
    rw_pad = jnp.pad(rw, ((0, 0), (0, LANES - n_e)))
    rw_hi = rw_pad.astype(BF16)
    rw_lo = (rw_pad - rw_hi.astype(F32)).astype(BF16)
    rw_pad = jnp.concatenate([rw_hi, rw_hi, rw_lo], axis=0)
    rb_pad = jnp.pad(rb, (0, LANES - n_e), constant_values=MASK_VALUE).reshape(1, LANES)
    h0, h1, idx_t, rank_t, weights, counts = _moe_route(x, gain, shift, scale, rw_pad, rb_pad, seq)
    tile = min(EXPERT_TILE, t * TOP_K // n_e)
    slot_row, block_e, n_valid, n_used, n_rows = _routing_tables(
        idx_t[:TOP_K], rank_t[:TOP_K], counts[0, :n_e], tile)
    rows0, rows1 = _sc_scatter_rows(h0, h1, slot_row, n_rows)
    ys0, ys1 = _expert_mlp(block_e, n_used, n_valid, rows0, rows1, wgu, bgu, wd, bd, layer, tile)
    y0, y1 = _sc_gather_rows(ys0, ys1, slot_row)
    return _combine(y0.reshape(TOP_K, t, d // 4), y1.reshape(TOP_K, t, d // 4), weights, x, gate, seq,
                    final_gain)


def kernel(x, c, ada_w, ada_b, norm_mix, norm_moe, a_wqkv, a_wo, a_lambda, a_subln, kv_norm,
           kv_ada_w, kv_ada_b, kv_w, b_wq, b_wo, router_w, router_b, w_gate_up, b_gate_up,
           w_down, b_down, final_norm):
    b, s, d = x.shape
    depth = ada_w.shape[0]
    n_a = a_wqkv.shape[0]
    xt = x.reshape(b * s, d)

    c_pad = jnp.pad(c, ((0, SUBLANES - b), (0, 0)))
    mod = _modulation(c_pad, ada_w, ada_b)[:, :b]
    kv_mod = _modulation(c_pad, kv_ada_w[None], kv_ada_b[None])[0, :b]

    for l in range(depth):
        sh1, sc1, g1, sh2, sc2, g2 = jnp.split(mod[l], 6, axis=-1)
        if l < n_a:
            qkv = _norm_linear(xt, norm_mix[l], sh1, sc1, a_wqkv[l].astype(BF16), s)
            o = _diff_attention(qkv, a_lambda[l], a_subln[l], b, s, l)
            xt = _proj_residual(o, a_wo[l].astype(BF16), xt, g1, s)
        else:
            if l == n_a:
                kv_sh, kv_sc = jnp.split(kv_mod, 2, axis=-1)
                kv = _norm_linear(xt, kv_norm, kv_sh, kv_sc, kv_w.astype(BF16), s)
            j = l - n_a
            q = _norm_linear(xt, norm_mix[l], sh1, sc1, b_wq[j].astype(BF16), s)
            o = _stick_attention(q, kv, b, s)
            xt = _proj_residual(o, b_wo[j].astype(BF16), xt, g1, s)
        xt = _moe(xt, norm_moe[l], sh2, sc2, g2, router_w[l], router_b[l],
                  w_gate_up, b_gate_up, w_down, b_down, l, s,
                  final_gain=final_norm if l == depth - 1 else None)
    return xt.reshape(b, s, d)
```

```python
import functools
import math

import jax
import jax.numpy as jnp
import numpy as np
from jax import lax
from jax.experimental import pallas as pl
from jax.experimental.pallas import tpu as pltpu
from jax.experimental.pallas import tpu_sc as plsc

F32 = jnp.float32
BF16 = jnp.bfloat16
EPS = 1e-6
MASK_VALUE = -1e30
TOP_K = 4
SWIGLU_ALPHA = 1.702
SWIGLU_LIMIT = 7.0
LOG2E = math.log2(math.e)
UNDERFLOW_BITS = 160.0
NORM_SLACK = 1.03

LANES = 128
SUBLANES = 8
HEAD_DIM = 64
V7X_VMEM_BYTES = 64 * 2 ** 20
VMEM_LIMIT = V7X_VMEM_BYTES * 7 // 8

TOKEN_TILE = 1024
ATTN_TILE = 512
STICK_TILE = 256
DIFF_SUBTILES = 4
STICK_SUBTILES = 8
EXPERT_TILE = 512


def _params(*semantics):
    return pltpu.CompilerParams(dimension_semantics=semantics, vmem_limit_bytes=VMEM_LIMIT)


def _largest_tile(n, cap):
    t = min(n, cap) // LANES * LANES
    while n % t:
        t -= LANES
    return t


def _adaln(x, gain, shift, scale):
    ms = jnp.mean(x * x, axis=-1, keepdims=True)
    return x * lax.rsqrt(ms + EPS) * gain * (1.0 + scale) + shift


def _pack_bf16_pair(lo, hi):
    lo_bits = lax.bitcast_convert_type(lo.astype(BF16).astype(F32), jnp.uint32) >> 16
    hi_bits = lax.bitcast_convert_type(hi.astype(BF16).astype(F32), jnp.uint32) & jnp.uint32(0xFFFF0000)
    return lo_bits | hi_bits


def _unpack_bf16_pair(packed):
    lo = lax.bitcast_convert_type(packed << 16, F32)
    hi = lax.bitcast_convert_type(packed & jnp.uint32(0xFFFF0000), F32)
    return lo, hi


def _pack_rows(x):
    q = x.shape[1] // 4
    return (_pack_bf16_pair(x[:, :q], x[:, 2 * q:3 * q]),
            _pack_bf16_pair(x[:, q:2 * q], x[:, 3 * q:]))


def _unpack_rows(p0, p1):
    a0, a2 = _unpack_bf16_pair(p0)
    a1, a3 = _unpack_bf16_pair(p1)
    return a0, a1, a2, a3


def _mod_kernel(c_ref, w_ref, b_ref, o_ref):
    c = c_ref[...]
    a = c / (1.0 + jnp.exp(-c))
    o_ref[...] = jnp.dot(a, w_ref[...], precision=lax.Precision.HIGHEST,
                         preferred_element_type=F32) + b_ref[...]


def _modulation(c_pad, w, b):
    n_l, d, n = w.shape
    tn = _largest_tile(n, 2048)
    return pl.pallas_call(
        _mod_kernel,
        out_shape=jax.ShapeDtypeStruct((n_l, SUBLANES, n), F32),
        grid=(n_l, n // tn),
        in_specs=[
            pl.BlockSpec((SUBLANES, d), lambda l, j: (0, 0)),
            pl.BlockSpec((None, d, tn), lambda l, j: (l, 0, j)),
            pl.BlockSpec((None, 1, tn), lambda l, j: (l, 0, j)),
        ],
        out_specs=pl.BlockSpec((None, SUBLANES, tn), lambda l, j: (l, 0, j)),
        compiler_params=_params("parallel", "parallel"),
        name="modulation",
    )(c_pad, w, b.reshape(n_l, 1, n))


def _norm_linear_kernel(x_ref, g_ref, sh_ref, sc_ref, w_ref, o_ref):
    h = _adaln(x_ref[...], g_ref[...], sh_ref[...], sc_ref[...])
    o_ref[...] = jnp.dot(h.astype(BF16), w_ref[...], preferred_element_type=F32).astype(o_ref.dtype)


def _norm_linear(x, gain, shift, scale, w, seq):
    t, d = x.shape
    n = w.shape[1]
    tm = min(TOKEN_TILE, seq)
    per_seq = seq // tm
    b = shift.shape[0]
    return pl.pallas_call(
        _norm_linear_kernel,
        out_shape=jax.ShapeDtypeStruct((t, n), BF16),
        grid=(t // tm,),
        in_specs=[
            pl.BlockSpec((tm, d), lambda i: (i, 0)),
            pl.BlockSpec((1, d), lambda i: (0, 0)),
            pl.BlockSpec((None, 1, d), lambda i: (i // per_seq, 0, 0)),
            pl.BlockSpec((None, 1, d), lambda i: (i // per_seq, 0, 0)),
            pl.BlockSpec((d, n), lambda i: (0, 0)),
        ],
        out_specs=pl.BlockSpec((tm, n), lambda i: (i, 0)),
        compiler_params=_params("parallel"),
        name="norm_linear",
    )(x, gain.reshape(1, d), shift.reshape(b, 1, d), scale.reshape(b, 1, d), w)


def _proj_residual_kernel(a_ref, w_ref, x_ref, g_ref, o_ref):
    y = jnp.dot(a_ref[...], w_ref[...], preferred_element_type=F32)
    o_ref[...] = x_ref[...] + g_ref[...] * y


def _proj_residual(a, w, x, gate, seq):
    t, d = x.shape
    k = a.shape[1]
    tm = min(TOKEN_TILE, seq)
    per_seq = seq // tm
    b = gate.shape[0]
    return pl.pallas_call(
        _proj_residual_kernel,
        out_shape=jax.ShapeDtypeStruct((t, d), F32),
        grid=(t // tm,),
        in_specs=[
            pl.BlockSpec((tm, k), lambda i: (i, 0)),
            pl.BlockSpec((k, d), lambda i: (0, 0)),
            pl.BlockSpec((tm, d), lambda i: (i, 0)),
            pl.BlockSpec((None, 1, d), lambda i: (i // per_seq, 0, 0)),
        ],
        out_specs=pl.BlockSpec((tm, d), lambda i: (i, 0)),
        compiler_params=_params("parallel"),
        name="proj_residual",
    )(a, w, x, gate.reshape(b, 1, d))


def _scaled_halves(q):
    q = (q.astype(F32) * (HEAD_DIM ** -0.5 * LOG2E)).astype(BF16)
    lane = lax.broadcasted_iota(jnp.int32, q.shape, 1)
    zero = jnp.zeros_like(q)
    return jnp.concatenate([jnp.where(lane < HEAD_DIM, q, zero),
                            jnp.where(lane >= HEAD_DIM, q, zero)], axis=0)


def _dot_nt(a, b):
    return lax.dot_general(a, b, (((1,), (1,)), ((), ())), preferred_element_type=F32)


def _tile_row_col(tq, tk):
    row = lax.broadcasted_iota(jnp.int32, (2 * tq, tk), 0)
    row = jnp.where(row >= tq, row - tq, row)
    col = lax.broadcasted_iota(jnp.int32, (2 * tq, tk), 1)
    return row, col


def _lane_chunks(n):
    return [slice(c * LANES, (c + 1) * LANES) for c in range(n // LANES)]


def _diff_attn_kernel(q_ref, k_ref, v_ref, slope_ref, lam_ref, subln_ref, o_ref,
                      m_ref, l_ref, acc_ref, sa_ref, sb_ref, kmax_ref, *, tile, subtiles, lam_init):
    step = pl.program_id(2)

    def one_tile(sub, carry):
        i = step * subtiles + sub
        rows = pl.ds(pl.multiple_of(sub * tile, tile), tile)
        qq = _scaled_halves(q_ref[rows, :])
        slope = slope_ref[...]
        col_i = lax.broadcasted_iota(jnp.int32, (1, tile), 1)
        ones = jnp.ones((tile, LANES), BF16)
        m_ref[...] = jnp.full(m_ref.shape, MASK_VALUE, F32)
        l_ref[...] = jnp.zeros(l_ref.shape, F32)
        acc_ref[...] = jnp.zeros(acc_ref.shape, F32)

        def scores(s_ref, kb):
            k = k_ref[pl.ds(pl.multiple_of(kb * tile, tile), tile), :]
            s_ref[...] = _dot_nt(qq, k) + slope * (col_i + (kb - i) * tile).astype(F32)

        def accumulate(s_ref, kb, diagonal):
            v = v_ref[pl.ds(pl.multiple_of(kb * tile, tile), tile), :]
            if diagonal:
                row, col = _tile_row_col(tile, tile)
                s_ref[...] = jnp.where(col <= row, s_ref[...], MASK_VALUE)
            m_prev = m_ref[...]
            m_new = jnp.maximum(m_prev, jnp.max(s_ref[...], axis=-1, keepdims=True))
            alpha = jnp.exp2(m_prev - m_new)
            p = jnp.concatenate([jnp.exp2(s_ref[:, sl] - m_new).astype(BF16) for sl in _lane_chunks(tile)],
                                axis=1)
            pv = jnp.dot(p, jnp.concatenate([v, ones], axis=1),
                         preferred_element_type=F32)
            acc_ref[...] = alpha * acc_ref[...] + pv[:, :LANES]
            l_ref[...] = alpha * l_ref[...] + pv[:, LANES:]
            m_ref[...] = m_new

        lane = lax.broadcasted_iota(jnp.int32, (1, LANES), 1)
        half_ones = (lax.broadcasted_iota(jnp.int32, (LANES, LANES), 0) < HEAD_DIM) == (
            lax.broadcasted_iota(jnp.int32, (LANES, LANES), 1) < HEAD_DIM)
        half_ones = jnp.where(half_ones, 1.0, 0.0).astype(BF16)

        @pl.when(i == 0)
        def _():
            def chunk_max(c, best):
                k = k_ref[pl.ds(pl.multiple_of(c * tile, tile), tile), :].astype(F32)
                n2 = jnp.dot((k * k).astype(BF16), half_ones, preferred_element_type=F32)
                return jnp.maximum(best, jnp.max(n2, axis=0, keepdims=True))

            best = lax.fori_loop(0, k_ref.shape[0] // tile, chunk_max, jnp.zeros((1, LANES), F32))
            kmax_ref[...] = jnp.broadcast_to(best * NORM_SLACK, kmax_ref.shape)

        qf = qq.astype(F32)
        qn2 = jnp.dot((qf * qf).astype(BF16), jnp.ones((LANES, LANES), BF16), preferred_element_type=F32)
        q_half = jnp.where(lane < HEAD_DIM, jnp.max(qn2[:tile], axis=0, keepdims=True),
                           jnp.max(qn2[tile:], axis=0, keepdims=True))
        raw_bound = jnp.sqrt(jnp.max(q_half * NORM_SLACK * kmax_ref[0:1, :], axis=1, keepdims=True))

        def live(kb):
            far = jnp.zeros((1, LANES), jnp.int32) + ((kb + 1 - i) * tile - 1)
            best_left = raw_bound + slope[:, :LANES] * far.astype(F32)
            lowest_max = jnp.min(m_ref[...], axis=0, keepdims=True)
            return jnp.max(best_left - lowest_max) > -UNDERFLOW_BITS

        scores(sa_ref, i)

        @pl.when(i == 0)
        def _():
            accumulate(sa_ref, 0, True)

        @pl.when(i == 1)
        def _():
            scores(sb_ref, 0)
            accumulate(sa_ref, 1, True)
            accumulate(sb_ref, 0, False)

        @pl.when(i >= 2)
        def _():
            scores(sb_ref, i - 1)
            accumulate(sa_ref, i, True)
            scores(sa_ref, i - 2)
            accumulate(sb_ref, i - 1, False)
            n_pairs = (i - 2) // 2

            def cond(carry):
                j, go = carry
                return jnp.logical_and(j < n_pairs, go)

            def body(carry):
                j, _ = carry
                kb = i - 2 - 2 * j
                scores(sb_ref, kb - 1)
                accumulate(sa_ref, kb, False)
                scores(sa_ref, kb - 2)
                accumulate(sb_ref, kb - 1, False)
                return j + 1, live(kb - 2)

            _, go = lax.while_loop(cond, body, (jnp.int32(0), live(i - 2)))

            @pl.when(jnp.logical_and(go, i % 2 == 0))
            def _():
                accumulate(sa_ref, 0, False)

            @pl.when(jnp.logical_and(go, i % 2 == 1))
            def _():
                scores(sb_ref, 0)
                accumulate(sa_ref, 1, False)
                accumulate(sb_ref, 0, False)

        o_all = acc_ref[...] / l_ref[...]
        lv = lam_ref[...]
        lam = (jnp.exp(jnp.sum(lv[0:1] * lv[1:2], axis=1, keepdims=True))
               - jnp.exp(jnp.sum(lv[2:3] * lv[3:4], axis=1, keepdims=True)) + lam_init)
        o = o_all[:tile] - lam * o_all[tile:]
        ms = jnp.mean(o * o, axis=-1, keepdims=True)
        o = o * lax.rsqrt(ms + EPS) * subln_ref[...] * (1.0 - lam_init)
        o_ref[rows, :] = o.astype(o_ref.dtype)
        return carry

    lax.fori_loop(0, subtiles, one_tile, 0)


def _diff_attention(qkv, lam_vecs, subln, batch, seq, layer_idx):
    t = qkv.shape[0]
    d = qkv.shape[1] // 3
    heads = d // LANES
    tile = min(ATTN_TILE, seq)
    subtiles = min(DIFF_SUBTILES, seq // tile)
    nq = seq // (tile * subtiles)
    lam_init = 0.8 - 0.6 * math.exp(-0.3 * layer_idx)
    slopes = np.array([2.0 ** (-8.0 * (h + 1) / heads) * LOG2E for h in range(heads)], np.float32)
    slopes = jnp.asarray(np.broadcast_to(slopes[:, None, None], (heads, 1, tile)))
    kernel = functools.partial(_diff_attn_kernel, tile=tile, subtiles=subtiles, lam_init=lam_init)
    return pl.pallas_call(
        kernel,
        out_shape=jax.ShapeDtypeStruct((t, d), BF16),
        grid=(batch, heads, nq),
        in_specs=[
            pl.BlockSpec((tile * subtiles, LANES), lambda b, h, i: (b * nq + i, h)),
            pl.BlockSpec((seq, LANES), lambda b, h, i: (b, heads + h)),
            pl.BlockSpec((seq, LANES), lambda b, h, i: (b, 2 * heads + h)),
            pl.BlockSpec((None, 1, tile), lambda b, h, i: (h, 0, 0)),
            pl.BlockSpec(lam_vecs.shape, lambda b, h, i: (0, 0)),
            pl.BlockSpec((1, LANES), lambda b, h, i: (0, 0)),
        ],
        out_specs=pl.BlockSpec((tile * subtiles, LANES), lambda b, h, i: (b * nq + i, h)),
        scratch_shapes=[
            pltpu.VMEM((2 * tile, LANES), F32),
            pltpu.VMEM((2 * tile, LANES), F32),
            pltpu.VMEM((2 * tile, LANES), F32),
            pltpu.VMEM((2 * tile, tile), F32),
            pltpu.VMEM((2 * tile, tile), F32),
            pltpu.VMEM((SUBLANES, LANES), F32),
        ],
        compiler_params=_params("parallel", "parallel", "arbitrary"),
        name="diff_attention",
    )(qkv, qkv, qkv, slopes, lam_vecs, subln.reshape(1, LANES))


CHUNK = 256


def _suffix_sum_matrix():
    j = np.arange(CHUNK)
    return jnp.asarray((j[:, None] >= j[None, :]).astype(np.float32), dtype=BF16)


def _stick_attn_kernel(q_ref, k_ref, v_ref, mt_ref, o_ref, rem_ref, acc_ref, za_ref, zb_ref, zw_ref,
                       *, tile, subtiles):
    step = pl.program_id(2)

    def one_tile(sub, carry):
        i = step * subtiles + sub
        rows = pl.ds(pl.multiple_of(sub * tile, tile), tile)
        qq = _scaled_halves(q_ref[rows, :])
        mt = mt_ref[...]
        rem_ref[...] = jnp.zeros(rem_ref.shape, F32)
        acc_ref[...] = jnp.zeros(acc_ref.shape, F32)

        def scores(z_ref, kb, blocks=1):
            keys = pl.ds(pl.multiple_of(kb * tile, tile), blocks * tile)
            z_ref[...] = _dot_nt(qq, k_ref[keys, :])

        def accumulate(z_ref, kb, diagonal, blocks=1):
            v = v_ref[pl.ds(pl.multiple_of(kb * tile, tile), blocks * tile), :]
            rem = rem_ref[...]
            n_chunks = blocks * tile // CHUNK
            diag_from = (blocks - 1) * tile
            chunks = []
            for c in reversed(range(n_chunks)):
                z = z_ref[:, c * CHUNK:(c + 1) * CHUNK]
                softplus = jnp.maximum(z, 0.0) + jnp.log2(1.0 + jnp.exp2(-jnp.abs(z)))
                masked = diagonal and (c + 1) * CHUNK > diag_from
                if masked:
                    row = lax.broadcasted_iota(jnp.int32, z.shape, 0)
                    row = jnp.where(row >= tile, row - tile, row)
                    strict = lax.broadcasted_iota(jnp.int32, z.shape, 1) + (c * CHUNK - diag_from) < row
                    softplus = jnp.where(strict, softplus, 0.0)
                sums = jnp.dot(softplus.astype(BF16), mt, preferred_element_type=F32)
                for h in reversed(range(CHUNK // LANES)):
                    sl = slice(h * LANES, (h + 1) * LANES)
                    a = jnp.exp2(z_ref[:, c * CHUNK + h * LANES:c * CHUNK + (h + 1) * LANES] - sums[:, sl] - rem)
                    if masked:
                        a = jnp.where(strict[:, sl], a, 0.0)
                    chunks.append(a.astype(BF16))
                rem = rem + jnp.broadcast_to(sums[:, 0:1], rem.shape)
            rem_ref[...] = rem
            acc_ref[...] += jnp.dot(jnp.concatenate(chunks[::-1], axis=1), v, preferred_element_type=F32)

        def live():
            return jnp.min(rem_ref[...]) < UNDERFLOW_BITS

        @pl.when(i == 0)
        def _():
            scores(za_ref, 0)
            accumulate(za_ref, 0, True)

        @pl.when(i >= 1)
        def _():
            scores(zw_ref, i - 1, blocks=2)
            accumulate(zw_ref, i - 1, True, blocks=2)

        @pl.when(jnp.logical_and(i >= 2, live()))
        def _():
            scores(za_ref, i - 2)
            n_pairs = (i - 2) // 2

            def cond(carry):
                j, go = carry
                return jnp.logical_and(j < n_pairs, go)

            def body(carry):
                j, _ = carry
                kb = i - 2 - 2 * j
                scores(zb_ref, kb - 1)
                accumulate(za_ref, kb, False)
                scores(za_ref, kb - 2)
                accumulate(zb_ref, kb - 1, False)
                return j + 1, live()

            _, go = lax.while_loop(cond, body, (jnp.int32(0), jnp.bool_(True)))

            @pl.when(jnp.logical_and(go, i % 2 == 0))
            def _():
                accumulate(za_ref, 0, False)

            @pl.when(jnp.logical_and(go, i % 2 == 1))
            def _():
                scores(zb_ref, 0)
                accumulate(za_ref, 1, False)
                accumulate(zb_ref, 0, False)

        acc = acc_ref[...]
        lane = lax.broadcasted_iota(jnp.int32, (tile, LANES), 1)
        o_ref[rows, :] = jnp.where(lane < HEAD_DIM, acc[:tile], acc[tile:]).astype(o_ref.dtype)
        return carry

    lax.fori_loop(0, subtiles, one_tile, 0)


def _stick_attention(q, kv, batch, seq):
    t, d = q.shape
    pairs = d // LANES
    tile = min(STICK_TILE, seq)
    subtiles = min(STICK_SUBTILES, seq // tile)
    nq = seq // (tile * subtiles)
    kernel = functools.partial(_stick_attn_kernel, tile=tile, subtiles=subtiles)
    return pl.pallas_call(
        kernel,
        out_shape=jax.ShapeDtypeStruct((t, d), BF16),
        grid=(batch, pairs, nq),
        in_specs=[
            pl.BlockSpec((tile * subtiles, LANES), lambda b, j, i: (b * nq + i, j)),
            pl.BlockSpec((seq, LANES), lambda b, j, i: (b, j)),
            pl.BlockSpec((seq, LANES), lambda b, j, i: (b, pairs + j)),
            pl.BlockSpec((CHUNK, CHUNK), lambda b, j, i: (0, 0)),
        ],
        out_specs=pl.BlockSpec((tile * subtiles, LANES), lambda b, j, i: (b * nq + i, j)),
        scratch_shapes=[
            pltpu.VMEM((2 * tile, LANES), F32),
            pltpu.VMEM((2 * tile, LANES), F32),
            pltpu.VMEM((2 * tile, tile), F32),
            pltpu.VMEM((2 * tile, tile), F32),
            pltpu.VMEM((2 * tile, 2 * tile), F32),
        ],
        compiler_params=_params("parallel", "parallel", "arbitrary"),
        name="stick_attention",
    )(q, kv, kv, _suffix_sum_matrix())


SC_WINDOW = 128


def _sc_gather_rows(tab0, tab1, idx):
    m = idx.shape[0]
    w = tab0.shape[1]
    mesh = plsc.VectorSubcoreMesh(core_axis_name="core", subcore_axis_name="subcore")
    assert m % (SC_WINDOW * mesh.num_cores * mesh.num_subcores) == 0, m
    out = jax.ShapeDtypeStruct((m, w), tab0.dtype)

    @pl.kernel(out_type=(out, out), mesh=mesh)
    def gather(t0_hbm, t1_hbm, i_hbm, o0_hbm, o1_hbm):
        for t_hbm, o_hbm in ((t0_hbm, o0_hbm), (t1_hbm, o1_hbm)):
            def body(i_vmem, o_vmem, t_hbm=t_hbm):
                pltpu.sync_copy(t_hbm.at[i_vmem.at[0]], o_vmem)

            pltpu.emit_pipeline(
                body,
                grid=(m // SC_WINDOW,),
                in_specs=[pl.BlockSpec((1, SC_WINDOW), lambda i: (0, i))],
                out_specs=[pl.BlockSpec((SC_WINDOW, w), lambda i: (i, 0))],
                core_axis_name=("core", "subcore"),
                dimension_semantics=(pltpu.PARALLEL,),
            )(i_hbm, o_hbm)

    return gather(tab0, tab1, idx.reshape(1, m))


def _sc_scatter_rows(tab0, tab1, idx, n_rows):
    m = idx.shape[0]
    t, w = tab0.shape
    mesh = plsc.VectorSubcoreMesh(core_axis_name="core", subcore_axis_name="subcore")
    assert m % (SC_WINDOW * mesh.num_cores * mesh.num_subcores) == 0 and t % SC_WINDOW == 0, (m, t)
    windows_per_pass = t // SC_WINDOW
    out = jax.ShapeDtypeStruct((n_rows, w), tab0.dtype)

    @pl.kernel(out_type=(out, out), mesh=mesh)
    def scatter(t0_hbm, t1_hbm, i_hbm, o0_hbm, o1_hbm):
        for t_hbm, o_hbm in ((t0_hbm, o0_hbm), (t1_hbm, o1_hbm)):
            def body(x_vmem, i_vmem, o_hbm=o_hbm):
                pltpu.sync_copy(x_vmem, o_hbm.at[i_vmem.at[0]])

            pltpu.emit_pipeline(
                body,
                grid=(m // SC_WINDOW,),
                in_specs=[pl.BlockSpec((SC_WINDOW, w), lambda i: (i % windows_per_pass, 0)),
                          pl.BlockSpec((1, SC_WINDOW), lambda i: (0, i))],
                out_specs=[],
                core_axis_name=("core", "subcore"),
                dimension_semantics=(pltpu.PARALLEL,),
            )(t_hbm, i_hbm)

    return scatter(tab0, tab1, idx.reshape(1, m))


def _moe_route_kernel(x_ref, g_ref, sh_ref, sc_ref, rw_ref, rb_ref,
                      h0_ref, h1_ref, idx_ref, rank_ref, gate_ref, count_ref, run_ref):
    @pl.when(pl.program_id(0) == 0)
    def _():
        run_ref[...] = jnp.zeros(run_ref.shape, F32)

    h = _adaln(x_ref[...], g_ref[...], sh_ref[...], sc_ref[...])
    h0_ref[...], h1_ref[...] = _pack_rows(h)
    h_hi = h.astype(BF16)
    h_lo = (h - h_hi.astype(F32)).astype(BF16)
    logits = jnp.dot(jnp.concatenate([h_hi, h_lo, h_hi], axis=1), rw_ref[...],
                     preferred_element_type=F32) + rb_ref[...]
    lane = lax.broadcasted_iota(jnp.int32, logits.shape, 1)
    lane_f = lane.astype(F32)
    vals, ids = [], []
    for _ in range(TOP_K):
        m = jnp.max(logits, axis=-1, keepdims=True)
        first = jnp.min(jnp.where(logits == m, lane_f, float(LANES)), axis=-1, keepdims=True)
        vals.append(m)
        ids.append(first)
        logits = jnp.where(lane_f == first, -jnp.inf, logits)
    exps = [jnp.exp(v - vals[0]) for v in vals]
    denom = exps[0]
    for e in exps[1:]:
        denom = denom + e
    tm = logits.shape[0]
    earlier = (lax.broadcasted_iota(jnp.int32, (tm, tm), 1) < lax.broadcasted_iota(jnp.int32, (tm, tm), 0))
    earlier = jnp.where(earlier, 1.0, 0.0).astype(BF16)
    seen = run_ref[...]
    gate = jnp.zeros(logits.shape, F32)
    idx = jnp.zeros(logits.shape, F32)
    rank = jnp.zeros(logits.shape, F32)
    for k in range(TOP_K):
        onehot = jnp.where(lane_f == ids[k], 1.0, 0.0)
        before = jnp.dot(earlier, onehot.astype(BF16), preferred_element_type=F32) + seen
        rank = jnp.where(lane == k, jnp.sum(onehot * before, axis=-1, keepdims=True), rank)
        seen = seen + jnp.sum(onehot, axis=0, keepdims=True)
        gate = jnp.where(lane == k, exps[k] / denom, gate)
        idx = jnp.where(lane == k, ids[k], idx)
    run_ref[...] = seen
    count_ref[...] = jnp.broadcast_to(seen, count_ref.shape).astype(jnp.int32)
    gate_ref[...] = gate
    idx_ref[...] = idx.T[:SUBLANES].astype(jnp.int32)
    rank_ref[...] = rank.T[:SUBLANES].astype(jnp.int32)


def _moe_route(x, gain, shift, scale, rw_pad, rb_pad, seq):
    t, d = x.shape
    tm = min(TOKEN_TILE, seq)
    per_seq = seq // tm
    b = shift.shape[0]
    return pl.pallas_call(
        _moe_route_kernel,
        out_shape=(jax.ShapeDtypeStruct((t, d // 4), jnp.uint32),
                   jax.ShapeDtypeStruct((t, d // 4), jnp.uint32),
                   jax.ShapeDtypeStruct((SUBLANES, t), jnp.int32),
                   jax.ShapeDtypeStruct((SUBLANES, t), jnp.int32),
                   jax.ShapeDtypeStruct((t, LANES), F32),
                   jax.ShapeDtypeStruct((SUBLANES, LANES), jnp.int32)),
        grid=(t // tm,),
        in_specs=[
            pl.BlockSpec((tm, d), lambda i: (i, 0)),
            pl.BlockSpec((1, d), lambda i: (0, 0)),
            pl.BlockSpec((None, 1, d), lambda i: (i // per_seq, 0, 0)),
            pl.BlockSpec((None, 1, d), lambda i: (i // per_seq, 0, 0)),
            pl.BlockSpec((3 * d, LANES), lambda i: (0, 0)),
            pl.BlockSpec((1, LANES), lambda i: (0, 0)),
        ],
        out_specs=(pl.BlockSpec((tm, d // 4), lambda i: (i, 0)),
                   pl.BlockSpec((tm, d // 4), lambda i: (i, 0)),
                   pl.BlockSpec((SUBLANES, tm), lambda i: (0, i)),
                   pl.BlockSpec((SUBLANES, tm), lambda i: (0, i)),
                   pl.BlockSpec((tm, LANES), lambda i: (i, 0)),
                   pl.BlockSpec((SUBLANES, LANES), lambda i: (0, 0))),
        scratch_shapes=[pltpu.VMEM((1, LANES), F32)],
        compiler_params=_params("arbitrary"),
        name="moe_route",
    )(x, gain.reshape(1, d), shift.reshape(b, 1, d), scale.reshape(b, 1, d), rw_pad, rb_pad)


def _expert_kernel(block_e_ref, n_used_ref, n_valid_ref, h0_ref, h1_ref, wgu_ref, bgu_ref, wd_ref, bd_ref,
                   o0_ref, o1_ref):
    i = pl.program_id(0)

    @pl.when(i < n_used_ref[0])
    def _():
        d_e = wd_ref.shape[0]
        written = lax.broadcasted_iota(jnp.int32, h0_ref.shape, 0) < n_valid_ref[i]
        zero = jnp.zeros(h0_ref.shape, h0_ref.dtype)
        quarters = _unpack_rows(jnp.where(written, h0_ref[...], zero), jnp.where(written, h1_ref[...], zero))
        h = jnp.concatenate([q.astype(BF16) for q in quarters], axis=1)
        gu = jnp.dot(h, wgu_ref[...].astype(BF16), preferred_element_type=F32) + bgu_ref[...]
        g = jnp.minimum(gu[:, :d_e], SWIGLU_LIMIT)
        u = jnp.clip(gu[:, d_e:], -SWIGLU_LIMIT, SWIGLU_LIMIT)
        act = g / (1.0 + jnp.exp(-SWIGLU_ALPHA * g)) * (u + 1.0)
        y = jnp.dot(act.astype(BF16), wd_ref[...].astype(BF16), preferred_element_type=F32) + bd_ref[...]
        o0_ref[...], o1_ref[...] = _pack_rows(y)

    @pl.when(i >= n_used_ref[0])
    def _():
        o0_ref[...] = jnp.zeros(o0_ref.shape, o0_ref.dtype)
        o1_ref[...] = jnp.zeros(o1_ref.shape, o1_ref.dtype)


def _expert_mlp(block_e, n_used, n_valid, rows0, rows1, wgu, bgu, wd, bd, layer, tile):
    n_rows, dq = rows0.shape
    d = 4 * dq
    n_l, n_e, _, two_de = wgu.shape
    d_e = two_de // 2
    n_blocks = n_rows // tile
    grid_spec = pltpu.PrefetchScalarGridSpec(
        num_scalar_prefetch=3,
        grid=(n_blocks,),
        in_specs=[
            pl.BlockSpec((tile, dq), lambda i, be, nu, nv: (i, 0)),
            pl.BlockSpec((tile, dq), lambda i, be, nu, nv: (i, 0)),
            pl.BlockSpec((None, None, d, two_de), lambda i, be, nu, nv: (layer, be[i], 0, 0)),
            pl.BlockSpec((None, None, 1, two_de), lambda i, be, nu, nv: (layer, be[i], 0, 0)),
            pl.BlockSpec((None, None, d_e, d), lambda i, be, nu, nv: (layer, be[i], 0, 0)),
            pl.BlockSpec((None, None, 1, d), lambda i, be, nu, nv: (layer, be[i], 0, 0)),
        ],
        out_specs=(pl.BlockSpec((tile, dq), lambda i, be, nu, nv: (i, 0)),
                   pl.BlockSpec((tile, dq), lambda i, be, nu, nv: (i, 0))),
    )
    packed = jax.ShapeDtypeStruct((n_rows, dq), jnp.uint32)
    return pl.pallas_call(
        _expert_kernel,
        out_shape=(packed, packed),
        grid_spec=grid_spec,
        compiler_params=_params("arbitrary"),
        name="expert_mlp",
    )(block_e, n_used, n_valid, rows0, rows1, wgu, bgu.reshape(n_l, n_e, 1, two_de), wd,
      bd.reshape(n_l, n_e, 1, d))


def _combine_kernel(y0_ref, y1_ref, w_ref, x_ref, g_ref, n_ref, o_ref, *, normalize):
    w = w_ref[...]
    acc = None
    for k in range(y0_ref.shape[0]):
        quarters = _unpack_rows(y0_ref[k], y1_ref[k])
        scaled = [w[:, k:k + 1] * q for q in quarters]
        acc = scaled if acc is None else [a + s for a, s in zip(acc, scaled)]
    x = x_ref[...] + g_ref[...] * jnp.concatenate(acc, axis=1)
    if normalize:
        x = x * lax.rsqrt(jnp.mean(x * x, axis=-1, keepdims=True) + EPS) * n_ref[...]
    o_ref[...] = x


def _combine(y0, y1, weights, x, gate, seq, final_gain=None):
    t, d = x.shape
    tm = min(TOKEN_TILE, seq)
    per_seq = seq // tm
    b = gate.shape[0]
    norm_gain = jnp.ones((1, d), F32) if final_gain is None else final_gain.reshape(1, d)
    return pl.pallas_call(
        functools.partial(_combine_kernel, normalize=final_gain is not None),
        out_shape=jax.ShapeDtypeStruct((t, d), F32),
        grid=(t // tm,),
        in_specs=[
            pl.BlockSpec((y0.shape[0], tm, d // 4), lambda i: (0, i, 0)),
            pl.BlockSpec((y0.shape[0], tm, d // 4), lambda i: (0, i, 0)),
            pl.BlockSpec((tm, LANES), lambda i: (i, 0)),
            pl.BlockSpec((tm, d), lambda i: (i, 0)),
            pl.BlockSpec((None, 1, d), lambda i: (i // per_seq, 0, 0)),
            pl.BlockSpec((1, d), lambda i: (0, 0)),
        ],
        out_specs=pl.BlockSpec((tm, d), lambda i: (i, 0)),
        compiler_params=_params("parallel"),
        name="moe_combine",
    )(y0, y1, weights, x, gate.reshape(b, 1, d), norm_gain)


ROW_BLOCK_MULTIPLE = 8


def _routing_tables(idx_t, rank_t, counts, tile):
    top_k, t = idx_t.shape
    n_experts = counts.shape[0]
    n_assign = top_k * t
    padded = (counts + tile - 1) // tile * tile
    pend = jnp.cumsum(padded)
    pstart = pend - padded
    e_flat = idx_t.reshape(n_assign)
    onehot = e_flat[:, None] == jnp.arange(n_experts, dtype=jnp.int32)[None, :]
    slot_row = rank_t.reshape(n_assign) + jnp.sum(jnp.where(onehot, pstart[None, :], 0), axis=1,
                                                  dtype=jnp.int32)
    n_blocks = (n_assign + n_experts * (tile - 1) + tile - 1) // tile
    n_blocks = -(-n_blocks // ROW_BLOCK_MULTIPLE) * ROW_BLOCK_MULTIPLE
    first_row = jnp.arange(n_blocks, dtype=jnp.int32) * tile
    block_e = jnp.minimum(jnp.sum(pend[None, :] <= first_row[:, None], axis=1), n_experts - 1)
    block_e = block_e.astype(jnp.int32)
    n_valid = jnp.clip(counts[block_e] - (first_row - pstart[block_e]), 0, tile).astype(jnp.int32)
    n_used = (pend[-1] // tile).astype(jnp.int32).reshape(1)
    return slot_row, block_e, n_valid, n_used, n_blocks * tile


def _moe(x, gain, shift, scale, gate, rw, rb, wgu, bgu, wd, bd, layer, seq, final_gain=None):
    t, d = x.shape
    n_e = rw.shape[1]
    rw_pad = jnp.pad(rw, ((0, 0), (0, LANES - n_e)))
    rw_hi = rw_pad.astype(BF16)
    rw_lo = (rw_pad - rw_hi.astype(F32)).astype(BF16)
    rw_pad = jnp.concatenate([rw_hi, rw_hi, rw_lo], axis=0)
    rb_pad = jnp.pad(rb, (0, LANES - n_e), constant_values=MASK_VALUE).reshape(1, LANES)
    h0, h1, idx_t, rank_t, weights, counts = _moe_route(x, gain, shift, scale, rw_pad, rb_pad, seq)
    tile = min(EXPERT_TILE, t * TOP_K // n_e)
    slot_row, block_e, n_valid, n_used, n_rows = _routing_tables(
        idx_t[:TOP_K], rank_t[:TOP_K], counts[0, :n_e], tile)
    rows0, rows1 = _sc_scatter_rows(h0, h1, slot_row, n_rows)
    ys0, ys1 = _expert_mlp(block_e, n_used, n_valid, rows0, rows1, wgu, bgu, wd, bd, layer, tile)
    y0, y1 = _sc_gather_rows(ys0, ys1, slot_row)
    return _combine(y0.reshape(TOP_K, t, d // 4), y1.reshape(TOP_K, t, d // 4), weights, x, gate, seq,
                    final_gain)


def kernel(x, c, ada_w, ada_b, norm_mix, norm_moe, a_wqkv, a_wo, a_lambda, a_subln, kv_norm,
           kv_ada_w, kv_ada_b, kv_w, b_wq, b_wo, router_w, router_b, w_gate_up, b_gate_up,
           w_down, b_down, final_norm):
    b, s, d = x.shape
    depth = ada_w.shape[0]
    n_a = a_wqkv.shape[0]
    xt = x.reshape(b * s, d)

    c_pad = jnp.pad(c, ((0, SUBLANES - b), (0, 0)))
    mod = _modulation(c_pad, ada_w, ada_b)[:, :b]
    kv_mod = _modulation(c_pad, kv_ada_w[None], kv_ada_b[None])[0, :b]

    for l in range(depth):
        sh1, sc1, g1, sh2, sc2, g2 = jnp.split(mod[l], 6, axis=-1)
        if l < n_a:
            qkv = _norm_linear(xt, norm_mix[l], sh1, sc1, a_wqkv[l].astype(BF16), s)
            o = _diff_attention(qkv, a_lambda[l], a_subln[l], b, s, l)
            xt = _proj_residual(o, a_wo[l].astype(BF16), xt, g1, s)
        else:
            if l == n_a:
                kv_sh, kv_sc = jnp.split(kv_mod, 2, axis=-1)
                kv = _norm_linear(xt, kv_norm, kv_sh, kv_sc, kv_w.astype(BF16), s)
            j = l - n_a
            q = _norm_linear(xt, norm_mix[l], sh1, sc1, b_wq[j].astype(BF16), s)
            o = _stick_attention(q, kv, b, s)
            xt = _proj_residual(o, b_wo[j].astype(BF16), xt, g1, s)
        xt = _moe(xt, norm_moe[l], sh2, sc2, g2, router_w[l], router_b[l],
                  w_gate_up, b_gate_up, w_down, b_down, l, s,
                  final_gain=final_norm if l == depth - 1 else None)
    return xt.reshape(b, s, d)
```
